```python
import math
import jax
import jax.numpy as jnp
from jax import lax
import numpy as np

D_MODEL = 2048
BATCH = 4
SEQ = 4096
DEPTH = 2

MEM_LEN = 256
MAX_POS_OFFSET = 4096

S5_WIDTH = D_MODEL // 4
S5_CH_PER_GROUP = 16
S5_GROUPS = S5_WIDTH // S5_CH_PER_GROUP
S5_STATE = 64
S5_LAMBDA_RE_MAX = -1e-4

MLA_HEADS = D_MODEL // 256
MLA_NOPE = 128
MLA_ROPE = 64
MLA_V = 128
MLA_Q_RANK = D_MODEL // 4
MLA_KV_RANK = D_MODEL // 4
MLA_WIDTH = MLA_HEADS * MLA_V
ROPE_THETA = 10000.0
Q_BLOCK = 128

GDN_HEADS = D_MODEL // 512
GDN_DK = 128
GDN_DV = 128
GDN_WIDTH = GDN_HEADS * GDN_DV
GDN_QKV = 2 * GDN_HEADS * GDN_DK + GDN_WIDTH
GDN_CONV = 4
GDN_CHUNK = 64

XA_HEADS = 4
XA_DH = 128
XA_WIDTH = XA_HEADS * XA_DH

MOE_GROUPS = 4
MOE_PER_GROUP = 8
MOE_EXPERTS = MOE_GROUPS * MOE_PER_GROUP
MOE_TOPK = 2
MOE_FF = D_MODEL // 4
MOE_ROW_BLOCK = 128

DN_ALPHA = (2 * DEPTH) ** 0.25
DN_BETA = (8 * DEPTH) ** -0.25

IN_SIZES = (S5_WIDTH, MLA_Q_RANK, MLA_KV_RANK, MLA_ROPE,
            GDN_HEADS * GDN_DK, GDN_HEADS * GDN_DK, GDN_WIDTH, GDN_WIDTH,
            GDN_HEADS, GDN_HEADS)
IN_COLS = sum(IN_SIZES)
MIX_WIDTH = S5_WIDTH + MLA_WIDTH + GDN_WIDTH

kernel_name = 'hybrid_s5_mla_gdn_hmoe_deepnorm'


def _rms(x, g, eps=1e-6):
    xf = x.astype(jnp.float32)
    y = xf * lax.rsqrt(jnp.mean(xf * xf, axis=-1, keepdims=True) + eps)
    return (y * g.astype(jnp.float32)).astype(x.dtype)


def _layernorm(x, g, b, eps=1e-5):
    xf = x.astype(jnp.float32)
    mu = jnp.mean(xf, axis=-1, keepdims=True)
    var = jnp.mean(jnp.square(xf - mu), axis=-1, keepdims=True)
    y = (xf - mu) * lax.rsqrt(var + eps)
    return (y * g.astype(jnp.float32) + b.astype(jnp.float32)).astype(x.dtype)


def _l2norm(x, eps=1e-6):
    xf = x.astype(jnp.float32)
    return xf * lax.rsqrt(jnp.sum(xf * xf, axis=-1, keepdims=True) + eps)


def _rope(x, positions):
    rdim = x.shape[-1]
    half = rdim // 2
    inv_freq = 1.0 / (ROPE_THETA ** (jnp.arange(half, dtype=jnp.float32) * (2.0 / rdim)))
    ang = positions.astype(jnp.float32)[..., None] * inv_freq
    ang = ang.reshape(ang.shape[:2] + (1,) * (x.ndim - 3) + (half,))
    cos, sin = jnp.cos(ang), jnp.sin(ang)
    xf = x.astype(jnp.float32)
    x1, x2 = xf[..., :half], xf[..., half:]
    return jnp.concatenate([x1 * cos - x2 * sin, x2 * cos + x1 * sin], axis=-1).astype(x.dtype)


def _split_in_proj(proj):
    idx = np.cumsum(np.array(IN_SIZES))[:-1].tolist()
    return jnp.split(proj, idx, axis=-1)


def _cmul(ar, ai, br, bi):
    return ar * br - ai * bi, ar * bi + ai * br


def _s5_scan(u, lam_re, lam_im, log_step, b_re, b_im, c_re, c_im, d_skip):
    f32 = jnp.float32
    lr = jnp.minimum(lam_re.astype(f32), S5_LAMBDA_RE_MAX)
    li = lam_im.astype(f32)
    dt = jnp.exp(log_step.astype(f32))[:, None]
    mag = jnp.exp(lr * dt)
    ab_re, ab_im = mag * jnp.cos(li * dt), mag * jnp.sin(li * dt)
    den = lr * lr + li * li
    nr, ni = ab_re - 1.0, ab_im
    fr = (nr * lr + ni * li) / den
    fi = (ni * lr - nr * li) / den
    br, bi = b_re.astype(f32), b_im.astype(f32)
    bb_re = fr[..., None] * br - fi[..., None] * bi
    bb_im = fr[..., None] * bi + fi[..., None] * br
    uf = u.astype(f32)
    bu_re = jnp.einsum('bsgh,gph->bsgp', uf, bb_re)
    bu_im = jnp.einsum('bsgh,gph->bsgp', uf, bb_im)
    a_re = jnp.broadcast_to(ab_re, bu_re.shape)
    a_im = jnp.broadcast_to(ab_im, bu_im.shape)

    def combine(e1, e2):
        a1r, a1i, b1r, b1i = e1
        a2r, a2i, b2r, b2i = e2
        ar, ai = _cmul(a2r, a2i, a1r, a1i)
        tr, ti = _cmul(a2r, a2i, b1r, b1i)
        return ar, ai, tr + b2r, ti + b2i

    _, _, xr, xi = lax.associative_scan(combine, (a_re, a_im, bu_re, bu_im), axis=1)
    y = (jnp.einsum('bsgp,ghp->bsgh', xr, c_re.astype(f32))
         - jnp.einsum('bsgp,ghp->bsgh', xi, c_im.astype(f32)))
    y = y + d_skip.astype(f32) * uf
    return y.astype(u.dtype)


def _s5_group(u, lam_re, lam_im, log_step, b_re, b_im, c_re, c_im, d_skip, w_glu, b_glu, g_out):
    bsz, seq, _ = u.shape
    y = _s5_scan(u.reshape(bsz, seq, S5_GROUPS, S5_CH_PER_GROUP),
                 lam_re, lam_im, log_step, b_re, b_im, c_re, c_im, d_skip)
    y = jax.nn.gelu(y.reshape(bsz, seq, S5_WIDTH))
    y = y * jax.nn.sigmoid(y @ w_glu + b_glu)
    return _rms(y, g_out)


def _causal_attention(q, k, v, scale):
    bsz, seq, heads, dk = q.shape
    nb = seq // Q_BLOCK
    qb = jnp.moveaxis(q.reshape(bsz, nb, Q_BLOCK, heads, dk), 1, 0)
    kpos = jnp.arange(seq)

    def one_block(args):
        qi, i = args
        s = jnp.einsum('bqhd,bkhd->bhqk', qi, k).astype(jnp.float32) * scale
        qpos = i * Q_BLOCK + jnp.arange(Q_BLOCK)
        s = jnp.where(kpos[None, :] <= qpos[:, None], s, -jnp.inf)
        p = jax.nn.softmax(s, axis=-1).astype(v.dtype)
        return jnp.einsum('bhqk,bkhd->bqhd', p, v)

    out = lax.map(one_block, (qb, jnp.arange(nb)))
    return jnp.moveaxis(out, 0, 1).reshape(bsz, seq, heads, v.shape[-1])


def _mla_group(cq, ckv, kr, positions, q_norm, w_uq, kv_norm, w_ukv, g_out):
    bsz, seq, _ = cq.shape
    q = (_rms(cq, q_norm) @ w_uq).reshape(bsz, seq, MLA_HEADS, MLA_NOPE + MLA_ROPE)
    q = jnp.concatenate([q[..., :MLA_NOPE], _rope(q[..., MLA_NOPE:], positions)], axis=-1)
    kv = (_rms(ckv, kv_norm) @ w_ukv).reshape(bsz, seq, MLA_HEADS, MLA_NOPE + MLA_V)
    k_pe = jnp.broadcast_to(_rope(kr, positions)[:, :, None, :], (bsz, seq, MLA_HEADS, MLA_ROPE))
    k = jnp.concatenate([kv[..., :MLA_NOPE], k_pe], axis=-1)
    v = kv[..., MLA_NOPE:]
    o = _causal_attention(q, k, v, (MLA_NOPE + MLA_ROPE) ** -0.5)
    return _rms(o.reshape(bsz, seq, MLA_WIDTH), g_out)


def _causal_dwconv(x, w):
    taps, ch = w.shape
    return lax.conv_general_dilated(x, w.astype(x.dtype)[:, None, :], window_strides=(1,),
                                    padding=((taps - 1, 0),),
                                    dimension_numbers=('NWC', 'WIO', 'NWC'),
                                    feature_group_count=ch)


def _gated_delta_chunked(q, k, v, g, beta):
    f32 = jnp.float32
    bsz, seq, heads, dk = q.shape
    dv = v.shape[-1]
    c = GDN_CHUNK
    n = seq // c

    def to_chunks(t):
        t = t.astype(f32).reshape((bsz, n, c, heads) + t.shape[3:])
        return jnp.moveaxis(t, 3, 1)

    q = to_chunks(q) * (dk ** -0.5)
    k, v, g, beta = to_chunks(k), to_chunks(v), to_chunks(g), to_chunks(beta)
    gc = jnp.cumsum(g, axis=-1)
    tri = jnp.tril(jnp.ones((c, c), dtype=bool))
    strict = jnp.tril(jnp.ones((c, c), dtype=bool), -1)
    decay = jnp.where(tri, jnp.exp(jnp.where(tri, gc[..., :, None] - gc[..., None, :], 0.0)), 0.0)
    kb = k * beta[..., None]
    vb = v * beta[..., None]
    lmat = jnp.where(strict, jnp.einsum('bhncd,bhnjd->bhncj', kb, k) * decay, 0.0)
    eye = jnp.eye(c, dtype=f32)
    tmat = lax.linalg.triangular_solve(eye + lmat, jnp.broadcast_to(eye, lmat.shape),
                                       left_side=True, lower=True, unit_diagonal=True)
    u = tmat @ vb
    w = tmat @ (kb * jnp.exp(gc)[..., None])
    a_intra = jnp.where(tri, jnp.einsum('bhncd,bhnjd->bhncj', q, k) * decay, 0.0)
    g_last = gc[..., -1]
    k_end = k * jnp.exp(g_last[..., None] - gc)[..., None]
    q_dec = q * jnp.exp(gc)[..., None]
    xs = tuple(jnp.moveaxis(t, 2, 0) for t in (q_dec, w, u, a_intra, k_end, g_last))

    def step(state, inp):
        qd, wi, ui, ai, ke, gl = inp
        v_new = ui - jnp.einsum('bhcd,bhde->bhce', wi, state)
        o = jnp.einsum('bhcd,bhde->bhce', qd, state) + jnp.einsum('bhcj,bhje->bhce', ai, v_new)
        state = state * jnp.exp(gl)[..., None, None] + jnp.einsum('bhcd,bhce->bhde', ke, v_new)
        return state, o

    s0 = jnp.zeros((bsz, heads, dk, dv), f32)
    _, o = lax.scan(step, s0, xs)
    o = jnp.moveaxis(o, 0, 2)
    return jnp.moveaxis(o, 1, 3).reshape(bsz, seq, heads, dv)


def _gdn_group(gq, gk, gv, gz, ga, gb, w_conv, a_log, dt_bias, g_out):
    f32 = jnp.float32
    bsz, seq, _ = gq.shape
    nqk = GDN_HEADS * GDN_DK
    qkv = jax.nn.silu(_causal_dwconv(jnp.concatenate([gq, gk, gv], axis=-1), w_conv))
    q = _l2norm(qkv[..., :nqk].reshape(bsz, seq, GDN_HEADS, GDN_DK))
    k = _l2norm(qkv[..., nqk:2 * nqk].reshape(bsz, seq, GDN_HEADS, GDN_DK))
    v = qkv[..., 2 * nqk:].reshape(bsz, seq, GDN_HEADS, GDN_DV)
    beta = jax.nn.sigmoid(gb.astype(f32))
    g = -jnp.exp(a_log.astype(f32)) * jax.nn.softplus(ga.astype(f32) + dt_bias.astype(f32))
    o = _gated_delta_chunked(q, k, v, g, beta)
    o = _rms(o, g_out) * jax.nn.silu(gz.astype(f32).reshape(bsz, seq, GDN_HEADS, GDN_DV))
    return o.reshape(bsz, seq, GDN_WIDTH).astype(gq.dtype)


def _memory_xattn(x, mem, w_q, w_k, w_v, w_o):
    bsz, seq, _ = x.shape
    mlen = mem.shape[1]
    q = (x @ w_q).reshape(bsz, seq, XA_HEADS, XA_DH)
    k = (mem @ w_k).reshape(bsz, mlen, XA_HEADS, XA_DH)
    v = (mem @ w_v).reshape(bsz, mlen, XA_HEADS, XA_DH)
    s = jnp.einsum('bshd,bmhd->bhsm', q, k).astype(jnp.float32) * (XA_DH ** -0.5)
    p = jax.nn.softmax(s, axis=-1).astype(v.dtype)
    o = jnp.einsum('bhsm,bmhd->bshd', p, v).reshape(bsz, seq, XA_WIDTH)
    return o @ w_o


def _hier_moe(x, w_group, b_group, w_expert, b_expert, w_gate_up, w_down):
    f32 = jnp.float32
    bsz, seq, dm = x.shape
    ntok = bsz * seq
    xt = x.reshape(ntok, dm)
    pg = jax.nn.softmax((xt @ w_group + b_group).astype(f32), axis=-1)
    pg_top, gsel = lax.top_k(pg, 1)
    le = (xt @ w_expert + b_expert).astype(f32).reshape(ntok, MOE_GROUPS, MOE_PER_GROUP)
    idx = jnp.broadcast_to(gsel[:, :, None], (ntok, 1, MOE_PER_GROUP))
    le = jnp.take_along_axis(le, idx, axis=1)[:, 0]
    pe_top, esel = lax.top_k(jax.nn.softmax(le, axis=-1), MOE_TOPK)
    gate = pg_top * pe_top / jnp.sum(pe_top, axis=-1, keepdims=True)
    eid = gsel * MOE_PER_GROUP + esel

    m = ntok * MOE_TOPK
    flat_e = eid.reshape(m)
    flat_tok = jnp.repeat(jnp.arange(ntok, dtype=jnp.int32), MOE_TOPK)
    flat_g = gate.reshape(m)
    order = jnp.argsort(flat_e)
    se = flat_e[order]
    counts = jnp.zeros((MOE_EXPERTS,), jnp.int32).at[flat_e].add(1)
    starts = jnp.cumsum(counts) - counts
    pcounts = (counts + MOE_ROW_BLOCK - 1) // MOE_ROW_BLOCK * MOE_ROW_BLOCK
    pends = jnp.cumsum(pcounts)
    pstarts = pends - pcounts
    dest = pstarts[se] + (jnp.arange(m, dtype=jnp.int32) - starts[se])
    nblk = (m + MOE_ROW_BLOCK - 1) // MOE_ROW_BLOCK + MOE_EXPERTS
    rows = nblk * MOE_ROW_BLOCK
    row_tok = jnp.full((rows,), ntok, jnp.int32).at[dest].set(flat_tok[order])
    row_gate = jnp.zeros((rows,), f32).at[dest].set(flat_g[order])
    blk_start = jnp.arange(nblk, dtype=jnp.int32) * MOE_ROW_BLOCK
    blk_exp = jnp.minimum(jnp.searchsorted(pends, blk_start, side='right'), MOE_EXPERTS - 1)
    xpad = jnp.concatenate([xt, jnp.zeros((1, dm), xt.dtype)], axis=0)
    xr = xpad[row_tok].reshape(nblk, MOE_ROW_BLOCK, dm)

    def expert_block(args):
        xb, e = args
        gu = xb @ w_gate_up[e]
        h = jax.nn.silu(gu[:, :MOE_FF]) * gu[:, MOE_FF:]
        return h @ w_down[e]

    yr = lax.map(expert_block, (xr, blk_exp)).reshape(rows, dm)
    yr = yr * row_gate[:, None].astype(yr.dtype)
    y = jax.ops.segment_sum(yr, row_tok, num_segments=ntok + 1)[:ntok]
    return y.reshape(bsz, seq, dm)


def setup_inputs(seed: int = 0) -> dict:
    key = jax.random.key(seed)
    keys = iter(jax.random.split(key, 64))
    f32 = jnp.float32
    L = DEPTH

    def normal(shape, scale):
        return scale * jax.random.normal(next(keys), shape, f32)

    def gain(shape):
        return 1.0 + normal(shape, 0.02)

    def log_uniform(shape, lo, hi):
        return jax.random.uniform(next(keys), shape, f32, math.log(lo), math.log(hi))

    x = normal((BATCH, SEQ, D_MODEL), 1.0)
    mem = normal((BATCH, MEM_LEN, D_MODEL), 1.0)
    positions = (jax.random.randint(next(keys), (BATCH, 1), 0, MAX_POS_OFFSET, dtype=jnp.int32)
                 + jnp.arange(SEQ, dtype=jnp.int32)[None, :])
    w_in = normal((L, D_MODEL, IN_COLS), D_MODEL ** -0.5)
    s5_lambda_re = -0.5 + normal((L, S5_GROUPS, S5_STATE), 0.01)
    s5_lambda_im = math.pi * jnp.arange(S5_STATE, dtype=f32) + normal((L, S5_GROUPS, S5_STATE), 0.01)
    s5_log_step = log_uniform((L, S5_GROUPS), 1e-3, 1e-1)
    b_scale = (2 * S5_CH_PER_GROUP) ** -0.5
    s5_b_re = normal((L, S5_GROUPS, S5_STATE, S5_CH_PER_GROUP), b_scale)
    s5_b_im = normal((L, S5_GROUPS, S5_STATE, S5_CH_PER_GROUP), b_scale)
    c_scale = S5_STATE ** -0.5
    s5_c_re = normal((L, S5_GROUPS, S5_CH_PER_GROUP, S5_STATE), c_scale)
    s5_c_im = normal((L, S5_GROUPS, S5_CH_PER_GROUP, S5_STATE), c_scale)
    s5_d = normal((L, S5_GROUPS, S5_CH_PER_GROUP), 1.0)
    s5_w_glu = normal((L, S5_WIDTH, S5_WIDTH), S5_WIDTH ** -0.5)
    s5_b_glu = normal((L, S5_WIDTH), 0.02)
    s5_out_norm = gain((L, S5_WIDTH))
    mla_q_norm = gain((L, MLA_Q_RANK))
    mla_w_uq = normal((L, MLA_Q_RANK, MLA_HEADS * (MLA_NOPE + MLA_ROPE)), MLA_Q_RANK ** -0.5)
    mla_kv_norm = gain((L, MLA_KV_RANK))
    mla_w_ukv = normal((L, MLA_KV_RANK, MLA_HEADS * (MLA_NOPE + MLA_V)), MLA_KV_RANK ** -0.5)
    mla_out_norm = gain((L, MLA_WIDTH))
    gdn_conv = normal((L, GDN_CONV, GDN_QKV), GDN_CONV ** -0.5)
    gdn_a_log = jnp.log(jax.random.uniform(next(keys), (L, GDN_HEADS), f32, 1.0, 16.0))
    dt = jnp.exp(log_uniform((L, GDN_HEADS), 1e-3, 1e-1))
    gdn_dt_bias = dt + jnp.log(-jnp.expm1(-dt))
    gdn_out_norm = gain((L, GDN_DV))
    w_out = normal((L, MIX_WIDTH, D_MODEL), DN_BETA * MIX_WIDTH ** -0.5)
    ln1_g = gain((L, D_MODEL))
    ln1_b = normal((L, D_MODEL), 0.02)
    xa_w_q = normal((L, D_MODEL, XA_WIDTH), D_MODEL ** -0.5)
    xa_w_k = normal((L, D_MODEL, XA_WIDTH), D_MODEL ** -0.5)
    xa_w_v = normal((L, D_MODEL, XA_WIDTH), D_MODEL ** -0.5)
    xa_w_o = normal((L, XA_WIDTH, D_MODEL), DN_BETA * XA_WIDTH ** -0.5)
    ln2_g = gain((L, D_MODEL))
    ln2_b = normal((L, D_MODEL), 0.02)
    moe_w_group = normal((L, D_MODEL, MOE_GROUPS), D_MODEL ** -0.5)
    moe_b_group = normal((L, MOE_GROUPS), 0.01)
    moe_w_expert = normal((L, D_MODEL, MOE_EXPERTS), D_MODEL ** -0.5)
    moe_b_expert = normal((L, MOE_EXPERTS), 0.01)
    moe_w_gate_up = normal((L, MOE_EXPERTS, D_MODEL, 2 * MOE_FF), D_MODEL ** -0.5)
    moe_w_down = normal((L, MOE_EXPERTS, MOE_FF, D_MODEL), DN_BETA * MOE_FF ** -0.5)
    ln3_g = gain((L, D_MODEL))
    ln3_b = normal((L, D_MODEL), 0.02)
    return {'x': x, 'mem': mem, 'positions': positions, 'w_in': w_in,
            's5_lambda_re': s5_lambda_re, 's5_lambda_im': s5_lambda_im, 's5_log_step': s5_log_step,
            's5_b_re': s5_b_re, 's5_b_im': s5_b_im, 's5_c_re': s5_c_re, 's5_c_im': s5_c_im,
            's5_d': s5_d, 's5_w_glu': s5_w_glu, 's5_b_glu': s5_b_glu, 's5_out_norm': s5_out_norm,
            'mla_q_norm': mla_q_norm, 'mla_w_uq': mla_w_uq, 'mla_kv_norm': mla_kv_norm,
            'mla_w_ukv': mla_w_ukv, 'mla_out_norm': mla_out_norm,
            'gdn_conv': gdn_conv, 'gdn_a_log': gdn_a_log, 'gdn_dt_bias': gdn_dt_bias,
            'gdn_out_norm': gdn_out_norm, 'w_out': w_out, 'ln1_g': ln1_g, 'ln1_b': ln1_b,
            'xa_w_q': xa_w_q, 'xa_w_k': xa_w_k, 'xa_w_v': xa_w_v, 'xa_w_o': xa_w_o,
            'ln2_g': ln2_g, 'ln2_b': ln2_b,
            'moe_w_group': moe_w_group, 'moe_b_group': moe_b_group,
            'moe_w_expert': moe_w_expert, 'moe_b_expert': moe_b_expert,
            'moe_w_gate_up': moe_w_gate_up, 'moe_w_down': moe_w_down,
            'ln3_g': ln3_g, 'ln3_b': ln3_b}


def reference(x, mem, positions, w_in, s5_lambda_re, s5_lambda_im, s5_log_step, s5_b_re, s5_b_im,
              s5_c_re, s5_c_im, s5_d, s5_w_glu, s5_b_glu, s5_out_norm, mla_q_norm, mla_w_uq,
              mla_kv_norm, mla_w_ukv, mla_out_norm, gdn_conv, gdn_a_log, gdn_dt_bias, gdn_out_norm,
              w_out, ln1_g, ln1_b, xa_w_q, xa_w_k, xa_w_v, xa_w_o, ln2_g, ln2_b,
              moe_w_group, moe_b_group, moe_w_expert, moe_b_expert, moe_w_gate_up, moe_w_down,
              ln3_g, ln3_b):
    for l in range(DEPTH):
        proj = x @ w_in[l]
        u, cq, ckv, kr, gq, gk, gv, gz, ga, gb = _split_in_proj(proj)
        y_s5 = _s5_group(u, s5_lambda_re[l], s5_lambda_im[l], s5_log_step[l], s5_b_re[l],
                         s5_b_im[l], s5_c_re[l], s5_c_im[l], s5_d[l], s5_w_glu[l], s5_b_glu[l],
                         s5_out_norm[l])
        y_mla = _mla_group(cq, ckv, kr, positions, mla_q_norm[l], mla_w_uq[l], mla_kv_norm[l],
                           mla_w_ukv[l], mla_out_norm[l])
        y_gdn = _gdn_group(gq, gk, gv, gz, ga, gb, gdn_conv[l], gdn_a_log[l], gdn_dt_bias[l],
                           gdn_out_norm[l])
        mixed = jnp.concatenate([y_s5, y_mla, y_gdn], axis=-1) @ w_out[l]
        x = _layernorm(DN_ALPHA * x + mixed, ln1_g[l], ln1_b[l])
        xa = _memory_xattn(x, mem, xa_w_q[l], xa_w_k[l], xa_w_v[l], xa_w_o[l])
        x = _layernorm(DN_ALPHA * x + xa, ln2_g[l], ln2_b[l])
        ff = _hier_moe(x, moe_w_group[l], moe_b_group[l], moe_w_expert[l], moe_b_expert[l],
                       moe_w_gate_up[l], moe_w_down[l])
        x = _layernorm(DN_ALPHA * x + ff, ln3_g[l], ln3_b[l])
    return x
```

```python
import functools
import math

import jax
import jax.numpy as jnp
from jax import lax
from jax.experimental import pallas as pl
from jax.experimental.pallas import tpu as pltpu

F32 = jnp.float32
BF16 = jnp.bfloat16
HIGHEST = lax.Precision.HIGHEST

S5_CH = 16
S5_STATE = 64
S5_LAMBDA_RE_MAX = -1e-4
S5_CHUNK = 32
MLA_NOPE = 128
MLA_ROPE = 64
MLA_V = 128
ROPE_THETA = 10000.0
GDN_DK = 128
GDN_DV = 128
GDN_CONV = 4
GDN_CHUNK = 64
XA_DH = 128
MOE_GROUPS = 4
MOE_PER_GROUP = 8
MOE_TOPK = 2
MOE_ROW_BLOCK = 256

LANES = 128
VMEM_LIMIT = 56 * 1024 * 1024


def _cparams(sem, vmem=None):
    return pltpu.CompilerParams(dimension_semantics=sem, vmem_limit_bytes=vmem)


def _const_spec(shape):
    nd = len(shape)
    return pl.BlockSpec(shape, lambda *_: (0,) * nd)


def _rms_rows(x, gain, eps=1e-6):
    return x * lax.rsqrt(jnp.mean(x * x, axis=-1, keepdims=True) + eps) * gain


def _layernorm_rows(x, g, b, eps=1e-5):
    mu = jnp.mean(x, axis=-1, keepdims=True)
    xc = x - mu
    var = jnp.mean(xc * xc, axis=-1, keepdims=True)
    return xc * lax.rsqrt(var + eps) * g + b


def _dot(a, b):
    return jnp.dot(a, b, preferred_element_type=F32)


def _in_proj_kernel(x_ref, w_ref, u_ref, cq_ref, ckv_ref, misc_ref, qkv_ref, gz_ref, *, splits):
    xb = x_ref[...].astype(BF16)
    outs = (u_ref, cq_ref, ckv_ref, misc_ref, qkv_ref, gz_ref)
    for o_ref, (lo, hi) in zip(outs, splits):
        o_ref[...] = _dot(xb, w_ref[:, lo:hi]).astype(o_ref.dtype)


def _in_proj(x2d, w_packed, widths, tm=256):
    t, d = x2d.shape
    splits, lo = [], 0
    for w in widths:
        splits.append((lo, lo + w))
        lo += w
    out_shape = tuple(jax.ShapeDtypeStruct((t, w), F32) for w in widths)
    out_specs = tuple(pl.BlockSpec((tm, w), lambda i: (i, 0)) for w in widths)
    return pl.pallas_call(
        functools.partial(_in_proj_kernel, splits=tuple(splits)),
        grid=(t // tm,),
        in_specs=[pl.BlockSpec((tm, d), lambda i: (i, 0)), _const_spec(w_packed.shape)],
        out_specs=out_specs,
        out_shape=out_shape,
        compiler_params=_cparams(("parallel",), VMEM_LIMIT),
        name="in_proj",
    )(x2d, w_packed)


def _s5_tables(lam_re, lam_im, log_step, b_re, b_im, c_re, c_im, d_skip, chunk, n_chunks):
    g, p = lam_re.shape
    h = b_re.shape[-1]
    lr = jnp.minimum(lam_re.astype(F32), S5_LAMBDA_RE_MAX)
    li = lam_im.astype(F32)
    dt = jnp.exp(log_step.astype(F32))[:, None]
    mag = jnp.exp(lr * dt)
    th = li * dt
    ab_re, ab_im = mag * jnp.cos(th), mag * jnp.sin(th)
    den = lr * lr + li * li
    nr, ni = ab_re - 1.0, ab_im
    fr = (nr * lr + ni * li) / den
    fi = (ni * lr - nr * li) / den
    br, bi = b_re.astype(F32), b_im.astype(F32)
    bb_re = fr[..., None] * br - fi[..., None] * bi
    bb_im = fr[..., None] * bi + fi[..., None] * br
    cr, ci = c_re.astype(F32), c_im.astype(F32)
    n = jnp.arange(chunk + 1, dtype=F32)[:, None, None]
    pmag = jnp.exp(n * (lr * dt)[None])
    pr, pi = pmag * jnp.cos(n * th[None]), pmag * jnp.sin(n * th[None])

    cb_re = cr.transpose(0, 2, 1)[:, :, :, None] * bb_re[:, :, None, :] - ci.transpose(0, 2, 1)[:, :, :, None] * bb_im[:, :, None, :]
    cb_im = cr.transpose(0, 2, 1)[:, :, :, None] * bb_im[:, :, None, :] + ci.transpose(0, 2, 1)[:, :, :, None] * bb_re[:, :, None, :]
    kk = (jnp.einsum('tgp,gphk->tghk', pr[:chunk], cb_re, precision=HIGHEST)
          - jnp.einsum('tgp,gphk->tghk', pi[:chunk], cb_im, precision=HIGHEST))
    kk = kk.at[0].add(jnp.eye(h, dtype=F32)[None] * d_skip.astype(F32)[:, :, None])
    jj = jnp.arange(chunk)
    lag = jj[None, :] - jj[:, None]
    toe = jnp.where((lag >= 0)[:, :, None, None, None], kk[jnp.clip(lag, 0)], 0.0)
    m_intra = toe.transpose(2, 0, 4, 1, 3).reshape(g, chunk * h, chunk * h)

    pr_rev, pi_rev = pr[chunk - 1::-1][:chunk], pi[chunk - 1::-1][:chunk]
    e_re = pr_rev[:, :, :, None] * bb_re[None] - pi_rev[:, :, :, None] * bb_im[None]
    e_im = pr_rev[:, :, :, None] * bb_im[None] + pi_rev[:, :, :, None] * bb_re[None]
    e_mat = jnp.concatenate([e_re, e_im], axis=2).transpose(1, 0, 3, 2).reshape(g, chunk * h, 2 * p)

    pr1, pi1 = pr[1:], pi[1:]
    f_re = cr[None] * pr1[:, :, None, :] - ci[None] * pi1[:, :, None, :]
    f_im = cr[None] * pi1[:, :, None, :] + ci[None] * pr1[:, :, None, :]
    f_mat = jnp.concatenate([f_re, -f_im], axis=3).transpose(1, 3, 0, 2).reshape(g, 2 * p, chunk * h)

    steps = max(1, int(math.ceil(math.log2(n_chunks))))
    ar, ai = pr[chunk], pi[chunk]
    a1, a2 = [], []
    for _ in range(steps):
        a1.append(jnp.concatenate([ar, ar], axis=-1))
        a2.append(jnp.concatenate([-ai, ai], axis=-1))
        ar, ai = ar * ar - ai * ai, 2.0 * ar * ai
    pad = (-steps) % 8
    a1 = jnp.pad(jnp.stack(a1, axis=1), ((0, 0), (0, pad), (0, 0)))
    a2 = jnp.pad(jnp.stack(a2, axis=1), ((0, 0), (0, pad), (0, 0)))
    return m_intra.astype(BF16), e_mat.astype(BF16), f_mat.astype(BF16), a1, a2


def _s5_kernel(u_ref, m_ref, e_ref, f_ref, a1_ref, a2_ref, y_ref, *, n_chunks, steps):
    u = u_ref[0]
    y = _dot(u, m_ref[0])
    s = _dot(u, e_ref[0])
    rows, width = s.shape
    half = width // 2
    c_idx = lax.broadcasted_iota(jnp.int32, (rows, width), 0) % n_chunks
    a1 = a1_ref[0]
    a2 = a2_ref[0]
    for k in range(steps):
        sh = 1 << k
        prev = jnp.where(c_idx >= sh, pltpu.roll(s, sh, axis=0), 0.0)
        s = s + a1[k:k + 1, :] * prev + a2[k:k + 1, :] * pltpu.roll(prev, half, axis=1)
    s_in = jnp.where(c_idx >= 1, pltpu.roll(s, 1, axis=0), 0.0)
    y_ref[0] = y + _dot(s_in.astype(BF16), f_ref[0])


def _s5_scan(u, tables, batch, seq, chunk):
    m_intra, e_mat, f_mat, a1, a2 = tables
    g = m_intra.shape[0]
    h = S5_CH
    n_chunks = seq // chunk
    rows = batch * n_chunks
    steps = max(1, int(math.ceil(math.log2(n_chunks))))
    ug = u.reshape(batch, n_chunks, chunk, g, h).transpose(3, 0, 1, 2, 4).reshape(g, rows, chunk * h).astype(BF16)
    y = pl.pallas_call(
        functools.partial(_s5_kernel, n_chunks=n_chunks, steps=steps),
        grid=(g,),
        in_specs=[pl.BlockSpec((1, rows, chunk * h), lambda i: (i, 0, 0)),
                  pl.BlockSpec((1,) + m_intra.shape[1:], lambda i: (i, 0, 0)),
                  pl.BlockSpec((1,) + e_mat.shape[1:], lambda i: (i, 0, 0)),
                  pl.BlockSpec((1,) + f_mat.shape[1:], lambda i: (i, 0, 0)),
                  pl.BlockSpec((1,) + a1.shape[1:], lambda i: (i, 0, 0)),
                  pl.BlockSpec((1,) + a2.shape[1:], lambda i: (i, 0, 0))],
        out_specs=pl.BlockSpec((1, rows, chunk * h), lambda i: (i, 0, 0)),
        out_shape=jax.ShapeDtypeStruct((g, rows, chunk * h), F32),
        compiler_params=_cparams(("parallel",), VMEM_LIMIT),
        name="s5_scan",
    )(ug, m_intra, e_mat, f_mat, a1, a2)
    return y.reshape(g, batch, n_chunks, chunk, h).transpose(1, 2, 3, 0, 4).reshape(batch * seq, g * h)


def _s5_glu_kernel(y_ref, w_ref, b_ref, g_ref, o_ref):
    y = jax.nn.gelu(y_ref[...])
    z = _dot(y.astype(BF16), w_ref[...]) + b_ref[...]
    y = y * jax.nn.sigmoid(z)
    o_ref[...] = _rms_rows(y, g_ref[...]).astype(o_ref.dtype)


def _s5_glu(y, w_glu, b_glu, g_out, tm=1024):
    t, w = y.shape
    return pl.pallas_call(
        _s5_glu_kernel,
        grid=(t // tm,),
        in_specs=[pl.BlockSpec((tm, w), lambda i: (i, 0)), _const_spec((w, w)), _const_spec((1, w)), _const_spec((1, w))],
        out_specs=pl.BlockSpec((tm, w), lambda i: (i, 0)),
        out_shape=jax.ShapeDtypeStruct((t, w), BF16),
        compiler_params=_cparams(("parallel",)),
        name="s5_glu",
    )(y, w_glu.astype(BF16), b_glu.reshape(1, w).astype(F32), g_out.reshape(1, w).astype(F32))


def _mla_proj_kernel(cq_ref, ckv_ref, misc_ref, cos_ref, sin_ref, qn_ref, kvn_ref, wuq_ref, wukv_ref,
                     qt_ref, k_ref, vt_ref, *, heads, scale):
    cq = _rms_rows(cq_ref[...], qn_ref[...]).astype(BF16)
    q = _dot(cq, wuq_ref[...]) * scale
    ckv = _rms_rows(ckv_ref[...], kvn_ref[...]).astype(BF16)
    kv = _dot(ckv, wukv_ref[...])
    cos = cos_ref[...]
    sin = sin_ref[...]
    lane = lax.broadcasted_iota(jnp.int32, cos.shape, 1)
    first_half = (lane % MLA_ROPE) < (MLA_ROPE // 2)

    def rope(x):
        partner = jnp.where(first_half, pltpu.roll(x, LANES - MLA_ROPE // 2, axis=1), pltpu.roll(x, MLA_ROPE // 2, axis=1))
        return x * cos + partner * sin

    kpe = rope(misc_ref[...])
    kpe_lo = jnp.where(lane < MLA_ROPE, kpe, 0.0)
    kpe_hi = pltpu.roll(kpe_lo, MLA_ROPE, axis=1)
    nope_w = heads * MLA_NOPE
    for pair in range(heads // 2):
        q_pe = rope(q[:, nope_w + LANES * pair:nope_w + LANES * (pair + 1)])
        for h in (2 * pair, 2 * pair + 1):
            qh = jnp.concatenate([q[:, MLA_NOPE * h:MLA_NOPE * (h + 1)], q_pe], axis=1)
            qt_ref[0, h] = qh.T.astype(BF16)
            kvw = MLA_NOPE + MLA_V
            kh = jnp.concatenate([kv[:, kvw * h:kvw * h + MLA_NOPE], kpe_lo if h % 2 == 0 else kpe_hi], axis=1)
            k_ref[0, h] = kh.astype(BF16)
            vt_ref[0, h] = kv[:, kvw * h + MLA_NOPE:kvw * (h + 1)].T.astype(BF16)


def _mla_proj(cq, ckv, misc, cos_t, sin_t, q_norm, kv_norm, w_uq, w_ukv, batch, seq, ts=256):
    t, rank = cq.shape
    heads = w_ukv.shape[1] // (MLA_NOPE + MLA_V)
    dq = MLA_NOPE + MLA_ROPE
    w3 = w_uq.reshape(rank, heads, dq)
    w_uq_p = jnp.concatenate([w3[:, :, :MLA_NOPE].reshape(rank, -1), w3[:, :, MLA_NOPE:].reshape(rank, -1)], axis=1).astype(BF16)
    per = seq // ts
    dk = MLA_NOPE + LANES
    return pl.pallas_call(
        functools.partial(_mla_proj_kernel, heads=heads, scale=dq ** -0.5),
        grid=(t // ts,),
        in_specs=[pl.BlockSpec((ts, rank), lambda i: (i, 0)), pl.BlockSpec((ts, rank), lambda i: (i, 0)),
                  pl.BlockSpec((ts, LANES), lambda i: (i, 0)), pl.BlockSpec((ts, LANES), lambda i: (i, 0)),
                  pl.BlockSpec((ts, LANES), lambda i: (i, 0)),
                  _const_spec((1, rank)), _const_spec((1, rank)), _const_spec(w_uq_p.shape), _const_spec(w_ukv.shape)],
        out_specs=(pl.BlockSpec((1, heads, dk, ts), lambda i: (i // per, 0, 0, i % per)),
                   pl.BlockSpec((1, heads, ts, dk), lambda i: (i // per, 0, i % per, 0)),
                   pl.BlockSpec((1, heads, MLA_V, ts), lambda i: (i // per, 0, 0, i % per))),
        out_shape=(jax.ShapeDtypeStruct((batch, heads, dk, seq), BF16),
                   jax.ShapeDtypeStruct((batch, heads, seq, dk), BF16),
                   jax.ShapeDtypeStruct((batch, heads, MLA_V, seq), BF16)),
        compiler_params=_cparams(("parallel",), VMEM_LIMIT),
        name="mla_proj",
    )(cq, ckv, misc, cos_t, sin_t, q_norm.reshape(1, rank).astype(F32), kv_norm.reshape(1, rank).astype(F32),
      w_uq_p, w_ukv.astype(BF16))


def _flash_kernel(qt_ref, k_ref, vt_ref, o_ref, m_sc, l_sc, acc_sc, *, tq, tk):
    qi = pl.program_id(2)
    ki = pl.program_id(3)
    last = (qi * tq + tq - 1) // tk

    @pl.when(ki == 0)
    def _():
        m_sc[...] = jnp.full(m_sc.shape, -1e30, F32)
        l_sc[...] = jnp.zeros(l_sc.shape, F32)
        acc_sc[...] = jnp.zeros(acc_sc.shape, F32)

    def step(masked):
        s = _dot(k_ref[0, 0], qt_ref[0, 0])
        if masked:
            kpos = ki * tk + lax.broadcasted_iota(jnp.int32, s.shape, 0)
            qpos = qi * tq + lax.broadcasted_iota(jnp.int32, s.shape, 1)
            s = jnp.where(kpos <= qpos, s, -1e30)
        m_prev = m_sc[...]
        m_new = jnp.maximum(m_prev, jnp.max(s, axis=0, keepdims=True))
        alpha = jnp.exp(m_prev - m_new)
        p = jnp.exp(s - m_new)
        l_sc[...] = alpha * l_sc[...] + jnp.sum(p, axis=0, keepdims=True)
        acc_sc[...] = alpha * acc_sc[...] + _dot(vt_ref[0, 0], p.astype(BF16))
        m_sc[...] = m_new

    visible = (ki + 1) * tk - 1 <= qi * tq

    @pl.when(visible)
    def _():
        step(False)

    @pl.when(jnp.logical_and(ki <= last, jnp.logical_not(visible)))
    def _():
        step(True)

    @pl.when(ki == last)
    def _():
        o_ref[0] = (acc_sc[...] / l_sc[...]).T.astype(o_ref.dtype)


def _mla_attention(qt, k, vt, tq=512, tk=512):
    batch, heads, dk, seq = qt.shape
    dv = vt.shape[2]

    def kv_idx(qi, ki):
        return jnp.minimum(ki, (qi * tq + tq - 1) // tk)

    return pl.pallas_call(
        functools.partial(_flash_kernel, tq=tq, tk=tk),
        grid=(batch, heads, seq // tq, seq // tk),
        in_specs=[pl.BlockSpec((1, 1, dk, tq), lambda b, h, qi, ki: (b, h, 0, qi)),
                  pl.BlockSpec((1, 1, tk, dk), lambda b, h, qi, ki: (b, h, kv_idx(qi, ki), 0)),
                  pl.BlockSpec((1, 1, dv, tk), lambda b, h, qi, ki: (b, h, 0, kv_idx(qi, ki)))],
        out_specs=pl.BlockSpec((1, tq, dv), lambda b, h, qi, ki: (b, qi, h)),
        out_shape=jax.ShapeDtypeStruct((batch, seq, heads * dv), F32),
        scratch_shapes=[pltpu.VMEM((1, tq), F32), pltpu.VMEM((1, tq), F32), pltpu.VMEM((dv, tq), F32)],
        compiler_params=_cparams(("parallel", "parallel", "parallel", "arbitrary"), VMEM_LIMIT),
        name="mla_flash",
    )(qt, k, vt)


def _rope_tables(positions):
    half = MLA_ROPE // 2
    inv_freq = 1.0 / (ROPE_THETA ** (jnp.arange(half, dtype=F32) * (2.0 / MLA_ROPE)))
    ang = positions.astype(F32).reshape(-1)[:, None] * inv_freq
    cos, sin = jnp.cos(ang), jnp.sin(ang)
    reps = LANES // MLA_ROPE
    return jnp.tile(jnp.concatenate([cos, cos], axis=1), (1, reps)), jnp.tile(jnp.concatenate([-sin, sin], axis=1), (1, reps))


G_LANE = 64
B_LANE = 68


def _gdn_pre_kernel(x_ref, prev_ref, misc_ref, cw_ref, alog_ref, dtb_ref,
                    q_ref, k_ref, v_ref, kt_ref, gb_ref, gbt_ref, *, heads, chunk):
    x = x_ref[0]
    ts = x.shape[0]
    prev = jnp.where(pl.program_id(1) > 0, prev_ref[0], 0.0)
    cw = cw_ref[...]
    row8 = lax.broadcasted_iota(jnp.int32, prev.shape, 0)
    acc = x * cw[GDN_CONV - 1:GDN_CONV, :]
    for d in range(1, GDN_CONV):
        xr = pltpu.roll(x, d, axis=0)
        head = jnp.where(row8 < d, pltpu.roll(prev, d, axis=0), xr[0:8])
        xs = jnp.concatenate([head, xr[8:]], axis=0)
        acc = acc + xs * cw[GDN_CONV - 1 - d:GDN_CONV - d, :]
    y = acc * jax.nn.sigmoid(acc)
    nqk = heads * GDN_DK

    def l2n(z):
        return z * lax.rsqrt(jnp.sum(z * z, axis=-1, keepdims=True) + 1e-6)

    for h in range(heads):
        q_ref[0, :, GDN_DK * h:GDN_DK * (h + 1)] = l2n(y[:, GDN_DK * h:GDN_DK * (h + 1)])
    kn = jnp.concatenate([l2n(y[:, nqk + GDN_DK * h:nqk + GDN_DK * (h + 1)]) for h in range(heads)], axis=1)
    k_ref[0] = kn
    v_ref[0] = y[:, 2 * nqk:]
    knt = kn.T
    for n in range(ts // chunk):
        kt_ref[0, n] = knt[:, chunk * n:chunk * (n + 1)]
    m = misc_ref[0]
    lane = lax.broadcasted_iota(jnp.int32, m.shape, 1)
    g = -jnp.exp(alog_ref[...]) * jax.nn.softplus(m + dtb_ref[...])
    beta = jax.nn.sigmoid(m)
    gb = jnp.where((lane >= G_LANE) & (lane < G_LANE + heads), g,
                   jnp.where((lane >= B_LANE) & (lane < B_LANE + heads), beta, 0.0))
    gb_ref[0] = gb
    gbt = gb.T[G_LANE:G_LANE + 8, :]
    for n in range(ts // chunk):
        gbt_ref[0, n] = gbt[:, chunk * n:chunk * (n + 1)]


def _gdn_pre(qkv, misc, w_conv, a_log, dt_bias, batch, seq, ts=256):
    width = qkv.shape[-1]
    heads = a_log.shape[0]
    chunk = GDN_CHUNK
    hd = heads * GDN_DK
    x3 = qkv.reshape(batch, seq, width)
    m3 = misc.reshape(batch, seq, LANES)
    alog_row = jnp.zeros((1, LANES), F32).at[0, G_LANE:G_LANE + heads].set(a_log.astype(F32))
    dtb_row = jnp.zeros((1, LANES), F32).at[0, G_LANE:G_LANE + heads].set(dt_bias.astype(F32))
    nck = ts // chunk
    tok = lambda b, s: (b, s, 0)
    return pl.pallas_call(
        functools.partial(_gdn_pre_kernel, heads=heads, chunk=chunk),
        grid=(batch, seq // ts),
        in_specs=[pl.BlockSpec((1, ts, width), tok),
                  pl.BlockSpec((1, 8, width), lambda b, s: (b, jnp.maximum(s * (ts // 8) - 1, 0), 0)),
                  pl.BlockSpec((1, ts, LANES), tok),
                  _const_spec(w_conv.shape), _const_spec((1, LANES)), _const_spec((1, LANES))],
        out_specs=(pl.BlockSpec((1, ts, hd), tok), pl.BlockSpec((1, ts, hd), tok), pl.BlockSpec((1, ts, width - 2 * hd), tok),
                   pl.BlockSpec((1, nck, hd, chunk), lambda b, s: (b, s, 0, 0)),
                   pl.BlockSpec((1, ts, LANES), tok),
                   pl.BlockSpec((1, nck, 8, chunk), lambda b, s: (b, s, 0, 0))),
        out_shape=(jax.ShapeDtypeStruct((batch, seq, hd), F32), jax.ShapeDtypeStruct((batch, seq, hd), F32),
                   jax.ShapeDtypeStruct((batch, seq, width - 2 * hd), F32),
                   jax.ShapeDtypeStruct((batch, seq // chunk, hd, chunk), F32),
                   jax.ShapeDtypeStruct((batch, seq, LANES), F32),
                   jax.ShapeDtypeStruct((batch, seq // chunk, 8, chunk), F32)),
        compiler_params=_cparams(("parallel", "parallel"), VMEM_LIMIT),
        name="gdn_pre",
    )(x3, x3, m3, w_conv.astype(F32), alog_row, dtb_row)


def _hdot(a, b):
    return jnp.dot(a, b, preferred_element_type=F32, precision=HIGHEST)


def _unit_lower_inverse(l):
    n = l.shape[0]
    eye = (lax.broadcasted_iota(jnp.int32, l.shape, 0) == lax.broadcasted_iota(jnp.int32, l.shape, 1)).astype(F32)
    p = eye - l
    sq = l
    k = 2
    while k < n:
        sq = _hdot(sq, sq)
        p = p + _hdot(p, sq)
        k *= 2
    return p


def _gdn_local_kernel(q_ref, k_ref, v_ref, kt_ref, gb_ref, gbt_ref,
                      u_ref, w_ref, qd_ref, a_ref, kend_ref, egl_ref, *, heads, chunk, n_chunks):
    c = chunk
    ri = lax.broadcasted_iota(jnp.int32, (c, c), 0)
    ci = lax.broadcasted_iota(jnp.int32, (c, c), 1)
    tri = ri >= ci
    strict = ri > ci
    tri_f = tri.astype(F32)
    tri_t = (ri <= ci).astype(F32)
    for n in range(n_chunks):
        r0 = n * c
        gbc = gb_ref[0, r0:r0 + c, :]
        gcc = _hdot(tri_f, gbc)
        gcr = _hdot(gbt_ref[0, n], tri_t)
        egl_rows = []
        for h in range(heads):
            cols = slice(GDN_DK * h, GDN_DK * (h + 1))
            gc_c = gcc[:, G_LANE + h:G_LANE + h + 1]
            gc_r = gcr[h:h + 1, :]
            beta_c = gbc[:, B_LANE + h:B_LANE + h + 1]
            g_last = gcc[c - 1:c, G_LANE + h:G_LANE + h + 1]
            decay = jnp.where(tri, jnp.exp(jnp.where(tri, gc_c - gc_r, 0.0)), 0.0)
            qh = q_ref[0, r0:r0 + c, cols] * (GDN_DK ** -0.5)
            kh = k_ref[0, r0:r0 + c, cols]
            vh = v_ref[0, r0:r0 + c, cols]
            kth = kt_ref[0, n, cols, :]
            kb = kh * beta_c
            vb = vh * beta_c
            kth_b = kth.astype(BF16)
            lmat = jnp.where(strict, _dot(kb.astype(BF16), kth_b) * decay, 0.0)
            tmat = _unit_lower_inverse(lmat).astype(BF16)
            eg = jnp.exp(gc_c)
            u_ref[0, r0:r0 + c, cols] = _dot(tmat, vb.astype(BF16))
            w_ref[0, r0:r0 + c, cols] = _dot(tmat, (kb * eg).astype(BF16)).astype(w_ref.dtype)
            qd_ref[0, r0:r0 + c, cols] = (qh * eg).astype(qd_ref.dtype)
            a_ref[0, r0:r0 + c, c * h:c * (h + 1)] = jnp.where(tri, _dot(qh.astype(BF16), kth_b) * decay, 0.0).astype(a_ref.dtype)
            kend_ref[0, n, cols, :] = (kth * jnp.exp(g_last - gc_r)).astype(kend_ref.dtype)
            egl_rows.append(jnp.broadcast_to(jnp.exp(g_last), (1, LANES)))
        egl_rows.append(jnp.zeros((8 - heads, LANES), F32))
        egl_ref[0, n] = jnp.concatenate(egl_rows, axis=0)


def _gdn_local(q, k, v, kt, gb, gbt, cb=4):
    batch, seq, hd = q.shape
    heads = hd // GDN_DK
    chunk = GDN_CHUNK
    n_all = seq // chunk
    ts = cb * chunk
    tok = lambda b, s: (b, s, 0)
    ck = lambda b, s: (b, s, 0, 0)
    return pl.pallas_call(
        functools.partial(_gdn_local_kernel, heads=heads, chunk=chunk, n_chunks=cb),
        grid=(batch, n_all // cb),
        in_specs=[pl.BlockSpec((1, ts, hd), tok), pl.BlockSpec((1, ts, hd), tok), pl.BlockSpec((1, ts, hd), tok),
                  pl.BlockSpec((1, cb, hd, chunk), ck), pl.BlockSpec((1, ts, LANES), tok), pl.BlockSpec((1, cb, 8, chunk), ck)],
        out_specs=(pl.BlockSpec((1, ts, hd), tok), pl.BlockSpec((1, ts, hd), tok), pl.BlockSpec((1, ts, hd), tok),
                   pl.BlockSpec((1, ts, heads * chunk), tok), pl.BlockSpec((1, cb, hd, chunk), ck),
                   pl.BlockSpec((1, cb, 8, LANES), ck)),
        out_shape=(jax.ShapeDtypeStruct((batch, seq, hd), F32), jax.ShapeDtypeStruct((batch, seq, hd), BF16),
                   jax.ShapeDtypeStruct((batch, seq, hd), BF16), jax.ShapeDtypeStruct((batch, seq, heads * chunk), BF16),
                   jax.ShapeDtypeStruct((batch, n_all, hd, chunk), BF16), jax.ShapeDtypeStruct((batch, n_all, 8, LANES), F32)),
        compiler_params=_cparams(("parallel", "parallel"), VMEM_LIMIT),
        name="gdn_local",
    )(q, k, v, kt, gb, gbt)


def _gdn_scan_kernel(u_ref, w_ref, qd_ref, a_ref, kend_ref, egl_ref, gz_ref, gn_ref, o_ref, st_ref, *, heads, chunk, n_chunks):
    c = chunk

    @pl.when(pl.program_id(1) == 0)
    def _():
        st_ref[...] = jnp.zeros(st_ref.shape, F32)

    gn = gn_ref[...]
    for n in range(n_chunks):
        r0 = n * c
        egl = egl_ref[0, n]
        for h in range(heads):
            cols = slice(GDN_DV * h, GDN_DV * (h + 1))
            st = st_ref[h]
            sb = st.astype(BF16)
            v_new = u_ref[0, r0:r0 + c, cols] - _dot(w_ref[0, r0:r0 + c, cols], sb)
            vb = v_new.astype(BF16)
            o = _dot(qd_ref[0, r0:r0 + c, cols], sb) + _dot(a_ref[0, r0:r0 + c, c * h:c * (h + 1)], vb)
            st_ref[h] = st * egl[h:h + 1, :] + _dot(kend_ref[0, n, cols, :], vb)
            z = gz_ref[0, r0:r0 + c, cols]
            o_ref[0, r0:r0 + c, cols] = (_rms_rows(o, gn) * (z * jax.nn.sigmoid(z))).astype(o_ref.dtype)


def _gdn_scan(u, w, qd, a, kend, egl, gz, g_out, cs=4):
    batch, seq, hd = u.shape
    heads = hd // GDN_DV
    chunk = GDN_CHUNK
    n_all = seq // chunk
    ts = cs * chunk
    tok = lambda b, s: (b, s, 0)
    ck = lambda b, s: (b, s, 0, 0)
    return pl.pallas_call(
        functools.partial(_gdn_scan_kernel, heads=heads, chunk=chunk, n_chunks=cs),
        grid=(batch, n_all // cs),
        in_specs=[pl.BlockSpec((1, ts, hd), tok), pl.BlockSpec((1, ts, hd), tok), pl.BlockSpec((1, ts, hd), tok),
                  pl.BlockSpec((1, ts, heads * chunk), tok), pl.BlockSpec((1, cs, hd, chunk), ck),
                  pl.BlockSpec((1, cs, 8, LANES), ck), pl.BlockSpec((1, ts, hd), tok), _const_spec((1, GDN_DV))],
        out_specs=pl.BlockSpec((1, ts, hd), tok),
        out_shape=jax.ShapeDtypeStruct((batch, seq, hd), BF16),
        scratch_shapes=[pltpu.VMEM((heads, GDN_DK, GDN_DV), F32)],
        compiler_params=_cparams(("parallel", "arbitrary"), VMEM_LIMIT),
        name="gdn_scan",
    )(u, w, qd, a, kend, egl, gz.reshape(batch, seq, hd), g_out.reshape(1, GDN_DV).astype(F32))


def _mix_out_kernel(s5_ref, mla_ref, gdn_ref, x_ref, w_ref, mg_ref, g_ref, b_ref, o_ref, *, alpha):
    w5 = s5_ref.shape[1]
    wm = mla_ref.shape[1]
    mla = _rms_rows(mla_ref[...], mg_ref[...]).astype(BF16)
    acc = _dot(s5_ref[...], w_ref[0:w5, :]) + _dot(mla, w_ref[w5:w5 + wm, :]) + _dot(gdn_ref[...], w_ref[w5 + wm:, :])
    o_ref[...] = _layernorm_rows(alpha * x_ref[...] + acc, g_ref[...], b_ref[...])


def _mix_out(y_s5, o_mla, y_gdn, x2d, w_out, mla_gain, ln_g, ln_b, alpha, tm=256):
    t, d = x2d.shape
    row = lambda i: (i, 0)
    return pl.pallas_call(
        functools.partial(_mix_out_kernel, alpha=alpha),
        grid=(t // tm,),
        in_specs=[pl.BlockSpec((tm, y_s5.shape[1]), row), pl.BlockSpec((tm, o_mla.shape[1]), row),
                  pl.BlockSpec((tm, y_gdn.shape[1]), row), pl.BlockSpec((tm, d), row),
                  _const_spec(w_out.shape), _const_spec((1, o_mla.shape[1])), _const_spec((1, d)), _const_spec((1, d))],
        out_specs=pl.BlockSpec((tm, d), row),
        out_shape=jax.ShapeDtypeStruct((t, d), F32),
        compiler_params=_cparams(("parallel",), VMEM_LIMIT),
        name="mix_out",
    )(y_s5, o_mla, y_gdn, x2d, w_out.astype(BF16), mla_gain.reshape(1, -1).astype(F32),
      ln_g.reshape(1, d).astype(F32), ln_b.reshape(1, d).astype(F32))


def _matmul_kernel(x_ref, w_ref, o_ref):
    o_ref[...] = _dot(x_ref[...].astype(BF16), w_ref[...]).astype(o_ref.dtype)


def _matmul(x, w, tm=256, tn=512):
    m, k = x.shape
    n = w.shape[1]
    return pl.pallas_call(
        _matmul_kernel,
        grid=(m // tm, n // tn),
        in_specs=[pl.BlockSpec((tm, k), lambda i, j: (i, 0)), pl.BlockSpec((k, tn), lambda i, j: (0, j))],
        out_specs=pl.BlockSpec((tm, tn), lambda i, j: (i, j)),
        out_shape=jax.ShapeDtypeStruct((m, n), F32),
        compiler_params=_cparams(("parallel", "parallel")),
        name="mem_kv_proj",
    )(x, w.astype(BF16))


def _xattn_kernel(x_ref, wq_ref, kt_ref, v_ref, wo_ref, g_ref, b_ref, wr_ref, o_ref, lg_ref, *, heads, alpha):
    x = x_ref[...]
    q = (_dot(x.astype(BF16), wq_ref[...]) * (XA_DH ** -0.5)).astype(BF16)
    outs = []
    for h in range(heads):
        cols = slice(XA_DH * h, XA_DH * (h + 1))
        s = _dot(q[:, cols], kt_ref[0, cols, :])
        s = s - jnp.max(s, axis=-1, keepdims=True)
        p = jnp.exp(s)
        p = p / jnp.sum(p, axis=-1, keepdims=True)
        outs.append(_dot(p.astype(BF16), v_ref[0, :, cols]))
    o = jnp.concatenate(outs, axis=1).astype(BF16)
    y = _layernorm_rows(alpha * x + _dot(o, wo_ref[...]), g_ref[...], b_ref[...])
    o_ref[...] = y
    lg_ref[...] = lax.dot_general(wr_ref[...], y, (((1,), (1,)), ((), ())), precision=HIGHEST, preferred_element_type=F32)


def _xattn(x2d, kt, v, w_q, w_o, ln_g, ln_b, w_router_t, seq, alpha, tm=256):
    t, d = x2d.shape
    width = w_q.shape[1]
    heads = width // XA_DH
    mlen = v.shape[1]
    per = seq // tm
    nr = w_router_t.shape[0]
    row = lambda i: (i, 0)
    return pl.pallas_call(
        functools.partial(_xattn_kernel, heads=heads, alpha=alpha),
        grid=(t // tm,),
        in_specs=[pl.BlockSpec((tm, d), row), _const_spec((d, width)),
                  pl.BlockSpec((1, width, mlen), lambda i: (i // per, 0, 0)),
                  pl.BlockSpec((1, mlen, width), lambda i: (i // per, 0, 0)),
                  _const_spec((width, d)), _const_spec((1, d)), _const_spec((1, d)), _const_spec((nr, d))],
        out_specs=(pl.BlockSpec((tm, d), row), pl.BlockSpec((nr, tm), lambda i: (0, i))),
        out_shape=(jax.ShapeDtypeStruct((t, d), F32), jax.ShapeDtypeStruct((nr, t), F32)),
        compiler_params=_cparams(("parallel",), VMEM_LIMIT),
        name="xattn",
    )(x2d, w_q.astype(BF16), kt, v, w_o.astype(BF16), ln_g.reshape(1, d).astype(F32), ln_b.reshape(1, d).astype(F32), w_router_t)


def _router_kernel(lg_ref, bias_ref, eid_ref, gate_ref):
    lg = lg_ref[...] + bias_ref[...]
    ng, ne = MOE_GROUPS, MOE_PER_GROUP
    grp = [lg[g:g + 1, :] for g in range(ng)]
    gmax = functools.reduce(jnp.maximum, grp)
    gexp = [jnp.exp(r - gmax) for r in grp]
    gsum = functools.reduce(lambda a, b: a + b, gexp)
    pg = [e / gsum for e in gexp]
    best, gsel = pg[0], jnp.zeros(pg[0].shape, jnp.int32)
    for g in range(1, ng):
        better = pg[g] > best
        gsel = jnp.where(better, g, gsel)
        best = jnp.where(better, pg[g], best)
    le = []
    for e in range(ne):
        r = lg[ng + e:ng + e + 1, :]
        for g in range(1, ng):
            r = jnp.where(gsel == g, lg[ng + g * ne + e:ng + g * ne + e + 1, :], r)
        le.append(r)
    emax = functools.reduce(jnp.maximum, le)
    eexp = [jnp.exp(r - emax) for r in le]
    esum = functools.reduce(lambda a, b: a + b, eexp)
    pe = [e / esum for e in eexp]
    sel, val = [], []
    for k in range(MOE_TOPK):
        bv, bi = None, None
        for e in range(ne):
            cand = pe[e]
            for prev in sel:
                cand = jnp.where(prev == e, -1.0, cand)
            if bv is None:
                bv, bi = cand, jnp.zeros(cand.shape, jnp.int32)
            else:
                better = cand > bv
                bi = jnp.where(better, e, bi)
                bv = jnp.where(better, cand, bv)
        sel.append(bi)
        val.append(bv)
    tot = functools.reduce(lambda a, b: a + b, val)
    zero_i = jnp.zeros((8 - MOE_TOPK,) + sel[0].shape[1:], jnp.int32)
    zero_f = jnp.zeros((8 - MOE_TOPK,) + sel[0].shape[1:], F32)
    eid_ref[...] = jnp.concatenate([gsel * ne + s for s in sel] + [zero_i], axis=0)
    gate_ref[...] = jnp.concatenate([best * v / tot for v in val] + [zero_f], axis=0)


def _router(lg_t, bias_col, tn=2048):
    nr, t = lg_t.shape
    tn = min(tn, t)
    return pl.pallas_call(
        _router_kernel,
        grid=(t // tn,),
        in_specs=[pl.BlockSpec((nr, tn), lambda i: (0, i)), _const_spec((nr, 1))],
        out_specs=(pl.BlockSpec((8, tn), lambda i: (0, i)), pl.BlockSpec((8, tn), lambda i: (0, i))),
        out_shape=(jax.ShapeDtypeStruct((8, t), jnp.int32), jax.ShapeDtypeStruct((8, t), F32)),
        compiler_params=_cparams(("parallel",)),
        name="router",
    )(lg_t, bias_col)


def _gather_rows_kernel(idx_ref, src_ref, dst_ref, sem, *, rows):
    base = pl.multiple_of(pl.program_id(0) * rows, rows)

    def start(i, carry):
        pltpu.make_async_copy(src_ref.at[pl.ds(idx_ref[i], 1)], dst_ref.at[pl.ds(base + i, 1)], sem).start()
        return carry

    lax.fori_loop(0, rows, start, 0)
    slab = dst_ref.at[pl.ds(base, rows)]
    pltpu.make_async_copy(slab, slab, sem).wait()


def _gather_rows(src, idx, rows=1024):
    n = idx.shape[0]
    return pl.pallas_call(
        functools.partial(_gather_rows_kernel, rows=rows),
        grid=(n // rows,),
        in_specs=[pl.BlockSpec((rows,), lambda i: (i,), memory_space=pltpu.SMEM),
                  pl.BlockSpec(memory_space=pl.ANY)],
        out_specs=pl.BlockSpec(memory_space=pl.ANY),
        out_shape=jax.ShapeDtypeStruct((n, src.shape[1]), src.dtype),
        scratch_shapes=[pltpu.SemaphoreType.DMA(())],
        compiler_params=_cparams(("arbitrary",)),
        name="gather_rows",
    )(idx, src)


def _expert_kernel(be_ref, nu_ref, x_ref, wgu_ref, wd_ref, y_ref, wgu_sc, wd_sc, *, ff):
    i = pl.program_id(0)
    changed = jnp.logical_or(i == 0, be_ref[i] != be_ref[jnp.maximum(i - 1, 0)])

    @pl.when(changed)
    def _():
        wgu_sc[...] = wgu_ref[0].astype(BF16)
        wd_sc[...] = wd_ref[0].astype(BF16)

    @pl.when(i < nu_ref[0])
    def _():
        gu = _dot(x_ref[...].astype(BF16), wgu_sc[...])
        gate = gu[:, :ff]
        h = gate * jax.nn.sigmoid(gate) * gu[:, ff:]
        y_ref[...] = _dot(h.astype(BF16), wd_sc[...])

    @pl.when(i >= nu_ref[0])
    def _():
        y_ref[...] = jnp.zeros(y_ref.shape, y_ref.dtype)


def _experts(xr, blk_exp, n_used, w_gate_up, w_down, rb):
    rows, d = xr.shape
    ff = w_down.shape[1]
    grid_spec = pltpu.PrefetchScalarGridSpec(
        num_scalar_prefetch=2,
        grid=(rows // rb,),
        in_specs=[pl.BlockSpec((rb, d), lambda i, be, nu: (i, 0)),
                  pl.BlockSpec((1, d, 2 * ff), lambda i, be, nu: (be[i], 0, 0)),
                  pl.BlockSpec((1, ff, d), lambda i, be, nu: (be[i], 0, 0))],
        out_specs=pl.BlockSpec((rb, d), lambda i, be, nu: (i, 0)),
        scratch_shapes=[pltpu.VMEM((d, 2 * ff), BF16), pltpu.VMEM((ff, d), BF16)],
    )
    return pl.pallas_call(
        functools.partial(_expert_kernel, ff=ff),
        grid_spec=grid_spec,
        out_shape=jax.ShapeDtypeStruct((rows, d), F32),
        compiler_params=_cparams(("arbitrary",), VMEM_LIMIT),
        name="experts",
    )(blk_exp, n_used, xr, w_gate_up, w_down)


def _moe_plan(eid, n_exp, rb):
    t, topk = eid.shape
    m = t * topk
    flat_e = eid.reshape(m)
    onehot = (flat_e[:, None] == jnp.arange(n_exp, dtype=jnp.int32)[None, :]).astype(jnp.int32)
    csum = jnp.cumsum(onehot, axis=0)
    counts = csum[-1]
    pcounts = (counts + rb - 1) // rb * rb
    pends = jnp.cumsum(pcounts)
    pstarts = pends - pcounts
    dest = jnp.sum(onehot * (pstarts[None, :] + csum - 1), axis=1)
    rows = m + n_exp * rb
    row_tok = jnp.zeros((rows,), jnp.int32).at[dest].set(jnp.arange(m, dtype=jnp.int32) // topk)
    nblk = rows // rb
    blk_start = jnp.arange(nblk, dtype=jnp.int32) * rb
    blk_exp = jnp.minimum(jnp.searchsorted(pends, blk_start, side='right'), n_exp - 1).astype(jnp.int32)
    n_used = (pends[-1] // rb).astype(jnp.int32).reshape(1)
    return dest.astype(jnp.int32), row_tok, blk_exp, n_used


def _moe_out_kernel(y_ref, x_ref, gate_ref, g_ref, b_ref, o_ref, *, alpha):
    d = x_ref.shape[1]
    gate = gate_ref[...]
    ffn = gate[:, 0:1] * y_ref[:, 0:d]
    for k in range(1, MOE_TOPK):
        ffn = ffn + gate[:, k:k + 1] * y_ref[:, k * d:(k + 1) * d]
    o_ref[...] = _layernorm_rows(alpha * x_ref[...] + ffn, g_ref[...], b_ref[...])


def _moe_out(y_tok, x2d, gate, ln_g, ln_b, alpha, tm=256):
    t, d = x2d.shape
    row = lambda i: (i, 0)
    return pl.pallas_call(
        functools.partial(_moe_out_kernel, alpha=alpha),
        grid=(t // tm,),
        in_specs=[pl.BlockSpec((tm, y_tok.shape[1]), row), pl.BlockSpec((tm, d), row), pl.BlockSpec((tm, gate.shape[1]), row),
                  _const_spec((1, d)), _const_spec((1, d))],
        out_specs=pl.BlockSpec((tm, d), row),
        out_shape=jax.ShapeDtypeStruct((t, d), F32),
        compiler_params=_cparams(("parallel",)),
        name="moe_out",
    )(y_tok, x2d, gate, ln_g.reshape(1, d).astype(F32), ln_b.reshape(1, d).astype(F32))


def _moe(x2, lg_t, b_group, b_expert, w_gate_up, w_down, ln_g, ln_b, alpha):
    t, d = x2.shape
    n_exp = w_gate_up.shape[0]
    nr = lg_t.shape[0]
    bias = jnp.zeros((nr, 1), F32).at[:MOE_GROUPS + n_exp, 0].set(jnp.concatenate([b_group, b_expert]).astype(F32))
    eid_t, gate_t = _router(lg_t, bias)
    eid = eid_t[:MOE_TOPK].T
    gate = gate_t[:MOE_TOPK].T
    dest, row_tok, blk_exp, n_used = _moe_plan(eid, n_exp, MOE_ROW_BLOCK)
    xr = _gather_rows(x2, row_tok)
    yr = _experts(xr, blk_exp, n_used, w_gate_up, w_down, MOE_ROW_BLOCK)
    y_tok = _gather_rows(yr, dest).reshape(t, MOE_TOPK * d)
    return _moe_out(y_tok, x2, gate, ln_g, ln_b, alpha)


def kernel(x, mem, positions, w_in, s5_lambda_re, s5_lambda_im, s5_log_step, s5_b_re, s5_b_im, s5_c_re, s5_c_im, s5_d, s5_w_glu, s5_b_glu, s5_out_norm, mla_q_norm, mla_w_uq, mla_kv_norm, mla_w_ukv, mla_out_norm, gdn_conv, gdn_a_log, gdn_dt_bias, gdn_out_norm, w_out, ln1_g, ln1_b, xa_w_q, xa_w_k, xa_w_v, xa_w_o, ln2_g, ln2_b, moe_w_group, moe_b_group, moe_w_expert, moe_b_expert, moe_w_gate_up, moe_w_down, ln3_g, ln3_b):
    batch, seq, d = x.shape
    t = batch * seq
    depth = w_in.shape[0]
    alpha = (2 * depth) ** 0.25
    mlen = mem.shape[1]
    s5_w = s5_w_glu.shape[1]
    rank_q = mla_w_uq.shape[1]
    rank_kv = mla_w_ukv.shape[1]
    g_heads = gdn_a_log.shape[1]
    g_qk = g_heads * GDN_DK
    g_v = gdn_conv.shape[2] - 2 * g_qk
    o_kr = s5_w + rank_q + rank_kv
    o_gq = o_kr + MLA_ROPE
    o_gz = o_gq + 2 * g_qk + g_v
    o_ga = o_gz + g_v
    widths = (s5_w, rank_q, rank_kv, LANES, 2 * g_qk + g_v, g_v)
    assert G_LANE == MLA_ROPE and B_LANE == G_LANE + g_heads and o_ga + 2 * g_heads == w_in.shape[2]

    cos_t, sin_t = _rope_tables(positions)
    mem2 = mem.reshape(batch * mlen, d)
    xt = x.reshape(t, d)
    for l in range(depth):
        w = w_in[l]
        w_packed = jnp.concatenate(
            [w[:, :o_gq], w[:, o_ga:], jnp.zeros((d, LANES - MLA_ROPE - 2 * g_heads), w.dtype), w[:, o_gq:o_ga]], axis=1).astype(BF16)
        u, cq, ckv, misc, qkv, gz = _in_proj(xt, w_packed, widths)

        tables = _s5_tables(s5_lambda_re[l], s5_lambda_im[l], s5_log_step[l], s5_b_re[l], s5_b_im[l], s5_c_re[l],
                            s5_c_im[l], s5_d[l], S5_CHUNK, seq // S5_CHUNK)
        y_s5 = _s5_glu(_s5_scan(u, tables, batch, seq, S5_CHUNK), s5_w_glu[l], s5_b_glu[l], s5_out_norm[l])

        qt, kk, vt = _mla_proj(cq, ckv, misc, cos_t, sin_t, mla_q_norm[l], mla_kv_norm[l], mla_w_uq[l], mla_w_ukv[l], batch, seq)
        o_mla = _mla_attention(qt, kk, vt).reshape(t, -1)

        gq, gk, gv, gkt, ggb, ggbt = _gdn_pre(qkv, misc, gdn_conv[l], gdn_a_log[l], gdn_dt_bias[l], batch, seq)
        local = _gdn_local(gq, gk, gv, gkt, ggb, ggbt)
        y_gdn = _gdn_scan(*local, gz, gdn_out_norm[l]).reshape(t, -1)

        x1 = _mix_out(y_s5, o_mla, y_gdn, xt, w_out[l], mla_out_norm[l], ln1_g[l], ln1_b[l], alpha)

        kv_mem = _matmul(mem2, jnp.concatenate([xa_w_k[l], xa_w_v[l]], axis=1))
        xa_w = xa_w_k.shape[2]
        kt_mem = kv_mem[:, :xa_w].reshape(batch, mlen, xa_w).transpose(0, 2, 1).astype(BF16)
        v_mem = kv_mem[:, xa_w:].reshape(batch, mlen, xa_w).astype(BF16)
        n_route = MOE_GROUPS + moe_w_expert.shape[2]
        w_router_t = jnp.pad(jnp.concatenate([moe_w_group[l], moe_w_expert[l]], axis=1).T.astype(F32),
                             ((0, (-n_route) % 8), (0, 0)))
        x2, lg_t = _xattn(x1, kt_mem, v_mem, xa_w_q[l], xa_w_o[l], ln2_g[l], ln2_b[l], w_router_t, seq, alpha)

        xt = _moe(x2, lg_t, moe_b_group[l], moe_b_expert[l], moe_w_gate_up[l], moe_w_down[l], ln3_g[l], ln3_b[l], alpha)
    return xt.reshape(batch, seq, d)
```

```python
import functools
import math

import jax
import jax.numpy as jnp
from jax import lax
from jax.experimental import pallas as pl
from jax.experimental.pallas import tpu as pltpu

F32 = jnp.float32
BF16 = jnp.bfloat16
HIGHEST = lax.Precision.HIGHEST

S5_CH = 16
S5_STATE = 64
S5_LAMBDA_RE_MAX = -1e-4
S5_CHUNK = 32
MLA_NOPE = 128
MLA_ROPE = 64
MLA_V = 128
ROPE_THETA = 10000.0
GDN_DK = 128
GDN_DV = 128
GDN_CONV = 4
GDN_CHUNK = 64
XA_DH = 128
MOE_GROUPS = 4
MOE_PER_GROUP = 8
MOE_TOPK = 2
MOE_ROW_BLOCK = 256

LANES = 128
VMEM_LIMIT = 56 * 1024 * 1024


def _cparams(sem, vmem=None, flags=None):
    return pltpu.CompilerParams(dimension_semantics=sem, vmem_limit_bytes=vmem, flags=flags)


def _const_spec(shape):
    nd = len(shape)
    return pl.BlockSpec(shape, lambda *_: (0,) * nd)


def _rms_rows(x, gain, eps=1e-6):
    return x * lax.rsqrt(jnp.mean(x * x, axis=-1, keepdims=True) + eps) * gain


def _layernorm_rows(x, g, b, eps=1e-5):
    mu = jnp.mean(x, axis=-1, keepdims=True)
    xc = x - mu
    var = jnp.mean(xc * xc, axis=-1, keepdims=True)
    return xc * lax.rsqrt(var + eps) * g + b


def _dot(a, b):
    return jnp.dot(a, b, preferred_element_type=F32)


def _in_proj_kernel(x_ref, w_ref, u_ref, cq_ref, ckv_ref, misc_ref, qkv_ref, gz_ref, *, splits):
    xb = x_ref[...].astype(BF16)
    outs = (u_ref, cq_ref, ckv_ref, misc_ref, qkv_ref, gz_ref)
    for o_ref, (lo, hi) in zip(outs, splits):
        o_ref[...] = _dot(xb, w_ref[:, lo:hi]).astype(o_ref.dtype)


def _in_proj(x2d, w_packed, widths, tm=256):
    t, d = x2d.shape
    splits, lo = [], 0
    for w in widths:
        splits.append((lo, lo + w))
        lo += w
    out_shape = tuple(jax.ShapeDtypeStruct((t, w), F32) for w in widths)
    out_specs = tuple(pl.BlockSpec((tm, w), lambda i: (i, 0)) for w in widths)
    return pl.pallas_call(
        functools.partial(_in_proj_kernel, splits=tuple(splits)),
        grid=(t // tm,),
        in_specs=[pl.BlockSpec((tm, d), lambda i: (i, 0)), _const_spec(w_packed.shape)],
        out_specs=out_specs,
        out_shape=out_shape,
        compiler_params=_cparams(("parallel",), VMEM_LIMIT),
        name="in_proj",
    )(x2d, w_packed)


def _s5_tables(lam_re, lam_im, log_step, b_re, b_im, c_re, c_im, d_skip, chunk, n_chunks):
    g, p = lam_re.shape
    h = b_re.shape[-1]
    lr = jnp.minimum(lam_re.astype(F32), S5_LAMBDA_RE_MAX)
    li = lam_im.astype(F32)
    dt = jnp.exp(log_step.astype(F32))[:, None]
    mag = jnp.exp(lr * dt)
    th = li * dt
    ab_re, ab_im = mag * jnp.cos(th), mag * jnp.sin(th)
    den = lr * lr + li * li
    nr, ni = ab_re - 1.0, ab_im
    fr = (nr * lr + ni * li) / den
    fi = (ni * lr - nr * li) / den
    br, bi = b_re.astype(F32), b_im.astype(F32)
    bb_re = fr[..., None] * br - fi[..., None] * bi
    bb_im = fr[..., None] * bi + fi[..., None] * br
    cr, ci = c_re.astype(F32), c_im.astype(F32)
    n = jnp.arange(chunk + 1, dtype=F32)[:, None, None]
    pmag = jnp.exp(n * (lr * dt)[None])
    pr, pi = pmag * jnp.cos(n * th[None]), pmag * jnp.sin(n * th[None])

    cb_re = cr.transpose(0, 2, 1)[:, :, :, None] * bb_re[:, :, None, :] - ci.transpose(0, 2, 1)[:, :, :, None] * bb_im[:, :, None, :]
    cb_im = cr.transpose(0, 2, 1)[:, :, :, None] * bb_im[:, :, None, :] + ci.transpose(0, 2, 1)[:, :, :, None] * bb_re[:, :, None, :]
    kk = (jnp.einsum('tgp,gphk->tghk', pr[:chunk], cb_re, precision=HIGHEST)
          - jnp.einsum('tgp,gphk->tghk', pi[:chunk], cb_im, precision=HIGHEST))
    kk = kk.at[0].add(jnp.eye(h, dtype=F32)[None] * d_skip.astype(F32)[:, :, None])
    jj = jnp.arange(chunk)
    lag = jj[None, :] - jj[:, None]
    toe = jnp.where((lag >= 0)[:, :, None, None, None], kk[jnp.clip(lag, 0)], 0.0)
    m_intra = toe.transpose(2, 0, 4, 1, 3).reshape(g, chunk * h, chunk * h)

    pr_rev, pi_rev = pr[chunk - 1::-1][:chunk], pi[chunk - 1::-1][:chunk]
    e_re = pr_rev[:, :, :, None] * bb_re[None] - pi_rev[:, :, :, None] * bb_im[None]
    e_im = pr_rev[:, :, :, None] * bb_im[None] + pi_rev[:, :, :, None] * bb_re[None]
    e_mat = jnp.concatenate([e_re, e_im], axis=2).transpose(1, 0, 3, 2).reshape(g, chunk * h, 2 * p)

    pr1, pi1 = pr[1:], pi[1:]
    f_re = cr[None] * pr1[:, :, None, :] - ci[None] * pi1[:, :, None, :]
    f_im = cr[None] * pi1[:, :, None, :] + ci[None] * pr1[:, :, None, :]
    f_mat = jnp.concatenate([f_re, -f_im], axis=3).transpose(1, 3, 0, 2).reshape(g, 2 * p, chunk * h)

    steps = max(1, int(math.ceil(math.log2(n_chunks))))
    ar, ai = pr[chunk], pi[chunk]
    a1, a2 = [], []
    for _ in range(steps):
        a1.append(jnp.concatenate([ar, ar], axis=-1))
        a2.append(jnp.concatenate([-ai, ai], axis=-1))
        ar, ai = ar * ar - ai * ai, 2.0 * ar * ai
    pad = (-steps) % 8
    a1 = jnp.pad(jnp.stack(a1, axis=1), ((0, 0), (0, pad), (0, 0)))
    a2 = jnp.pad(jnp.stack(a2, axis=1), ((0, 0), (0, pad), (0, 0)))
    return m_intra.astype(BF16), e_mat.astype(BF16), f_mat.astype(BF16), a1, a2


def _s5_kernel(u_ref, m_ref, e_ref, f_ref, a1_ref, a2_ref, y_ref, *, n_chunks, steps):
    u = u_ref[0]
    y = _dot(u, m_ref[0])
    s = _dot(u, e_ref[0])
    rows, width = s.shape
    half = width // 2
    c_idx = lax.broadcasted_iota(jnp.int32, (rows, width), 0) % n_chunks
    a1 = a1_ref[0]
    a2 = a2_ref[0]
    for k in range(steps):
        sh = 1 << k
        prev = jnp.where(c_idx >= sh, pltpu.roll(s, sh, axis=0), 0.0)
        s = s + a1[k:k + 1, :] * prev + a2[k:k + 1, :] * pltpu.roll(prev, half, axis=1)
    s_in = jnp.where(c_idx >= 1, pltpu.roll(s, 1, axis=0), 0.0)
    y_ref[0] = y + _dot(s_in.astype(BF16), f_ref[0])


def _s5_scan(u, tables, batch, seq, chunk):
    m_intra, e_mat, f_mat, a1, a2 = tables
    g = m_intra.shape[0]
    h = S5_CH
    n_chunks = seq // chunk
    rows = batch * n_chunks
    steps = max(1, int(math.ceil(math.log2(n_chunks))))
    ug = u.reshape(batch, n_chunks, chunk, g, h).transpose(3, 0, 1, 2, 4).reshape(g, rows, chunk * h).astype(BF16)
    y = pl.pallas_call(
        functools.partial(_s5_kernel, n_chunks=n_chunks, steps=steps),
        grid=(g,),
        in_specs=[pl.BlockSpec((1, rows, chunk * h), lambda i: (i, 0, 0)),
                  pl.BlockSpec((1,) + m_intra.shape[1:], lambda i: (i, 0, 0)),
                  pl.BlockSpec((1,) + e_mat.shape[1:], lambda i: (i, 0, 0)),
                  pl.BlockSpec((1,) + f_mat.shape[1:], lambda i: (i, 0, 0)),
                  pl.BlockSpec((1,) + a1.shape[1:], lambda i: (i, 0, 0)),
                  pl.BlockSpec((1,) + a2.shape[1:], lambda i: (i, 0, 0))],
        out_specs=pl.BlockSpec((1, rows, chunk * h), lambda i: (i, 0, 0)),
        out_shape=jax.ShapeDtypeStruct((g, rows, chunk * h), F32),
        compiler_params=_cparams(("parallel",), VMEM_LIMIT),
        name="s5_scan",
    )(ug, m_intra, e_mat, f_mat, a1, a2)
    return y.reshape(g, batch, n_chunks, chunk, h).transpose(1, 2, 3, 0, 4).reshape(batch * seq, g * h)


def _s5_glu_kernel(y_ref, w_ref, b_ref, g_ref, o_ref):
    y = jax.nn.gelu(y_ref[...])
    z = _dot(y.astype(BF16), w_ref[...]) + b_ref[...]
    y = y * jax.nn.sigmoid(z)
    o_ref[...] = _rms_rows(y, g_ref[...]).astype(o_ref.dtype)


def _s5_glu(y, w_glu, b_glu, g_out, tm=1024):
    t, w = y.shape
    return pl.pallas_call(
        _s5_glu_kernel,
        grid=(t // tm,),
        in_specs=[pl.BlockSpec((tm, w), lambda i: (i, 0)), _const_spec((w, w)), _const_spec((1, w)), _const_spec((1, w))],
        out_specs=pl.BlockSpec((tm, w), lambda i: (i, 0)),
        out_shape=jax.ShapeDtypeStruct((t, w), BF16),
        compiler_params=_cparams(("parallel",)),
        name="s5_glu",
    )(y, w_glu.astype(BF16), b_glu.reshape(1, w).astype(F32), g_out.reshape(1, w).astype(F32))


def _mla_proj_kernel(cq_ref, ckv_ref, misc_ref, cos_ref, sin_ref, qn_ref, kvn_ref, wuq_ref, wukv_ref,
                     qt_ref, k_ref, vt_ref, *, heads, scale):
    cq = _rms_rows(cq_ref[...], qn_ref[...]).astype(BF16)
    q = _dot(cq, wuq_ref[...]) * scale
    ckv = _rms_rows(ckv_ref[...], kvn_ref[...]).astype(BF16)
    kv = _dot(ckv, wukv_ref[...])
    cos = cos_ref[...]
    sin = sin_ref[...]
    lane = lax.broadcasted_iota(jnp.int32, cos.shape, 1)
    first_half = (lane % MLA_ROPE) < (MLA_ROPE // 2)

    def rope(x):
        partner = jnp.where(first_half, pltpu.roll(x, LANES - MLA_ROPE // 2, axis=1), pltpu.roll(x, MLA_ROPE // 2, axis=1))
        return x * cos + partner * sin

    kpe = rope(misc_ref[...])
    kpe_lo = jnp.where(lane < MLA_ROPE, kpe, 0.0)
    kpe_hi = pltpu.roll(kpe_lo, MLA_ROPE, axis=1)
    nope_w = heads * MLA_NOPE
    for pair in range(heads // 2):
        q_pe = rope(q[:, nope_w + LANES * pair:nope_w + LANES * (pair + 1)])
        for h in (2 * pair, 2 * pair + 1):
            qh = jnp.concatenate([q[:, MLA_NOPE * h:MLA_NOPE * (h + 1)], q_pe], axis=1)
            qt_ref[0, h] = qh.T.astype(BF16)
            kvw = MLA_NOPE + MLA_V
            kh = jnp.concatenate([kv[:, kvw * h:kvw * h + MLA_NOPE], kpe_lo if h % 2 == 0 else kpe_hi], axis=1)
            k_ref[0, h] = kh.astype(BF16)
            vt_ref[0, h] = kv[:, kvw * h + MLA_NOPE:kvw * (h + 1)].T.astype(BF16)


def _mla_proj(cq, ckv, misc, cos_t, sin_t, q_norm, kv_norm, w_uq, w_ukv, batch, seq, ts=256):
    t, rank = cq.shape
    heads = w_ukv.shape[1] // (MLA_NOPE + MLA_V)
    dq = MLA_NOPE + MLA_ROPE
    w3 = w_uq.reshape(rank, heads, dq)
    w_uq_p = jnp.concatenate([w3[:, :, :MLA_NOPE].reshape(rank, -1), w3[:, :, MLA_NOPE:].reshape(rank, -1)], axis=1).astype(BF16)
    per = seq // ts
    dk = MLA_NOPE + LANES
    return pl.pallas_call(
        functools.partial(_mla_proj_kernel, heads=heads, scale=dq ** -0.5),
        grid=(t // ts,),
        in_specs=[pl.BlockSpec((ts, rank), lambda i: (i, 0)), pl.BlockSpec((ts, rank), lambda i: (i, 0)),
                  pl.BlockSpec((ts, LANES), lambda i: (i, 0)), pl.BlockSpec((ts, LANES), lambda i: (i, 0)),
                  pl.BlockSpec((ts, LANES), lambda i: (i, 0)),
                  _const_spec((1, rank)), _const_spec((1, rank)), _const_spec(w_uq_p.shape), _const_spec(w_ukv.shape)],
        out_specs=(pl.BlockSpec((1, heads, dk, ts), lambda i: (i // per, 0, 0, i % per)),
                   pl.BlockSpec((1, heads, ts, dk), lambda i: (i // per, 0, i % per, 0)),
                   pl.BlockSpec((1, heads, MLA_V, ts), lambda i: (i // per, 0, 0, i % per))),
        out_shape=(jax.ShapeDtypeStruct((batch, heads, dk, seq), BF16),
                   jax.ShapeDtypeStruct((batch, heads, seq, dk), BF16),
                   jax.ShapeDtypeStruct((batch, heads, MLA_V, seq), BF16)),
        compiler_params=_cparams(("parallel",), VMEM_LIMIT),
        name="mla_proj",
    )(cq, ckv, misc, cos_t, sin_t, q_norm.reshape(1, rank).astype(F32), kv_norm.reshape(1, rank).astype(F32),
      w_uq_p, w_ukv.astype(BF16))


def _flash_kernel(qt_ref, k_ref, vt_ref, o_ref, s_sc, acc_sc, *, tq, tk, sub):
    qi = pl.program_id(2)
    qt = qt_ref[0, 0]
    nsub = tk // sub

    def scores(j, slot):
        k0 = pl.multiple_of(j * tk, tk)
        s_sc[slot] = _dot(k_ref[0, 0, pl.ds(k0, tk), :], qt)

    def consume(j, m_prev, l_prev, masked):
        slot = lax.rem(j, 2)
        k0 = pl.multiple_of(j * tk, tk)
        ss = [s_sc[slot, r * sub:(r + 1) * sub, :] for r in range(nsub)]
        if masked:
            qpos = qi * tq + lax.broadcasted_iota(jnp.int32, ss[0].shape, 1)
            kpos = k0 + lax.broadcasted_iota(jnp.int32, ss[0].shape, 0)
            ss = [jnp.where(kpos + r * sub <= qpos, s, -1e30) for r, s in enumerate(ss)]
        else:
            scores(j + 1, 1 - slot)
        m_new = functools.reduce(jnp.maximum, [jnp.max(s, axis=0, keepdims=True) for s in ss], m_prev)
        alpha = jnp.exp(m_prev - m_new)
        ps = [jnp.exp(s - m_new) for s in ss]
        l_new = alpha * l_prev + functools.reduce(lambda a, b: a + b, [jnp.sum(p, axis=0, keepdims=True) for p in ps])
        pv = [_dot(vt_ref[0, 0, :, pl.ds(pl.multiple_of(k0 + r * sub, sub), sub)], ps[r].astype(BF16)) for r in range(nsub)]
        acc_sc[...] = alpha * acc_sc[...] + functools.reduce(lambda a, b: a + b, pv)
        return m_new, l_new

    scores(0, 0)
    acc_sc[...] = jnp.zeros(acc_sc.shape, F32)
    m0 = jnp.full((1, tq), -1e30, F32)
    l0 = jnp.zeros((1, tq), F32)
    m, l = lax.fori_loop(0, qi, lambda j, c: consume(j, c[0], c[1], False), (m0, l0))
    m, l = consume(qi, m, l, True)
    o_ref[0] = (acc_sc[...] / l).T.astype(o_ref.dtype)


def _mla_attention(qt, k, vt, tq=512, sub=256):
    batch, heads, dk, seq = qt.shape
    dv = vt.shape[2]
    tk = tq
    return pl.pallas_call(
        functools.partial(_flash_kernel, tq=tq, tk=tk, sub=min(sub, tk)),
        grid=(batch, heads, seq // tq),
        in_specs=[pl.BlockSpec((1, 1, dk, tq), lambda b, h, qi: (b, h, 0, qi)),
                  pl.BlockSpec((1, 1, seq, dk), lambda b, h, qi: (b, h, 0, 0)),
                  pl.BlockSpec((1, 1, dv, seq), lambda b, h, qi: (b, h, 0, 0))],
        out_specs=pl.BlockSpec((1, tq, dv), lambda b, h, qi: (b, qi, h)),
        out_shape=jax.ShapeDtypeStruct((batch, seq, heads * dv), F32),
        scratch_shapes=[pltpu.VMEM((2, tk, tq), F32), pltpu.VMEM((dv, tq), F32)],
        compiler_params=_cparams(("parallel", "parallel", "arbitrary"), VMEM_LIMIT),
        name="mla_flash",
    )(qt, k, vt)


def _rope_tables(positions):
    half = MLA_ROPE // 2
    inv_freq = 1.0 / (ROPE_THETA ** (jnp.arange(half, dtype=F32) * (2.0 / MLA_ROPE)))
    ang = positions.astype(F32).reshape(-1)[:, None] * inv_freq
    cos, sin = jnp.cos(ang), jnp.sin(ang)
    reps = LANES // MLA_ROPE
    return jnp.tile(jnp.concatenate([cos, cos], axis=1), (1, reps)), jnp.tile(jnp.concatenate([-sin, sin], axis=1), (1, reps))


G_LANE = 64
B_LANE = 68


def _gdn_pre_kernel(x_ref, prev_ref, misc_ref, cw_ref, alog_ref, dtb_ref,
                    q_ref, k_ref, v_ref, kt_ref, gb_ref, gbt_ref, grow_ref, *, heads, chunk):
    x = x_ref[0]
    ts = x.shape[0]
    prev = jnp.where(pl.program_id(1) > 0, prev_ref[0], 0.0)
    cw = cw_ref[...]
    row8 = lax.broadcasted_iota(jnp.int32, prev.shape, 0)
    acc = x * cw[GDN_CONV - 1:GDN_CONV, :]
    for d in range(1, GDN_CONV):
        xr = pltpu.roll(x, d, axis=0)
        head = jnp.where(row8 < d, pltpu.roll(prev, d, axis=0), xr[0:8])
        xs = jnp.concatenate([head, xr[8:]], axis=0)
        acc = acc + xs * cw[GDN_CONV - 1 - d:GDN_CONV - d, :]
    y = acc * jax.nn.sigmoid(acc)
    nqk = heads * GDN_DK

    def l2n(z):
        return z * lax.rsqrt(jnp.sum(z * z, axis=-1, keepdims=True) + 1e-6)

    for h in range(heads):
        q_ref[0, :, GDN_DK * h:GDN_DK * (h + 1)] = l2n(y[:, GDN_DK * h:GDN_DK * (h + 1)])
    kn = jnp.concatenate([l2n(y[:, nqk + GDN_DK * h:nqk + GDN_DK * (h + 1)]) for h in range(heads)], axis=1)
    k_ref[0] = kn
    v_ref[0] = y[:, 2 * nqk:]
    knt = kn.T
    for n in range(ts // chunk):
        kt_ref[0, n] = knt[:, chunk * n:chunk * (n + 1)]
    m = misc_ref[0]
    lane = lax.broadcasted_iota(jnp.int32, m.shape, 1)
    g = -jnp.exp(alog_ref[...]) * jax.nn.softplus(m + dtb_ref[...])
    beta = jax.nn.sigmoid(m)
    gb = jnp.where((lane >= G_LANE) & (lane < G_LANE + heads), g,
                   jnp.where((lane >= B_LANE) & (lane < B_LANE + heads), beta, 0.0))
    gb_ref[0] = gb
    gbt = gb.T[G_LANE:G_LANE + 8, :]
    for n in range(ts // chunk):
        gbt_ref[0, n] = gbt[:, chunk * n:chunk * (n + 1)]
        g_rows = jnp.concatenate([gbt[h:h + 1, chunk * n:chunk * (n + 1)] for h in range(heads)], axis=1)
        grow_ref[0, n] = jnp.broadcast_to(g_rows, (8, heads * chunk))


def _gdn_pre(qkv, misc, w_conv, a_log, dt_bias, batch, seq, ts=256):
    width = qkv.shape[-1]
    heads = a_log.shape[0]
    chunk = GDN_CHUNK
    hd = heads * GDN_DK
    x3 = qkv.reshape(batch, seq, width)
    m3 = misc.reshape(batch, seq, LANES)
    alog_row = jnp.zeros((1, LANES), F32).at[0, G_LANE:G_LANE + heads].set(a_log.astype(F32))
    dtb_row = jnp.zeros((1, LANES), F32).at[0, G_LANE:G_LANE + heads].set(dt_bias.astype(F32))
    nck = ts // chunk
    tok = lambda b, s: (b, s, 0)
    return pl.pallas_call(
        functools.partial(_gdn_pre_kernel, heads=heads, chunk=chunk),
        grid=(batch, seq // ts),
        in_specs=[pl.BlockSpec((1, ts, width), tok),
                  pl.BlockSpec((1, 8, width), lambda b, s: (b, jnp.maximum(s * (ts // 8) - 1, 0), 0)),
                  pl.BlockSpec((1, ts, LANES), tok),
                  _const_spec(w_conv.shape), _const_spec((1, LANES)), _const_spec((1, LANES))],
        out_specs=(pl.BlockSpec((1, ts, hd), tok), pl.BlockSpec((1, ts, hd), tok), pl.BlockSpec((1, ts, width - 2 * hd), tok),
                   pl.BlockSpec((1, nck, hd, chunk), lambda b, s: (b, s, 0, 0)),
                   pl.BlockSpec((1, ts, LANES), tok),
                   pl.BlockSpec((1, nck, 8, chunk), lambda b, s: (b, s, 0, 0)),
                   pl.BlockSpec((1, nck, 8, heads * chunk), lambda b, s: (b, s, 0, 0))),
        out_shape=(jax.ShapeDtypeStruct((batch, seq, hd), F32), jax.ShapeDtypeStruct((batch, seq, hd), F32),
                   jax.ShapeDtypeStruct((batch, seq, width - 2 * hd), F32),
                   jax.ShapeDtypeStruct((batch, seq // chunk, hd, chunk), F32),
                   jax.ShapeDtypeStruct((batch, seq, LANES), F32),
                   jax.ShapeDtypeStruct((batch, seq // chunk, 8, chunk), F32),
                   jax.ShapeDtypeStruct((batch, seq // chunk, 8, heads * chunk), F32)),
        compiler_params=_cparams(("parallel", "parallel"), VMEM_LIMIT),
        name="gdn_pre",
    )(x3, x3, m3, w_conv.astype(F32), alog_row, dtb_row)


def _hdot(a, b):
    return jnp.dot(a, b, preferred_element_type=F32, precision=HIGHEST)


def _gdn_local_kernel(q_ref, k_ref, v_ref, kt_ref, gb_ref, gbt_ref, grow_ref,
                      u_ref, w_ref, qd_ref, a_ref, kend_ref, egl_ref, *, heads, chunk, n_chunks):
    c = chunk
    hc = heads * c
    hd = heads * GDN_DK
    iota = lambda shape, ax: lax.broadcasted_iota(jnp.int32, shape, ax)
    ri, li = iota((c, hc), 0), iota((c, hc), 1)
    lj, lh = li % c, li // c
    tri_cat = ri >= lj
    strict_cat = ri > lj
    eye_cat = (ri == lj).astype(F32)
    r2, l2 = iota((hc, hc), 0), iota((hc, hc), 1)
    same_blk = (r2 // c) == (l2 // c)
    tri_bd = jnp.logical_and(same_blk, (r2 % c) <= (l2 % c)).astype(F32)
    head_rows = (iota((hc, hd), 0) // c) == (iota((hc, hd), 1) // GDN_DK)
    r1, c1 = iota((c, c), 0), iota((c, c), 1)
    tri_f = (r1 >= c1).astype(F32)
    tri_t = (r1 <= c1).astype(F32)
    nt = (((1,), (1,)), ((), ()))

    def bdiag(x):
        return jnp.where(same_blk, jnp.concatenate([x] * heads, axis=0), 0.0)

    def bdiag_wide(x):
        return jnp.where(head_rows, jnp.concatenate([x] * heads, axis=0), 0.0)

    def per_head_cols(cols, width):
        return jnp.concatenate([jnp.broadcast_to(col, (c, width)) for col in cols], axis=1)

    st = []
    for n in range(n_chunks):
        r0 = n * c
        gbc = gb_ref[0, r0:r0 + c, :]
        gcc = _hdot(tri_f, gbc)
        gc_cols = [gcc[:, G_LANE + h:G_LANE + h + 1] for h in range(heads)]
        gc_c = jnp.broadcast_to(gc_cols[0], (c, hc))
        for h in range(1, heads):
            gc_c = jnp.where(lh == h, jnp.broadcast_to(gc_cols[h], (c, hc)), gc_c)
        gc_r = _hdot(grow_ref[0, n], tri_bd)[0:1, :]
        decay = jnp.where(tri_cat, jnp.exp(jnp.where(tri_cat, gc_c - gc_r, 0.0)), 0.0)
        beta_w = per_head_cols([gbc[:, B_LANE + h:B_LANE + h + 1] for h in range(heads)], GDN_DK)
        eg_w = per_head_cols([jnp.exp(col) for col in gc_cols], GDN_DK)
        q = q_ref[0, r0:r0 + c, :] * (GDN_DK ** -0.5)
        k = k_ref[0, r0:r0 + c, :]
        v = v_ref[0, r0:r0 + c, :]
        kb = k * beta_w
        k_bd = bdiag_wide(k).astype(BF16)
        kk = lax.dot_general(kb.astype(BF16), k_bd, nt, preferred_element_type=F32)
        qk = lax.dot_general(q.astype(BF16), k_bd, nt, preferred_element_type=F32)
        lmat = jnp.where(strict_cat, kk * decay, 0.0)
        qd_ref[0, r0:r0 + c, :] = (q * eg_w).astype(qd_ref.dtype)
        a_ref[0, r0:r0 + c, :] = jnp.where(tri_cat, qk * decay, 0.0).astype(a_ref.dtype)
        gcr = _hdot(gbt_ref[0, n], tri_t)
        g_last = [gcr[h:h + 1, c - 1:c] for h in range(heads)]
        f = jnp.concatenate([jnp.broadcast_to(jnp.exp(g_last[h] - gcr[h:h + 1, :]), (GDN_DK, c)) for h in range(heads)], axis=0)
        kend_ref[0, n] = (kt_ref[0, n] * f).astype(kend_ref.dtype)
        egl_ref[0, n] = jnp.concatenate([jnp.broadcast_to(jnp.exp(g_last[h]), (1, LANES)) for h in range(heads)]
                                        + [jnp.zeros((8 - heads, LANES), F32)], axis=0)
        st.append(dict(p=eye_cat - lmat, sq=lmat, vb=v * beta_w, kbe=kb * eg_w))
    kpow = 2
    while kpow < c:
        for d in st:
            d["sq"] = _dot(d["sq"].astype(BF16), bdiag(d["sq"]).astype(BF16))
        for d in st:
            d["p"] = d["p"] + _dot(d["p"].astype(BF16), bdiag(d["sq"]).astype(BF16))
        kpow *= 2
    for n, d in enumerate(st):
        r0 = n * c
        tmat = d["p"].astype(BF16)
        u_ref[0, r0:r0 + c, :] = _dot(tmat, bdiag_wide(d["vb"]).astype(BF16))
        w_ref[0, r0:r0 + c, :] = _dot(tmat, bdiag_wide(d["kbe"]).astype(BF16)).astype(w_ref.dtype)


def _gdn_local(q, k, v, kt, gb, gbt, grow, cb=4):
    batch, seq, hd = q.shape
    heads = hd // GDN_DK
    chunk = GDN_CHUNK
    n_all = seq // chunk
    ts = cb * chunk
    tok = lambda b, s: (b, s, 0)
    ck = lambda b, s: (b, s, 0, 0)
    return pl.pallas_call(
        functools.partial(_gdn_local_kernel, heads=heads, chunk=chunk, n_chunks=cb),
        grid=(batch, n_all // cb),
        in_specs=[pl.BlockSpec((1, ts, hd), tok), pl.BlockSpec((1, ts, hd), tok), pl.BlockSpec((1, ts, hd), tok),
                  pl.BlockSpec((1, cb, hd, chunk), ck), pl.BlockSpec((1, ts, LANES), tok), pl.BlockSpec((1, cb, 8, chunk), ck),
                  pl.BlockSpec((1, cb, 8, heads * chunk), ck)],
        out_specs=(pl.BlockSpec((1, ts, hd), tok), pl.BlockSpec((1, ts, hd), tok), pl.BlockSpec((1, ts, hd), tok),
                   pl.BlockSpec((1, ts, heads * chunk), tok), pl.BlockSpec((1, cb, hd, chunk), ck),
                   pl.BlockSpec((1, cb, 8, LANES), ck)),
        out_shape=(jax.ShapeDtypeStruct((batch, seq, hd), F32), jax.ShapeDtypeStruct((batch, seq, hd), BF16),
                   jax.ShapeDtypeStruct((batch, seq, hd), BF16), jax.ShapeDtypeStruct((batch, seq, heads * chunk), BF16),
                   jax.ShapeDtypeStruct((batch, n_all, hd, chunk), BF16), jax.ShapeDtypeStruct((batch, n_all, 8, LANES), F32)),
        compiler_params=_cparams(("parallel", "parallel"), VMEM_LIMIT),
        name="gdn_local",
    )(q, k, v, kt, gb, gbt, grow)


def _gdn_scan_kernel(u_ref, w_ref, qd_ref, a_ref, kend_ref, egl_ref, gz_ref, gn_ref, o_ref, st_ref, *, heads, chunk, n_chunks):
    c = chunk

    @pl.when(pl.program_id(1) == 0)
    def _():
        st_ref[...] = jnp.zeros(st_ref.shape, F32)

    gn = gn_ref[...]
    state = [st_ref[h] for h in range(heads)]
    col = lambda h: slice(GDN_DV * h, GDN_DV * (h + 1))
    for n in range(n_chunks):
        r0 = n * c
        egl = egl_ref[0, n]
        sb = [st.astype(BF16) for st in state]
        v_new = [u_ref[0, r0:r0 + c, col(h)] - _dot(w_ref[0, r0:r0 + c, col(h)], sb[h]) for h in range(heads)]
        vb = [x.astype(BF16) for x in v_new]
        state = [state[h] * egl[h:h + 1, :] + _dot(kend_ref[0, n, col(h), :], vb[h]) for h in range(heads)]
        for h in range(heads):
            o = _dot(qd_ref[0, r0:r0 + c, col(h)], sb[h]) + _dot(a_ref[0, r0:r0 + c, c * h:c * (h + 1)], vb[h])
            z = gz_ref[0, r0:r0 + c, col(h)]
            o_ref[0, r0:r0 + c, col(h)] = (_rms_rows(o, gn) * (z * jax.nn.sigmoid(z))).astype(o_ref.dtype)
    for h in range(heads):
        st_ref[h] = state[h]


def _gdn_scan(u, w, qd, a, kend, egl, gz, g_out, cs=4):
    batch, seq, hd = u.shape
    heads = hd // GDN_DV
    chunk = GDN_CHUNK
    n_all = seq // chunk
    ts = cs * chunk
    tok = lambda b, s: (b, s, 0)
    ck = lambda b, s: (b, s, 0, 0)
    return pl.pallas_call(
        functools.partial(_gdn_scan_kernel, heads=heads, chunk=chunk, n_chunks=cs),
        grid=(batch, n_all // cs),
        in_specs=[pl.BlockSpec((1, ts, hd), tok), pl.BlockSpec((1, ts, hd), tok), pl.BlockSpec((1, ts, hd), tok),
                  pl.BlockSpec((1, ts, heads * chunk), tok), pl.BlockSpec((1, cs, hd, chunk), ck),
                  pl.BlockSpec((1, cs, 8, LANES), ck), pl.BlockSpec((1, ts, hd), tok), _const_spec((1, GDN_DV))],
        out_specs=pl.BlockSpec((1, ts, hd), tok),
        out_shape=jax.ShapeDtypeStruct((batch, seq, hd), BF16),
        scratch_shapes=[pltpu.VMEM((heads, GDN_DK, GDN_DV), F32)],
        compiler_params=_cparams(("parallel", "arbitrary"), VMEM_LIMIT),
        name="gdn_scan",
    )(u, w, qd, a, kend, egl, gz.reshape(batch, seq, hd), g_out.reshape(1, GDN_DV).astype(F32))


def _mix_out_kernel(s5_ref, mla_ref, gdn_ref, x_ref, w_ref, mg_ref, g_ref, b_ref, o_ref, *, alpha):
    w5 = s5_ref.shape[1]
    wm = mla_ref.shape[1]
    mla = _rms_rows(mla_ref[...], mg_ref[...]).astype(BF16)
    acc = _dot(s5_ref[...], w_ref[0:w5, :]) + _dot(mla, w_ref[w5:w5 + wm, :]) + _dot(gdn_ref[...], w_ref[w5 + wm:, :])
    o_ref[...] = _layernorm_rows(alpha * x_ref[...] + acc, g_ref[...], b_ref[...])


def _mix_out(y_s5, o_mla, y_gdn, x2d, w_out, mla_gain, ln_g, ln_b, alpha, tm=256):
    t, d = x2d.shape
    row = lambda i: (i, 0)
    return pl.pallas_call(
        functools.partial(_mix_out_kernel, alpha=alpha),
        grid=(t // tm,),
        in_specs=[pl.BlockSpec((tm, y_s5.shape[1]), row), pl.BlockSpec((tm, o_mla.shape[1]), row),
                  pl.BlockSpec((tm, y_gdn.shape[1]), row), pl.BlockSpec((tm, d), row),
                  _const_spec(w_out.shape), _const_spec((1, o_mla.shape[1])), _const_spec((1, d)), _const_spec((1, d))],
        out_specs=pl.BlockSpec((tm, d), row),
        out_shape=jax.ShapeDtypeStruct((t, d), F32),
        compiler_params=_cparams(("parallel",), VMEM_LIMIT),
        name="mix_out",
    )(y_s5, o_mla, y_gdn, x2d, w_out.astype(BF16), mla_gain.reshape(1, -1).astype(F32),
      ln_g.reshape(1, d).astype(F32), ln_b.reshape(1, d).astype(F32))


def _matmul_kernel(x_ref, w_ref, o_ref):
    o_ref[...] = _dot(x_ref[...].astype(BF16), w_ref[...]).astype(o_ref.dtype)


def _matmul(x, w, tm=256, tn=512):
    m, k = x.shape
    n = w.shape[1]
    return pl.pallas_call(
        _matmul_kernel,
        grid=(m // tm, n // tn),
        in_specs=[pl.BlockSpec((tm, k), lambda i, j: (i, 0)), pl.BlockSpec((k, tn), lambda i, j: (0, j))],
        out_specs=pl.BlockSpec((tm, tn), lambda i, j: (i, j)),
        out_shape=jax.ShapeDtypeStruct((m, n), F32),
        compiler_params=_cparams(("parallel", "parallel")),
        name="mem_kv_proj",
    )(x, w.astype(BF16))


def _xattn_kernel(x_ref, wq_ref, kt_ref, v_ref, wo_ref, g_ref, b_ref, wr_ref, o_ref, lg_ref, *, heads, alpha):
    x = x_ref[...]
    q = (_dot(x.astype(BF16), wq_ref[...]) * (XA_DH ** -0.5)).astype(BF16)
    outs = []
    for h in range(heads):
        cols = slice(XA_DH * h, XA_DH * (h + 1))
        s = _dot(q[:, cols], kt_ref[0, cols, :])
        s = s - jnp.max(s, axis=-1, keepdims=True)
        p = jnp.exp(s)
        p = p / jnp.sum(p, axis=-1, keepdims=True)
        outs.append(_dot(p.astype(BF16), v_ref[0, :, cols]))
    o = jnp.concatenate(outs, axis=1).astype(BF16)
    y = _layernorm_rows(alpha * x + _dot(o, wo_ref[...]), g_ref[...], b_ref[...])
    o_ref[...] = y
    lg_ref[...] = lax.dot_general(wr_ref[...], y, (((1,), (1,)), ((), ())), precision=HIGHEST, preferred_element_type=F32)


def _xattn(x2d, kt, v, w_q, w_o, ln_g, ln_b, w_router_t, seq, alpha, tm=256):
    t, d = x2d.shape
    width = w_q.shape[1]
    heads = width // XA_DH
    mlen = v.shape[1]
    per = seq // tm
    nr = w_router_t.shape[0]
    row = lambda i: (i, 0)
    return pl.pallas_call(
        functools.partial(_xattn_kernel, heads=heads, alpha=alpha),
        grid=(t // tm,),
        in_specs=[pl.BlockSpec((tm, d), row), _const_spec((d, width)),
                  pl.BlockSpec((1, width, mlen), lambda i: (i // per, 0, 0)),
                  pl.BlockSpec((1, mlen, width), lambda i: (i // per, 0, 0)),
                  _const_spec((width, d)), _const_spec((1, d)), _const_spec((1, d)), _const_spec((nr, d))],
        out_specs=(pl.BlockSpec((tm, d), row), pl.BlockSpec((nr, tm), lambda i: (0, i))),
        out_shape=(jax.ShapeDtypeStruct((t, d), F32), jax.ShapeDtypeStruct((nr, t), F32)),
        compiler_params=_cparams(("parallel",), VMEM_LIMIT),
        name="xattn",
    )(x2d, w_q.astype(BF16), kt, v, w_o.astype(BF16), ln_g.reshape(1, d).astype(F32), ln_b.reshape(1, d).astype(F32), w_router_t)


def _router_kernel(lg_ref, bias_ref, eid_ref, gate_ref):
    lg = lg_ref[...] + bias_ref[...]
    ng, ne = MOE_GROUPS, MOE_PER_GROUP
    grp = [lg[g:g + 1, :] for g in range(ng)]
    gmax = functools.reduce(jnp.maximum, grp)
    gexp = [jnp.exp(r - gmax) for r in grp]
    gsum = functools.reduce(lambda a, b: a + b, gexp)
    pg = [e / gsum for e in gexp]
    best, gsel = pg[0], jnp.zeros(pg[0].shape, jnp.int32)
    for g in range(1, ng):
        better = pg[g] > best
        gsel = jnp.where(better, g, gsel)
        best = jnp.where(better, pg[g], best)
    le = []
    for e in range(ne):
        r = lg[ng + e:ng + e + 1, :]
        for g in range(1, ng):
            r = jnp.where(gsel == g, lg[ng + g * ne + e:ng + g * ne + e + 1, :], r)
        le.append(r)
    emax = functools.reduce(jnp.maximum, le)
    eexp = [jnp.exp(r - emax) for r in le]
    esum = functools.reduce(lambda a, b: a + b, eexp)
    pe = [e / esum for e in eexp]
    sel, val = [], []
    for k in range(MOE_TOPK):
        bv, bi = None, None
        for e in range(ne):
            cand = pe[e]
            for prev in sel:
                cand = jnp.where(prev == e, -1.0, cand)
            if bv is None:
                bv, bi = cand, jnp.zeros(cand.shape, jnp.int32)
            else:
                better = cand > bv
                bi = jnp.where(better, e, bi)
                bv = jnp.where(better, cand, bv)
        sel.append(bi)
        val.append(bv)
    tot = functools.reduce(lambda a, b: a + b, val)
    zero_i = jnp.zeros((8 - MOE_TOPK,) + sel[0].shape[1:], jnp.int32)
    zero_f = jnp.zeros((8 - MOE_TOPK,) + sel[0].shape[1:], F32)
    eid_ref[...] = jnp.concatenate([gsel * ne + s for s in sel] + [zero_i], axis=0)
    gate_ref[...] = jnp.concatenate([best * v / tot for v in val] + [zero_f], axis=0)


def _router(lg_t, bias_col, tn=2048):
    nr, t = lg_t.shape
    tn = min(tn, t)
    return pl.pallas_call(
        _router_kernel,
        grid=(t // tn,),
        in_specs=[pl.BlockSpec((nr, tn), lambda i: (0, i)), _const_spec((nr, 1))],
        out_specs=(pl.BlockSpec((8, tn), lambda i: (0, i)), pl.BlockSpec((8, tn), lambda i: (0, i))),
        out_shape=(jax.ShapeDtypeStruct((8, t), jnp.int32), jax.ShapeDtypeStruct((8, t), F32)),
        compiler_params=_cparams(("parallel",)),
        name="router",
    )(lg_t, bias_col)


def _gather_rows_kernel(idx_ref, src_ref, o_ref, sem, *, rows):
    def start(i, carry):
        pltpu.make_async_copy(src_ref.at[pl.ds(idx_ref[i], 1)], o_ref.at[pl.ds(i, 1)], sem).start()
        return carry

    lax.fori_loop(0, rows, start, 0, unroll=8)
    pltpu.make_async_copy(o_ref, o_ref, sem).wait()


def _gather_rows(src, idx, rows=1024):
    n = idx.shape[0]
    d = src.shape[1]
    return pl.pallas_call(
        functools.partial(_gather_rows_kernel, rows=rows),
        grid=(n // rows,),
        in_specs=[pl.BlockSpec((rows,), lambda i: (i,), memory_space=pltpu.SMEM),
                  pl.BlockSpec(memory_space=pl.ANY)],
        out_specs=pl.BlockSpec((rows, d), lambda i: (i, 0)),
        out_shape=jax.ShapeDtypeStruct((n, d), src.dtype),
        scratch_shapes=[pltpu.SemaphoreType.DMA(())],
        compiler_params=_cparams(("arbitrary",), VMEM_LIMIT),
        name="gather_rows",
    )(idx, src)


def _expert_kernel(be_ref, nu_ref, x_ref, wgu_ref, wd_ref, y_ref, wgu_sc, wd_sc, *, ff):
    i = pl.program_id(0)
    changed = jnp.logical_or(i == 0, be_ref[i] != be_ref[jnp.maximum(i - 1, 0)])

    @pl.when(changed)
    def _():
        wgu_sc[...] = wgu_ref[0, 0].astype(BF16)
        wd_sc[...] = wd_ref[0, 0].astype(BF16)

    @pl.when(i < nu_ref[0])
    def _():
        gu = _dot(x_ref[...].astype(BF16), wgu_sc[...])
        gate = gu[:, :ff]
        h = gate * jax.nn.sigmoid(gate) * gu[:, ff:]
        y_ref[...] = _dot(h.astype(BF16), wd_sc[...])

    @pl.when(i >= nu_ref[0])
    def _():
        y_ref[...] = jnp.zeros(y_ref.shape, y_ref.dtype)


def _experts(xr, blk_exp, n_used, w_gate_up, w_down, layer, rb):
    rows, d = xr.shape
    ff = w_down.shape[2]
    grid_spec = pltpu.PrefetchScalarGridSpec(
        num_scalar_prefetch=2,
        grid=(rows // rb,),
        in_specs=[pl.BlockSpec((rb, d), lambda i, be, nu: (i, 0)),
                  pl.BlockSpec((1, 1, d, 2 * ff), lambda i, be, nu: (layer, be[i], 0, 0)),
                  pl.BlockSpec((1, 1, ff, d), lambda i, be, nu: (layer, be[i], 0, 0))],
        out_specs=pl.BlockSpec((rb, d), lambda i, be, nu: (i, 0)),
        scratch_shapes=[pltpu.VMEM((d, 2 * ff), BF16), pltpu.VMEM((ff, d), BF16)],
    )
    return pl.pallas_call(
        functools.partial(_expert_kernel, ff=ff),
        grid_spec=grid_spec,
        out_shape=jax.ShapeDtypeStruct((rows, d), F32),
        compiler_params=_cparams(("arbitrary",), VMEM_LIMIT),
        name="experts",
    )(blk_exp, n_used, xr, w_gate_up, w_down)


def _moe_plan(eid, n_exp, rb):
    t, topk = eid.shape
    m = t * topk
    flat_e = eid.reshape(m)
    onehot = (flat_e[:, None] == jnp.arange(n_exp, dtype=jnp.int32)[None, :]).astype(jnp.int32)
    csum = jnp.cumsum(onehot, axis=0)
    counts = csum[-1]
    pcounts = (counts + rb - 1) // rb * rb
    pends = jnp.cumsum(pcounts)
    pstarts = pends - pcounts
    dest = jnp.sum(onehot * (pstarts[None, :] + csum - 1), axis=1)
    rows = m + n_exp * rb
    row_tok = jnp.zeros((rows,), jnp.int32).at[dest].set(jnp.arange(m, dtype=jnp.int32) // topk)
    nblk = rows // rb
    blk_start = jnp.arange(nblk, dtype=jnp.int32) * rb
    blk_exp = jnp.minimum(jnp.sum((pends[None, :] <= blk_start[:, None]).astype(jnp.int32), axis=1), n_exp - 1)
    n_used = (pends[-1] // rb).astype(jnp.int32).reshape(1)
    return dest.astype(jnp.int32), row_tok, blk_exp, n_used


def _moe_out_kernel(*refs, alpha):
    y_refs = refs[:MOE_TOPK]
    x_ref, gate_ref, g_ref, b_ref, o_ref = refs[MOE_TOPK:]
    gate = gate_ref[...]
    ffn = gate[:, 0:1] * y_refs[0][0]
    for k in range(1, MOE_TOPK):
        ffn = ffn + gate[:, k:k + 1] * y_refs[k][0]
    o_ref[...] = _layernorm_rows(alpha * x_ref[...] + ffn, g_ref[...], b_ref[...])


def _moe_out(y_tok, x2d, gate, ln_g, ln_b, alpha, tm=256):
    t, d = x2d.shape
    row = lambda i: (i, 0)
    return pl.pallas_call(
        functools.partial(_moe_out_kernel, alpha=alpha),
        grid=(t // tm,),
        in_specs=[pl.BlockSpec((1, tm, d), functools.partial(lambda k, i: (k, i, 0), k)) for k in range(MOE_TOPK)]
                 + [pl.BlockSpec((tm, d), row), pl.BlockSpec((tm, gate.shape[1]), row), _const_spec((1, d)), _const_spec((1, d))],
        out_specs=pl.BlockSpec((tm, d), row),
        out_shape=jax.ShapeDtypeStruct((t, d), F32),
        compiler_params=_cparams(("parallel",)),
        name="moe_out",
    )(*([y_tok] * MOE_TOPK), x2d, gate, ln_g.reshape(1, d).astype(F32), ln_b.reshape(1, d).astype(F32))


def _moe(x2, lg_t, b_group, b_expert, w_gate_up, w_down, layer, ln_g, ln_b, alpha):
    t, d = x2.shape
    n_exp = w_gate_up.shape[1]
    nr = lg_t.shape[0]
    bias = jnp.zeros((nr, 1), F32).at[:MOE_GROUPS + n_exp, 0].set(jnp.concatenate([b_group, b_expert]).astype(F32))
    eid_t, gate_t = _router(lg_t, bias)
    eid = eid_t[:MOE_TOPK].T
    gate = gate_t[:MOE_TOPK].T
    dest, row_tok, blk_exp, n_used = _moe_plan(eid, n_exp, MOE_ROW_BLOCK)
    xr = _gather_rows(x2, row_tok)
    yr = _experts(xr, blk_exp, n_used, w_gate_up, w_down, layer, MOE_ROW_BLOCK)
    y_tok = _gather_rows(yr, dest.reshape(t, MOE_TOPK).T.reshape(-1)).reshape(MOE_TOPK, t, d)
    return _moe_out(y_tok, x2, gate, ln_g, ln_b, alpha)


def kernel(x, mem, positions, w_in, s5_lambda_re, s5_lambda_im, s5_log_step, s5_b_re, s5_b_im, s5_c_re, s5_c_im, s5_d, s5_w_glu, s5_b_glu, s5_out_norm, mla_q_norm, mla_w_uq, mla_kv_norm, mla_w_ukv, mla_out_norm, gdn_conv, gdn_a_log, gdn_dt_bias, gdn_out_norm, w_out, ln1_g, ln1_b, xa_w_q, xa_w_k, xa_w_v, xa_w_o, ln2_g, ln2_b, moe_w_group, moe_b_group, moe_w_expert, moe_b_expert, moe_w_gate_up, moe_w_down, ln3_g, ln3_b):
    batch, seq, d = x.shape
    t = batch * seq
    depth = w_in.shape[0]
    alpha = (2 * depth) ** 0.25
    mlen = mem.shape[1]
    s5_w = s5_w_glu.shape[1]
    rank_q = mla_w_uq.shape[1]
    rank_kv = mla_w_ukv.shape[1]
    g_heads = gdn_a_log.shape[1]
    g_qk = g_heads * GDN_DK
    g_v = gdn_conv.shape[2] - 2 * g_qk
    o_kr = s5_w + rank_q + rank_kv
    o_gq = o_kr + MLA_ROPE
    o_gz = o_gq + 2 * g_qk + g_v
    o_ga = o_gz + g_v
    widths = (s5_w, rank_q, rank_kv, LANES, 2 * g_qk + g_v, g_v)
    assert G_LANE == MLA_ROPE and B_LANE == G_LANE + g_heads and o_ga + 2 * g_heads == w_in.shape[2]

    cos_t, sin_t = _rope_tables(positions)
    mem2 = mem.reshape(batch * mlen, d)
    xt = x.reshape(t, d)
    for l in range(depth):
        w = w_in[l]
        w_packed = jnp.concatenate(
            [w[:, :o_gq], w[:, o_ga:], jnp.zeros((d, LANES - MLA_ROPE - 2 * g_heads), w.dtype), w[:, o_gq:o_ga]], axis=1).astype(BF16)
        u, cq, ckv, misc, qkv, gz = _in_proj(xt, w_packed, widths)

        tables = _s5_tables(s5_lambda_re[l], s5_lambda_im[l], s5_log_step[l], s5_b_re[l], s5_b_im[l], s5_c_re[l],
                            s5_c_im[l], s5_d[l], S5_CHUNK, seq // S5_CHUNK)
        y_s5 = _s5_glu(_s5_scan(u, tables, batch, seq, S5_CHUNK), s5_w_glu[l], s5_b_glu[l], s5_out_norm[l])

        qt, kk, vt = _mla_proj(cq, ckv, misc, cos_t, sin_t, mla_q_norm[l], mla_kv_norm[l], mla_w_uq[l], mla_w_ukv[l], batch, seq)
        o_mla = _mla_attention(qt, kk, vt).reshape(t, -1)

        gq, gk, gv, gkt, ggb, ggbt, ggrow = _gdn_pre(qkv, misc, gdn_conv[l], gdn_a_log[l], gdn_dt_bias[l], batch, seq)
        local = _gdn_local(gq, gk, gv, gkt, ggb, ggbt, ggrow)
        y_gdn = _gdn_scan(*local, gz, gdn_out_norm[l]).reshape(t, -1)

        x1 = _mix_out(y_s5, o_mla, y_gdn, xt, w_out[l], mla_out_norm[l], ln1_g[l], ln1_b[l], alpha)

        kv_mem = _matmul(mem2, jnp.concatenate([xa_w_k[l], xa_w_v[l]], axis=1))
        xa_w = xa_w_k.shape[2]
        kt_mem = kv_mem[:, :xa_w].reshape(batch, mlen, xa_w).transpose(0, 2, 1).astype(BF16)
        v_mem = kv_mem[:, xa_w:].reshape(batch, mlen, xa_w).astype(BF16)
        n_route = MOE_GROUPS + moe_w_expert.shape[2]
        w_router_t = jnp.pad(jnp.concatenate([moe_w_group[l], moe_w_expert[l]], axis=1).T.astype(F32),
                             ((0, (-n_route) % 8), (0, 0)))
        x2, lg_t = _xattn(x1, kt_mem, v_mem, xa_w_q[l], xa_w_o[l], ln2_g[l], ln2_b[l], w_router_t, seq, alpha)

        xt = _moe(x2, lg_t, moe_b_group[l], moe_b_expert[l], moe_w_gate_up, moe_w_down, l, ln3_g[l], ln3_b[l], alpha)
    return xt.reshape(batch, seq, d)
```

```python
import functools
import math

import jax
import jax.numpy as jnp
from jax import lax
from jax.experimental import pallas as pl
from jax.experimental.pallas import tpu as pltpu

F32 = jnp.float32
BF16 = jnp.bfloat16
HIGHEST = lax.Precision.HIGHEST

S5_CH = 16
S5_STATE = 64
S5_LAMBDA_RE_MAX = -1e-4
S5_CHUNK = 32
MLA_NOPE = 128
MLA_ROPE = 64
MLA_V = 128
ROPE_THETA = 10000.0
GDN_DK = 128
GDN_DV = 128
GDN_CONV = 4
GDN_CHUNK = 64
XA_DH = 128
MOE_GROUPS = 4
MOE_PER_GROUP = 8
MOE_TOPK = 2
MOE_ROW_BLOCK = 256

LANES = 128
VMEM_LIMIT = 56 * 1024 * 1024


def _cparams(sem, vmem=None, flags=None):
    return pltpu.CompilerParams(dimension_semantics=sem, vmem_limit_bytes=vmem, flags=flags)


def _const_spec(shape):
    nd = len(shape)
    return pl.BlockSpec(shape, lambda *_: (0,) * nd)


def _rms_rows(x, gain, eps=1e-6):
    return x * lax.rsqrt(jnp.mean(x * x, axis=-1, keepdims=True) + eps) * gain


def _layernorm_rows(x, g, b, eps=1e-5):
    mu = jnp.mean(x, axis=-1, keepdims=True)
    xc = x - mu
    var = jnp.mean(xc * xc, axis=-1, keepdims=True)
    return xc * lax.rsqrt(var + eps) * g + b


def _dot(a, b):
    return jnp.dot(a, b, preferred_element_type=F32)


def _in_proj_kernel(x_ref, w_ref, u_ref, cq_ref, ckv_ref, misc_ref, qkv_ref, gz_ref, *, splits):
    xb = x_ref[...].astype(BF16)
    outs = (u_ref, cq_ref, ckv_ref, misc_ref, qkv_ref, gz_ref)
    for o_ref, (lo, hi) in zip(outs, splits):
        o_ref[...] = _dot(xb, w_ref[:, lo:hi]).astype(o_ref.dtype)


def _in_proj(x2d, w_packed, widths, tm=256):
    t, d = x2d.shape
    splits, lo = [], 0
    for w in widths:
        splits.append((lo, lo + w))
        lo += w
    out_shape = tuple(jax.ShapeDtypeStruct((t, w), F32) for w in widths)
    out_specs = tuple(pl.BlockSpec((tm, w), lambda i: (i, 0)) for w in widths)
    return pl.pallas_call(
        functools.partial(_in_proj_kernel, splits=tuple(splits)),
        grid=(t // tm,),
        in_specs=[pl.BlockSpec((tm, d), lambda i: (i, 0)), _const_spec(w_packed.shape)],
        out_specs=out_specs,
        out_shape=out_shape,
        compiler_params=_cparams(("parallel",), VMEM_LIMIT),
        name="in_proj",
    )(x2d, w_packed)


def _s5_tables(lam_re, lam_im, log_step, b_re, b_im, c_re, c_im, d_skip, chunk, n_chunks):
    g, p = lam_re.shape
    h = b_re.shape[-1]
    lr = jnp.minimum(lam_re.astype(F32), S5_LAMBDA_RE_MAX)
    li = lam_im.astype(F32)
    dt = jnp.exp(log_step.astype(F32))[:, None]
    mag = jnp.exp(lr * dt)
    th = li * dt
    ab_re, ab_im = mag * jnp.cos(th), mag * jnp.sin(th)
    den = lr * lr + li * li
    nr, ni = ab_re - 1.0, ab_im
    fr = (nr * lr + ni * li) / den
    fi = (ni * lr - nr * li) / den
    br, bi = b_re.astype(F32), b_im.astype(F32)
    bb_re = fr[..., None] * br - fi[..., None] * bi
    bb_im = fr[..., None] * bi + fi[..., None] * br
    cr, ci = c_re.astype(F32), c_im.astype(F32)
    n = jnp.arange(chunk + 1, dtype=F32)[:, None, None]
    pmag = jnp.exp(n * (lr * dt)[None])
    pr, pi = pmag * jnp.cos(n * th[None]), pmag * jnp.sin(n * th[None])

    cb_re = cr.transpose(0, 2, 1)[:, :, :, None] * bb_re[:, :, None, :] - ci.transpose(0, 2, 1)[:, :, :, None] * bb_im[:, :, None, :]
    cb_im = cr.transpose(0, 2, 1)[:, :, :, None] * bb_im[:, :, None, :] + ci.transpose(0, 2, 1)[:, :, :, None] * bb_re[:, :, None, :]
    kk = (jnp.einsum('tgp,gphk->tghk', pr[:chunk], cb_re, precision=HIGHEST)
          - jnp.einsum('tgp,gphk->tghk', pi[:chunk], cb_im, precision=HIGHEST))
    kk = kk.at[0].add(jnp.eye(h, dtype=F32)[None] * d_skip.astype(F32)[:, :, None])
    lags = kk.transpose(1, 3, 0, 2).reshape(g, h, chunk * h)
    padded = jnp.concatenate([jnp.zeros_like(lags), lags], axis=-1)
    m_intra = jnp.stack([padded[:, :, (chunk - j) * h:(2 * chunk - j) * h] for j in range(chunk)], axis=1)
    m_intra = m_intra.reshape(g, chunk * h, chunk * h)

    pr_rev, pi_rev = pr[chunk - 1::-1][:chunk], pi[chunk - 1::-1][:chunk]
    e_re = pr_rev[:, :, :, None] * bb_re[None] - pi_rev[:, :, :, None] * bb_im[None]
    e_im = pr_rev[:, :, :, None] * bb_im[None] + pi_rev[:, :, :, None] * bb_re[None]
    e_mat = jnp.concatenate([e_re, e_im], axis=2).transpose(1, 0, 3, 2).reshape(g, chunk * h, 2 * p)

    pr1, pi1 = pr[1:], pi[1:]
    f_re = cr[None] * pr1[:, :, None, :] - ci[None] * pi1[:, :, None, :]
    f_im = cr[None] * pi1[:, :, None, :] + ci[None] * pr1[:, :, None, :]
    f_mat = jnp.concatenate([f_re, -f_im], axis=3).transpose(1, 3, 0, 2).reshape(g, 2 * p, chunk * h)

    steps = max(1, int(math.ceil(math.log2(n_chunks))))
    ar, ai = pr[chunk], pi[chunk]
    a1, a2 = [], []
    for _ in range(steps):
        a1.append(jnp.concatenate([ar, ar], axis=-1))
        a2.append(jnp.concatenate([-ai, ai], axis=-1))
        ar, ai = ar * ar - ai * ai, 2.0 * ar * ai
    pad = (-steps) % 8
    a1 = jnp.pad(jnp.stack(a1, axis=1), ((0, 0), (0, pad), (0, 0)))
    a2 = jnp.pad(jnp.stack(a2, axis=1), ((0, 0), (0, pad), (0, 0)))
    return m_intra.astype(BF16), e_mat.astype(BF16), f_mat.astype(BF16), a1, a2


def _s5_kernel(u_ref, m_ref, e_ref, f_ref, a1_ref, a2_ref, y_ref, *, n_chunks, steps):
    u = u_ref[0]
    y = _dot(u, m_ref[0])
    s = _dot(u, e_ref[0])
    rows, width = s.shape
    half = width // 2
    c_idx = lax.broadcasted_iota(jnp.int32, (rows, width), 0) % n_chunks
    a1 = a1_ref[0]
    a2 = a2_ref[0]
    for k in range(steps):
        sh = 1 << k
        prev = jnp.where(c_idx >= sh, pltpu.roll(s, sh, axis=0), 0.0)
        s = s + a1[k:k + 1, :] * prev + a2[k:k + 1, :] * pltpu.roll(prev, half, axis=1)
    s_in = jnp.where(c_idx >= 1, pltpu.roll(s, 1, axis=0), 0.0)
    y_ref[0] = y + _dot(s_in.astype(BF16), f_ref[0])


def _s5_scan(u, tables, batch, seq, chunk):
    m_intra, e_mat, f_mat, a1, a2 = tables
    g = m_intra.shape[0]
    h = S5_CH
    n_chunks = seq // chunk
    rows = batch * n_chunks
    steps = max(1, int(math.ceil(math.log2(n_chunks))))
    ug = u.reshape(batch, n_chunks, chunk, g, h).transpose(3, 0, 1, 2, 4).reshape(g, rows, chunk * h).astype(BF16)
    y = pl.pallas_call(
        functools.partial(_s5_kernel, n_chunks=n_chunks, steps=steps),
        grid=(g,),
        in_specs=[pl.BlockSpec((1, rows, chunk * h), lambda i: (i, 0, 0)),
                  pl.BlockSpec((1,) + m_intra.shape[1:], lambda i: (i, 0, 0)),
                  pl.BlockSpec((1,) + e_mat.shape[1:], lambda i: (i, 0, 0)),
                  pl.BlockSpec((1,) + f_mat.shape[1:], lambda i: (i, 0, 0)),
                  pl.BlockSpec((1,) + a1.shape[1:], lambda i: (i, 0, 0)),
                  pl.BlockSpec((1,) + a2.shape[1:], lambda i: (i, 0, 0))],
        out_specs=pl.BlockSpec((1, rows, chunk * h), lambda i: (i, 0, 0)),
        out_shape=jax.ShapeDtypeStruct((g, rows, chunk * h), F32),
        compiler_params=_cparams(("parallel",), VMEM_LIMIT),
        name="s5_scan",
    )(ug, m_intra, e_mat, f_mat, a1, a2)
    return y.reshape(g, batch, n_chunks, chunk, h).transpose(1, 2, 3, 0, 4).reshape(batch * seq, g * h)


def _s5_glu_kernel(y_ref, w_ref, b_ref, g_ref, o_ref):
    y = jax.nn.gelu(y_ref[...])
    z = _dot(y.astype(BF16), w_ref[...]) + b_ref[...]
    y = y * jax.nn.sigmoid(z)
    o_ref[...] = _rms_rows(y, g_ref[...]).astype(o_ref.dtype)


def _s5_glu(y, w_glu, b_glu, g_out, tm=1024):
    t, w = y.shape
    return pl.pallas_call(
        _s5_glu_kernel,
        grid=(t // tm,),
        in_specs=[pl.BlockSpec((tm, w), lambda i: (i, 0)), _const_spec((w, w)), _const_spec((1, w)), _const_spec((1, w))],
        out_specs=pl.BlockSpec((tm, w), lambda i: (i, 0)),
        out_shape=jax.ShapeDtypeStruct((t, w), BF16),
        compiler_params=_cparams(("parallel",)),
        name="s5_glu",
    )(y, w_glu.astype(BF16), b_glu.reshape(1, w).astype(F32), g_out.reshape(1, w).astype(F32))


def _mla_proj_kernel(cq_ref, ckv_ref, misc_ref, cos_ref, sin_ref, qn_ref, kvn_ref, wuq_ref, wukv_ref,
                     qt_ref, k_ref, vt_ref, *, heads, scale):
    cq = _rms_rows(cq_ref[...], qn_ref[...]).astype(BF16)
    q = _dot(cq, wuq_ref[...]) * scale
    ckv = _rms_rows(ckv_ref[...], kvn_ref[...]).astype(BF16)
    kv = _dot(ckv, wukv_ref[...])
    cos = cos_ref[...]
    sin = sin_ref[...]
    lane = lax.broadcasted_iota(jnp.int32, cos.shape, 1)
    first_half = (lane % MLA_ROPE) < (MLA_ROPE // 2)

    def rope(x):
        partner = jnp.where(first_half, pltpu.roll(x, LANES - MLA_ROPE // 2, axis=1), pltpu.roll(x, MLA_ROPE // 2, axis=1))
        return x * cos + partner * sin

    kpe = rope(misc_ref[...])
    kpe_lo = jnp.where(lane < MLA_ROPE, kpe, 0.0)
    kpe_hi = pltpu.roll(kpe_lo, MLA_ROPE, axis=1)
    nope_w = heads * MLA_NOPE
    for pair in range(heads // 2):
        q_pe = rope(q[:, nope_w + LANES * pair:nope_w + LANES * (pair + 1)])
        for h in (2 * pair, 2 * pair + 1):
            qh = jnp.concatenate([q[:, MLA_NOPE * h:MLA_NOPE * (h + 1)], q_pe], axis=1)
            qt_ref[0, h] = qh.T.astype(BF16)
            kvw = MLA_NOPE + MLA_V
            kh = jnp.concatenate([kv[:, kvw * h:kvw * h + MLA_NOPE], kpe_lo if h % 2 == 0 else kpe_hi], axis=1)
            k_ref[0, h] = kh.astype(BF16)
            vt_ref[0, h] = kv[:, kvw * h + MLA_NOPE:kvw * (h + 1)].T.astype(BF16)


def _mla_proj(cq, ckv, misc, cos_t, sin_t, q_norm, kv_norm, w_uq, w_ukv, batch, seq, ts=256):
    t, rank = cq.shape
    heads = w_ukv.shape[1] // (MLA_NOPE + MLA_V)
    dq = MLA_NOPE + MLA_ROPE
    w3 = w_uq.reshape(rank, heads, dq)
    w_uq_p = jnp.concatenate([w3[:, :, :MLA_NOPE].reshape(rank, -1), w3[:, :, MLA_NOPE:].reshape(rank, -1)], axis=1).astype(BF16)
    per = seq // ts
    dk = MLA_NOPE + LANES
    return pl.pallas_call(
        functools.partial(_mla_proj_kernel, heads=heads, scale=dq ** -0.5),
        grid=(t // ts,),
        in_specs=[pl.BlockSpec((ts, rank), lambda i: (i, 0)), pl.BlockSpec((ts, rank), lambda i: (i, 0)),
                  pl.BlockSpec((ts, LANES), lambda i: (i, 0)), pl.BlockSpec((ts, LANES), lambda i: (i, 0)),
                  pl.BlockSpec((ts, LANES), lambda i: (i, 0)),
                  _const_spec((1, rank)), _const_spec((1, rank)), _const_spec(w_uq_p.shape), _const_spec(w_ukv.shape)],
        out_specs=(pl.BlockSpec((1, heads, dk, ts), lambda i: (i // per, 0, 0, i % per)),
                   pl.BlockSpec((1, heads, ts, dk), lambda i: (i // per, 0, i % per, 0)),
                   pl.BlockSpec((1, heads, MLA_V, ts), lambda i: (i // per, 0, 0, i % per))),
        out_shape=(jax.ShapeDtypeStruct((batch, heads, dk, seq), BF16),
                   jax.ShapeDtypeStruct((batch, heads, seq, dk), BF16),
                   jax.ShapeDtypeStruct((batch, heads, MLA_V, seq), BF16)),
        compiler_params=_cparams(("parallel",), VMEM_LIMIT),
        name="mla_proj",
    )(cq, ckv, misc, cos_t, sin_t, q_norm.reshape(1, rank).astype(F32), kv_norm.reshape(1, rank).astype(F32),
      w_uq_p, w_ukv.astype(BF16))


def _flash_kernel(qt_ref, k_ref, vt_ref, o_ref, s_sc, acc_sc, *, tq, tk, sub):
    qi = pl.program_id(2)
    qt = qt_ref[0, 0]
    nsub = tk // sub

    def scores(j, slot):
        k0 = pl.multiple_of(j * tk, tk)
        s_sc[slot] = _dot(k_ref[0, 0, pl.ds(k0, tk), :], qt)

    def consume(j, m_prev, l_prev, masked):
        slot = lax.rem(j, 2)
        k0 = pl.multiple_of(j * tk, tk)
        ss = [s_sc[slot, r * sub:(r + 1) * sub, :] for r in range(nsub)]
        if masked:
            qpos = qi * tq + lax.broadcasted_iota(jnp.int32, ss[0].shape, 1)
            kpos = k0 + lax.broadcasted_iota(jnp.int32, ss[0].shape, 0)
            ss = [jnp.where(kpos + r * sub <= qpos, s, -1e30) for r, s in enumerate(ss)]
        else:
            scores(j + 1, 1 - slot)
        m_new = functools.reduce(jnp.maximum, [jnp.max(s, axis=0, keepdims=True) for s in ss], m_prev)
        alpha = jnp.exp(m_prev - m_new)
        ps = [jnp.exp(s - m_new) for s in ss]
        l_new = alpha * l_prev + functools.reduce(lambda a, b: a + b, [jnp.sum(p, axis=0, keepdims=True) for p in ps])
        pv = [_dot(vt_ref[0, 0, :, pl.ds(pl.multiple_of(k0 + r * sub, sub), sub)], ps[r].astype(BF16)) for r in range(nsub)]
        acc_sc[...] = alpha * acc_sc[...] + functools.reduce(lambda a, b: a + b, pv)
        return m_new, l_new

    scores(0, 0)
    acc_sc[...] = jnp.zeros(acc_sc.shape, F32)
    m0 = jnp.full((1, tq), -1e30, F32)
    l0 = jnp.zeros((1, tq), F32)
    m, l = lax.fori_loop(0, qi, lambda j, c: consume(j, c[0], c[1], False), (m0, l0))
    m, l = consume(qi, m, l, True)
    o_ref[0] = (acc_sc[...] / l).T.astype(o_ref.dtype)


def _mla_attention(qt, k, vt, tq=512, sub=256):
    batch, heads, dk, seq = qt.shape
    dv = vt.shape[2]
    tk = tq
    return pl.pallas_call(
        functools.partial(_flash_kernel, tq=tq, tk=tk, sub=min(sub, tk)),
        grid=(batch, heads, seq // tq),
        in_specs=[pl.BlockSpec((1, 1, dk, tq), lambda b, h, qi: (b, h, 0, qi)),
                  pl.BlockSpec((1, 1, seq, dk), lambda b, h, qi: (b, h, 0, 0)),
                  pl.BlockSpec((1, 1, dv, seq), lambda b, h, qi: (b, h, 0, 0))],
        out_specs=pl.BlockSpec((1, tq, dv), lambda b, h, qi: (b, qi, h)),
        out_shape=jax.ShapeDtypeStruct((batch, seq, heads * dv), F32),
        scratch_shapes=[pltpu.VMEM((2, tk, tq), F32), pltpu.VMEM((dv, tq), F32)],
        compiler_params=_cparams(("parallel", "parallel", "arbitrary"), VMEM_LIMIT),
        name="mla_flash",
    )(qt, k, vt)


def _rope_tables(positions):
    half = MLA_ROPE // 2
    inv_freq = 1.0 / (ROPE_THETA ** (jnp.arange(half, dtype=F32) * (2.0 / MLA_ROPE)))
    ang = positions.astype(F32).reshape(-1)[:, None] * inv_freq
    cos, sin = jnp.cos(ang), jnp.sin(ang)
    reps = LANES // MLA_ROPE
    return jnp.tile(jnp.concatenate([cos, cos], axis=1), (1, reps)), jnp.tile(jnp.concatenate([-sin, sin], axis=1), (1, reps))


G_LANE = 64
B_LANE = 68


def _gdn_pre_kernel(x_ref, prev_ref, misc_ref, cw_ref, alog_ref, dtb_ref,
                    q_ref, k_ref, v_ref, kt_ref, gb_ref, gbt_ref, grow_ref, *, heads, chunk):
    x = x_ref[0]
    ts = x.shape[0]
    prev = jnp.where(pl.program_id(1) > 0, prev_ref[0], 0.0)
    cw = cw_ref[...]
    row8 = lax.broadcasted_iota(jnp.int32, prev.shape, 0)
    acc = x * cw[GDN_CONV - 1:GDN_CONV, :]
    for d in range(1, GDN_CONV):
        xr = pltpu.roll(x, d, axis=0)
        head = jnp.where(row8 < d, pltpu.roll(prev, d, axis=0), xr[0:8])
        xs = jnp.concatenate([head, xr[8:]], axis=0)
        acc = acc + xs * cw[GDN_CONV - 1 - d:GDN_CONV - d, :]
    y = acc * jax.nn.sigmoid(acc)
    nqk = heads * GDN_DK

    def l2n(z):
        return z * lax.rsqrt(jnp.sum(z * z, axis=-1, keepdims=True) + 1e-6)

    for h in range(heads):
        q_ref[0, :, GDN_DK * h:GDN_DK * (h + 1)] = l2n(y[:, GDN_DK * h:GDN_DK * (h + 1)])
    kn = jnp.concatenate([l2n(y[:, nqk + GDN_DK * h:nqk + GDN_DK * (h + 1)]) for h in range(heads)], axis=1)
    k_ref[0] = kn
    v_ref[0] = y[:, 2 * nqk:]
    knt = kn.T
    for n in range(ts // chunk):
        kt_ref[0, n] = knt[:, chunk * n:chunk * (n + 1)]
    m = misc_ref[0]
    lane = lax.broadcasted_iota(jnp.int32, m.shape, 1)
    g = -jnp.exp(alog_ref[...]) * jax.nn.softplus(m + dtb_ref[...])
    beta = jax.nn.sigmoid(m)
    gb = jnp.where((lane >= G_LANE) & (lane < G_LANE + heads), g,
                   jnp.where((lane >= B_LANE) & (lane < B_LANE + heads), beta, 0.0))
    gb_ref[0] = gb
    gbt = gb.T[G_LANE:G_LANE + 8, :]
    for n in range(ts // chunk):
        gbt_ref[0, n] = gbt[:, chunk * n:chunk * (n + 1)]
        g_rows = jnp.concatenate([gbt[h:h + 1, chunk * n:chunk * (n + 1)] for h in range(heads)], axis=1)
        grow_ref[0, n] = jnp.broadcast_to(g_rows, (8, heads * chunk))


def _gdn_pre(qkv, misc, w_conv, a_log, dt_bias, batch, seq, ts=256):
    width = qkv.shape[-1]
    heads = a_log.shape[0]
    chunk = GDN_CHUNK
    hd = heads * GDN_DK
    x3 = qkv.reshape(batch, seq, width)
    m3 = misc.reshape(batch, seq, LANES)
    alog_row = jnp.zeros((1, LANES), F32).at[0, G_LANE:G_LANE + heads].set(a_log.astype(F32))
    dtb_row = jnp.zeros((1, LANES), F32).at[0, G_LANE:G_LANE + heads].set(dt_bias.astype(F32))
    nck = ts // chunk
    tok = lambda b, s: (b, s, 0)
    return pl.pallas_call(
        functools.partial(_gdn_pre_kernel, heads=heads, chunk=chunk),
        grid=(batch, seq // ts),
        in_specs=[pl.BlockSpec((1, ts, width), tok),
                  pl.BlockSpec((1, 8, width), lambda b, s: (b, jnp.maximum(s * (ts // 8) - 1, 0), 0)),
                  pl.BlockSpec((1, ts, LANES), tok),
                  _const_spec(w_conv.shape), _const_spec((1, LANES)), _const_spec((1, LANES))],
        out_specs=(pl.BlockSpec((1, ts, hd), tok), pl.BlockSpec((1, ts, hd), tok), pl.BlockSpec((1, ts, width - 2 * hd), tok),
                   pl.BlockSpec((1, nck, hd, chunk), lambda b, s: (b, s, 0, 0)),
                   pl.BlockSpec((1, ts, LANES), tok),
                   pl.BlockSpec((1, nck, 8, chunk), lambda b, s: (b, s, 0, 0)),
                   pl.BlockSpec((1, nck, 8, heads * chunk), lambda b, s: (b, s, 0, 0))),
        out_shape=(jax.ShapeDtypeStruct((batch, seq, hd), F32), jax.ShapeDtypeStruct((batch, seq, hd), F32),
                   jax.ShapeDtypeStruct((batch, seq, width - 2 * hd), F32),
                   jax.ShapeDtypeStruct((batch, seq // chunk, hd, chunk), F32),
                   jax.ShapeDtypeStruct((batch, seq, LANES), F32),
                   jax.ShapeDtypeStruct((batch, seq // chunk, 8, chunk), F32),
                   jax.ShapeDtypeStruct((batch, seq // chunk, 8, heads * chunk), F32)),
        compiler_params=_cparams(("parallel", "parallel"), VMEM_LIMIT),
        name="gdn_pre",
    )(x3, x3, m3, w_conv.astype(F32), alog_row, dtb_row)


def _hdot(a, b):
    return jnp.dot(a, b, preferred_element_type=F32, precision=HIGHEST)


def _gdn_local_kernel(q_ref, k_ref, v_ref, kt_ref, gb_ref, gbt_ref, grow_ref,
                      u_ref, w_ref, qd_ref, a_ref, kend_ref, egl_ref, *, heads, chunk, n_chunks):
    c = chunk
    hc = heads * c
    hd = heads * GDN_DK
    iota = lambda shape, ax: lax.broadcasted_iota(jnp.int32, shape, ax)
    ri, li = iota((c, hc), 0), iota((c, hc), 1)
    lj, lh = li % c, li // c
    tri_cat = ri >= lj
    strict_cat = ri > lj
    eye_cat = (ri == lj).astype(F32)
    r2, l2 = iota((hc, hc), 0), iota((hc, hc), 1)
    same_blk = (r2 // c) == (l2 // c)
    tri_bd = jnp.logical_and(same_blk, (r2 % c) <= (l2 % c)).astype(F32)
    head_rows = (iota((hc, hd), 0) // c) == (iota((hc, hd), 1) // GDN_DK)
    r1, c1 = iota((c, c), 0), iota((c, c), 1)
    tri_f = (r1 >= c1).astype(F32)
    tri_t = (r1 <= c1).astype(F32)
    nt = (((1,), (1,)), ((), ()))

    def bdiag(x):
        return jnp.where(same_blk, jnp.concatenate([x] * heads, axis=0), 0.0)

    def bdiag_wide(x):
        return jnp.where(head_rows, jnp.concatenate([x] * heads, axis=0), 0.0)

    def per_head_cols(cols, width):
        return jnp.concatenate([jnp.broadcast_to(col, (c, width)) for col in cols], axis=1)

    st = []
    for n in range(n_chunks):
        r0 = n * c
        gbc = gb_ref[0, r0:r0 + c, :]
        gcc = _hdot(tri_f, gbc)
        gc_cols = [gcc[:, G_LANE + h:G_LANE + h + 1] for h in range(heads)]
        gc_c = jnp.broadcast_to(gc_cols[0], (c, hc))
        for h in range(1, heads):
            gc_c = jnp.where(lh == h, jnp.broadcast_to(gc_cols[h], (c, hc)), gc_c)
        gc_r = _hdot(grow_ref[0, n], tri_bd)[0:1, :]
        decay = jnp.where(tri_cat, jnp.exp(jnp.where(tri_cat, gc_c - gc_r, 0.0)), 0.0)
        beta_w = per_head_cols([gbc[:, B_LANE + h:B_LANE + h + 1] for h in range(heads)], GDN_DK)
        eg_w = per_head_cols([jnp.exp(col) for col in gc_cols], GDN_DK)
        q = q_ref[0, r0:r0 + c, :] * (GDN_DK ** -0.5)
        k = k_ref[0, r0:r0 + c, :]
        v = v_ref[0, r0:r0 + c, :]
        kb = k * beta_w
        k_bd = bdiag_wide(k).astype(BF16)
        kk = lax.dot_general(kb.astype(BF16), k_bd, nt, preferred_element_type=F32)
        qk = lax.dot_general(q.astype(BF16), k_bd, nt, preferred_element_type=F32)
        lmat = jnp.where(strict_cat, kk * decay, 0.0)
        qd_ref[0, r0:r0 + c, :] = (q * eg_w).astype(qd_ref.dtype)
        a_ref[0, r0:r0 + c, :] = jnp.where(tri_cat, qk * decay, 0.0).astype(a_ref.dtype)
        gcr = _hdot(gbt_ref[0, n], tri_t)
        g_last = [gcr[h:h + 1, c - 1:c] for h in range(heads)]
        f = jnp.concatenate([jnp.broadcast_to(jnp.exp(g_last[h] - gcr[h:h + 1, :]), (GDN_DK, c)) for h in range(heads)], axis=0)
        kend_ref[0, n] = (kt_ref[0, n] * f).astype(kend_ref.dtype)
        egl_ref[0, n] = jnp.concatenate([jnp.broadcast_to(jnp.exp(g_last[h]), (1, LANES)) for h in range(heads)]
                                        + [jnp.zeros((8 - heads, LANES), F32)], axis=0)
        st.append(dict(p=eye_cat - lmat, sq=lmat, vb=v * beta_w, kbe=kb * eg_w))
    kpow = 2
    while kpow < c:
        for d in st:
            d["sq"] = _dot(d["sq"].astype(BF16), bdiag(d["sq"]).astype(BF16))
        for d in st:
            d["p"] = d["p"] + _dot(d["p"].astype(BF16), bdiag(d["sq"]).astype(BF16))
        kpow *= 2
    for n, d in enumerate(st):
        r0 = n * c
        tmat = d["p"].astype(BF16)
        u_ref[0, r0:r0 + c, :] = _dot(tmat, bdiag_wide(d["vb"]).astype(BF16))
        w_ref[0, r0:r0 + c, :] = _dot(tmat, bdiag_wide(d["kbe"]).astype(BF16)).astype(w_ref.dtype)


def _gdn_local(q, k, v, kt, gb, gbt, grow, cb=4):
    batch, seq, hd = q.shape
    heads = hd // GDN_DK
    chunk = GDN_CHUNK
    n_all = seq // chunk
    ts = cb * chunk
    tok = lambda b, s: (b, s, 0)
    ck = lambda b, s: (b, s, 0, 0)
    return pl.pallas_call(
        functools.partial(_gdn_local_kernel, heads=heads, chunk=chunk, n_chunks=cb),
        grid=(batch, n_all // cb),
        in_specs=[pl.BlockSpec((1, ts, hd), tok), pl.BlockSpec((1, ts, hd), tok), pl.BlockSpec((1, ts, hd), tok),
                  pl.BlockSpec((1, cb, hd, chunk), ck), pl.BlockSpec((1, ts, LANES), tok), pl.BlockSpec((1, cb, 8, chunk), ck),
                  pl.BlockSpec((1, cb, 8, heads * chunk), ck)],
        out_specs=(pl.BlockSpec((1, ts, hd), tok), pl.BlockSpec((1, ts, hd), tok), pl.BlockSpec((1, ts, hd), tok),
                   pl.BlockSpec((1, ts, heads * chunk), tok), pl.BlockSpec((1, cb, hd, chunk), ck),
                   pl.BlockSpec((1, cb, 8, LANES), ck)),
        out_shape=(jax.ShapeDtypeStruct((batch, seq, hd), F32), jax.ShapeDtypeStruct((batch, seq, hd), BF16),
                   jax.ShapeDtypeStruct((batch, seq, hd), BF16), jax.ShapeDtypeStruct((batch, seq, heads * chunk), BF16),
                   jax.ShapeDtypeStruct((batch, n_all, hd, chunk), BF16), jax.ShapeDtypeStruct((batch, n_all, 8, LANES), F32)),
        compiler_params=_cparams(("parallel", "parallel"), VMEM_LIMIT),
        name="gdn_local",
    )(q, k, v, kt, gb, gbt, grow)


def _gdn_scan_kernel(u_ref, w_ref, qd_ref, a_ref, kend_ref, egl_ref, gz_ref, gn_ref, o_ref, st_ref, *, heads, chunk, n_chunks):
    c = chunk

    @pl.when(pl.program_id(1) == 0)
    def _():
        st_ref[...] = jnp.zeros(st_ref.shape, F32)

    gn = gn_ref[...]
    state = [st_ref[h] for h in range(heads)]
    col = lambda h: slice(GDN_DV * h, GDN_DV * (h + 1))
    for n in range(n_chunks):
        r0 = n * c
        egl = egl_ref[0, n]
        sb = [st.astype(BF16) for st in state]
        v_new = [u_ref[0, r0:r0 + c, col(h)] - _dot(w_ref[0, r0:r0 + c, col(h)], sb[h]) for h in range(heads)]
        vb = [x.astype(BF16) for x in v_new]
        state = [state[h] * egl[h:h + 1, :] + _dot(kend_ref[0, n, col(h), :], vb[h]) for h in range(heads)]
        for h in range(heads):
            o = _dot(qd_ref[0, r0:r0 + c, col(h)], sb[h]) + _dot(a_ref[0, r0:r0 + c, c * h:c * (h + 1)], vb[h])
            z = gz_ref[0, r0:r0 + c, col(h)]
            o_ref[0, r0:r0 + c, col(h)] = (_rms_rows(o, gn) * (z * jax.nn.sigmoid(z))).astype(o_ref.dtype)
    for h in range(heads):
        st_ref[h] = state[h]


def _gdn_scan(u, w, qd, a, kend, egl, gz, g_out, cs=4):
    batch, seq, hd = u.shape
    heads = hd // GDN_DV
    chunk = GDN_CHUNK
    n_all = seq // chunk
    ts = cs * chunk
    tok = lambda b, s: (b, s, 0)
    ck = lambda b, s: (b, s, 0, 0)
    return pl.pallas_call(
        functools.partial(_gdn_scan_kernel, heads=heads, chunk=chunk, n_chunks=cs),
        grid=(batch, n_all // cs),
        in_specs=[pl.BlockSpec((1, ts, hd), tok), pl.BlockSpec((1, ts, hd), tok), pl.BlockSpec((1, ts, hd), tok),
                  pl.BlockSpec((1, ts, heads * chunk), tok), pl.BlockSpec((1, cs, hd, chunk), ck),
                  pl.BlockSpec((1, cs, 8, LANES), ck), pl.BlockSpec((1, ts, hd), tok), _const_spec((1, GDN_DV))],
        out_specs=pl.BlockSpec((1, ts, hd), tok),
        out_shape=jax.ShapeDtypeStruct((batch, seq, hd), BF16),
        scratch_shapes=[pltpu.VMEM((heads, GDN_DK, GDN_DV), F32)],
        compiler_params=_cparams(("parallel", "arbitrary"), VMEM_LIMIT),
        name="gdn_scan",
    )(u, w, qd, a, kend, egl, gz.reshape(batch, seq, hd), g_out.reshape(1, GDN_DV).astype(F32))


def _mix_out_kernel(s5_ref, mla_ref, gdn_ref, x_ref, w_ref, mg_ref, g_ref, b_ref, o_ref, *, alpha, parts):
    w5 = s5_ref.shape[1]
    wm = mla_ref.shape[1]
    pm = x_ref.shape[0] // parts
    rows = lambda r: slice(r * pm, (r + 1) * pm)
    accs = []
    for r in range(parts):
        mla = _rms_rows(mla_ref[rows(r), :], mg_ref[...]).astype(BF16)
        accs.append(_dot(s5_ref[rows(r), :], w_ref[0:w5, :]) + _dot(mla, w_ref[w5:w5 + wm, :])
                    + _dot(gdn_ref[rows(r), :], w_ref[w5 + wm:, :]))
    for r in range(parts):
        o_ref[rows(r), :] = _layernorm_rows(alpha * x_ref[rows(r), :] + accs[r], g_ref[...], b_ref[...])


def _mix_out(y_s5, o_mla, y_gdn, x2d, w_out, mla_gain, ln_g, ln_b, alpha, tm=512, parts=2):
    t, d = x2d.shape
    row = lambda i: (i, 0)
    return pl.pallas_call(
        functools.partial(_mix_out_kernel, alpha=alpha, parts=parts),
        grid=(t // tm,),
        in_specs=[pl.BlockSpec((tm, y_s5.shape[1]), row), pl.BlockSpec((tm, o_mla.shape[1]), row),
                  pl.BlockSpec((tm, y_gdn.shape[1]), row), pl.BlockSpec((tm, d), row),
                  _const_spec(w_out.shape), _const_spec((1, o_mla.shape[1])), _const_spec((1, d)), _const_spec((1, d))],
        out_specs=pl.BlockSpec((tm, d), row),
        out_shape=jax.ShapeDtypeStruct((t, d), F32),
        compiler_params=_cparams(("parallel",), VMEM_LIMIT),
        name="mix_out",
    )(y_s5, o_mla, y_gdn, x2d, w_out.astype(BF16), mla_gain.reshape(1, -1).astype(F32),
      ln_g.reshape(1, d).astype(F32), ln_b.reshape(1, d).astype(F32))


def _matmul_kernel(x_ref, w_ref, o_ref):
    o_ref[...] = _dot(x_ref[...].astype(BF16), w_ref[...]).astype(o_ref.dtype)


def _matmul(x, w, tm=256, tn=512):
    m, k = x.shape
    n = w.shape[1]
    return pl.pallas_call(
        _matmul_kernel,
        grid=(m // tm, n // tn),
        in_specs=[pl.BlockSpec((tm, k), lambda i, j: (i, 0)), pl.BlockSpec((k, tn), lambda i, j: (0, j))],
        out_specs=pl.BlockSpec((tm, tn), lambda i, j: (i, j)),
        out_shape=jax.ShapeDtypeStruct((m, n), F32),
        compiler_params=_cparams(("parallel", "parallel")),
        name="mem_kv_proj",
    )(x, w.astype(BF16))


def _xattn_kernel(x_ref, wq_ref, kt_ref, v_ref, wo_ref, g_ref, b_ref, wr_ref, o_ref, lg_ref, *, heads, alpha, parts):
    tm = x_ref.shape[0]
    pm = tm // parts
    nr = lg_ref.shape[0]
    nt = (((1,), (1,)), ((), ()))
    cols = lambda h: slice(XA_DH * h, XA_DH * (h + 1))
    xs = [x_ref[r * pm:(r + 1) * pm, :] for r in range(parts)]
    qs = [(_dot(x.astype(BF16), wq_ref[...]) * (XA_DH ** -0.5)).astype(BF16) for x in xs]
    ss = [[_dot(q[:, cols(h)], kt_ref[0, cols(h), :]) for h in range(heads)] for q in qs]
    ps = []
    for part in ss:
        row = []
        for s in part:
            e = jnp.exp(s - jnp.max(s, axis=-1, keepdims=True))
            row.append((e / jnp.sum(e, axis=-1, keepdims=True)).astype(BF16))
        ps.append(row)
    os = [jnp.concatenate([_dot(p[h], v_ref[0, :, cols(h)]) for h in range(heads)], axis=1).astype(BF16) for p in ps]
    ys = [_layernorm_rows(alpha * xs[r] + _dot(os[r], wo_ref[...]), g_ref[...], b_ref[...]) for r in range(parts)]
    w_hi_lo = wr_ref[...]
    for r in range(parts):
        o_ref[r * pm:(r + 1) * pm, :] = ys[r]
        y_hi = ys[r].astype(BF16)
        y_lo = (ys[r] - y_hi.astype(F32)).astype(BF16)
        p1 = lax.dot_general(w_hi_lo, y_hi, nt, preferred_element_type=F32)
        p2 = lax.dot_general(w_hi_lo[:nr], y_lo, nt, preferred_element_type=F32)
        lg_ref[:, r * pm:(r + 1) * pm] = p1[:nr] + p1[nr:] + p2


def _xattn(x2d, kt, v, w_q, w_o, ln_g, ln_b, w_router_t, seq, alpha, tm=512, parts=2):
    t, d = x2d.shape
    width = w_q.shape[1]
    heads = width // XA_DH
    mlen = v.shape[1]
    per = seq // tm
    nr = w_router_t.shape[0]
    w_hi = w_router_t.astype(BF16)
    w_lo = (w_router_t - w_hi.astype(F32)).astype(BF16)
    row = lambda i: (i, 0)
    return pl.pallas_call(
        functools.partial(_xattn_kernel, heads=heads, alpha=alpha, parts=parts),
        grid=(t // tm,),
        in_specs=[pl.BlockSpec((tm, d), row), _const_spec((d, width)),
                  pl.BlockSpec((1, width, mlen), lambda i: (i // per, 0, 0)),
                  pl.BlockSpec((1, mlen, width), lambda i: (i // per, 0, 0)),
                  _const_spec((width, d)), _const_spec((1, d)), _const_spec((1, d)), _const_spec((2 * nr, d))],
        out_specs=(pl.BlockSpec((tm, d), row), pl.BlockSpec((nr, tm), lambda i: (0, i))),
        out_shape=(jax.ShapeDtypeStruct((t, d), F32), jax.ShapeDtypeStruct((nr, t), F32)),
        compiler_params=_cparams(("parallel",), VMEM_LIMIT),
        name="xattn",
    )(x2d, w_q.astype(BF16), kt, v, w_o.astype(BF16), ln_g.reshape(1, d).astype(F32), ln_b.reshape(1, d).astype(F32),
      jnp.concatenate([w_hi, w_lo], axis=0))


def _router_kernel(lg_ref, bias_ref, eid_ref, gate_ref):
    lg = lg_ref[...] + bias_ref[...]
    ng, ne = MOE_GROUPS, MOE_PER_GROUP
    grp = [lg[g:g + 1, :] for g in range(ng)]
    gmax = functools.reduce(jnp.maximum, grp)
    gexp = [jnp.exp(r - gmax) for r in grp]
    gsum = functools.reduce(lambda a, b: a + b, gexp)
    pg = [e / gsum for e in gexp]
    best, gsel = pg[0], jnp.zeros(pg[0].shape, jnp.int32)
    for g in range(1, ng):
        better = pg[g] > best
        gsel = jnp.where(better, g, gsel)
        best = jnp.where(better, pg[g], best)
    le = []
    for e in range(ne):
        r = lg[ng + e:ng + e + 1, :]
        for g in range(1, ng):
            r = jnp.where(gsel == g, lg[ng + g * ne + e:ng + g * ne + e + 1, :], r)
        le.append(r)
    emax = functools.reduce(jnp.maximum, le)
    eexp = [jnp.exp(r - emax) for r in le]
    esum = functools.reduce(lambda a, b: a + b, eexp)
    pe = [e / esum for e in eexp]
    sel, val = [], []
    for k in range(MOE_TOPK):
        bv, bi = None, None
        for e in range(ne):
            cand = pe[e]
            for prev in sel:
                cand = jnp.where(prev == e, -1.0, cand)
            if bv is None:
                bv, bi = cand, jnp.zeros(cand.shape, jnp.int32)
            else:
                better = cand > bv
                bi = jnp.where(better, e, bi)
                bv = jnp.where(better, cand, bv)
        sel.append(bi)
        val.append(bv)
    tot = functools.reduce(lambda a, b: a + b, val)
    zero_i = jnp.zeros((8 - MOE_TOPK,) + sel[0].shape[1:], jnp.int32)
    zero_f = jnp.zeros((8 - MOE_TOPK,) + sel[0].shape[1:], F32)
    eid_ref[...] = jnp.concatenate([gsel * ne + s for s in sel] + [zero_i], axis=0)
    gate_ref[...] = jnp.concatenate([best * v / tot for v in val] + [zero_f], axis=0)


def _router(lg_t, bias_col, tn=2048):
    nr, t = lg_t.shape
    tn = min(tn, t)
    return pl.pallas_call(
        _router_kernel,
        grid=(t // tn,),
        in_specs=[pl.BlockSpec((nr, tn), lambda i: (0, i)), _const_spec((nr, 1))],
        out_specs=(pl.BlockSpec((8, tn), lambda i: (0, i)), pl.BlockSpec((8, tn), lambda i: (0, i))),
        out_shape=(jax.ShapeDtypeStruct((8, t), jnp.int32), jax.ShapeDtypeStruct((8, t), F32)),
        compiler_params=_cparams(("parallel",)),
        name="router",
    )(lg_t, bias_col)


def _start_row_gather(idx_ref, src_hbm, buf, sem):
    for r in range(buf.shape[0]):
        pltpu.make_async_copy(src_hbm.at[pl.ds(idx_ref[r], 1)], buf.at[pl.ds(r, 1)], sem).start()


def _wait_row_gather(buf, sem):
    pltpu.make_async_copy(buf, buf, sem).wait()


def _expert_kernel(be_ref, nu_ref, idx0_ref, idxn_ref, x_hbm, wgu_ref, wd_ref, y_ref, xbuf, sem, wgu_sc, wd_sc, *, ff):
    i = pl.program_id(0)
    n_used = nu_ref[0]
    slot = lax.rem(i, 2)
    changed = jnp.logical_or(i == 0, be_ref[i] != be_ref[jnp.maximum(i - 1, 0)])

    @pl.when(changed)
    def _():
        wgu_sc[...] = wgu_ref[0, 0].astype(BF16)
        wd_sc[...] = wd_ref[0, 0].astype(BF16)

    @pl.when(i == 0)
    def _():
        _start_row_gather(idx0_ref, x_hbm, xbuf.at[0], sem.at[0])

    @pl.when(i < n_used)
    def _():
        _wait_row_gather(xbuf.at[slot], sem.at[slot])
        x = xbuf[slot].astype(BF16)
        _start_row_gather(idxn_ref, x_hbm, xbuf.at[1 - slot], sem.at[1 - slot])
        gu = _dot(x, wgu_sc[...])
        gate = gu[:, :ff]
        h = gate * jax.nn.sigmoid(gate) * gu[:, ff:]
        y_ref[...] = _dot(h.astype(BF16), wd_sc[...])

    @pl.when(i == n_used)
    def _():
        _wait_row_gather(xbuf.at[slot], sem.at[slot])

    @pl.when(i >= n_used)
    def _():
        y_ref[...] = jnp.zeros(y_ref.shape, y_ref.dtype)


def _experts(x2d, row_tok, blk_exp, n_used, w_gate_up, w_down, layer, rb):
    rows = row_tok.shape[0]
    d = x2d.shape[1]
    ff = w_down.shape[2]
    nblk = rows // rb
    grid_spec = pltpu.PrefetchScalarGridSpec(
        num_scalar_prefetch=2,
        grid=(nblk,),
        in_specs=[pl.BlockSpec((rb,), lambda i, be, nu: (0,), memory_space=pltpu.SMEM),
                  pl.BlockSpec((rb,), lambda i, be, nu: (jnp.minimum(i + 1, nblk - 1),), memory_space=pltpu.SMEM),
                  pl.BlockSpec(memory_space=pl.ANY),
                  pl.BlockSpec((1, 1, d, 2 * ff), lambda i, be, nu: (layer, be[i], 0, 0)),
                  pl.BlockSpec((1, 1, ff, d), lambda i, be, nu: (layer, be[i], 0, 0))],
        out_specs=pl.BlockSpec((rb, d), lambda i, be, nu: (i, 0)),
        scratch_shapes=[pltpu.VMEM((2, rb, d), x2d.dtype), pltpu.SemaphoreType.DMA((2,)),
                        pltpu.VMEM((d, 2 * ff), BF16), pltpu.VMEM((ff, d), BF16)],
    )
    return pl.pallas_call(
        functools.partial(_expert_kernel, ff=ff),
        grid_spec=grid_spec,
        out_shape=jax.ShapeDtypeStruct((rows, d), F32),
        compiler_params=_cparams(("arbitrary",), VMEM_LIMIT),
        name="experts",
    )(blk_exp, n_used, row_tok, row_tok, x2d, w_gate_up, w_down)


def _moe_plan(eid, n_exp, rb):
    t, topk = eid.shape
    m = t * topk
    flat_e = eid.reshape(m)
    onehot = (flat_e[:, None] == jnp.arange(n_exp, dtype=jnp.int32)[None, :]).astype(jnp.int32)
    csum = jnp.cumsum(onehot, axis=0)
    counts = csum[-1]
    pcounts = (counts + rb - 1) // rb * rb
    pends = jnp.cumsum(pcounts)
    pstarts = pends - pcounts
    dest = jnp.sum(onehot * (pstarts[None, :] + csum - 1), axis=1)
    rows = m + n_exp * rb
    row_tok = (jnp.arange(rows, dtype=jnp.int32) % t).at[dest].set(jnp.arange(m, dtype=jnp.int32) // topk)
    nblk = rows // rb
    blk_start = jnp.arange(nblk, dtype=jnp.int32) * rb
    blk_exp = jnp.minimum(jnp.sum((pends[None, :] <= blk_start[:, None]).astype(jnp.int32), axis=1), n_exp - 1)
    n_used = (pends[-1] // rb).astype(jnp.int32).reshape(1)
    return dest.astype(jnp.int32), row_tok, blk_exp, n_used


def _moe_out_kernel(*refs, alpha):
    idx0 = refs[:MOE_TOPK]
    idxn = refs[MOE_TOPK:2 * MOE_TOPK]
    y_hbm, x_ref, gate_ref, g_ref, b_ref, o_ref, ybuf, sem = refs[2 * MOE_TOPK:]
    i = pl.program_id(0)
    slot = lax.rem(i, 2)

    @pl.when(i == 0)
    def _():
        for k in range(MOE_TOPK):
            _start_row_gather(idx0[k], y_hbm, ybuf.at[0, k], sem.at[0])

    @pl.when(i + 1 < pl.num_programs(0))
    def _():
        for k in range(MOE_TOPK):
            _start_row_gather(idxn[k], y_hbm, ybuf.at[1 - slot, k], sem.at[1 - slot])

    _wait_row_gather(ybuf.at[slot], sem.at[slot])
    gate = gate_ref[...]
    ffn = gate[:, 0:1] * ybuf[slot, 0]
    for k in range(1, MOE_TOPK):
        ffn = ffn + gate[:, k:k + 1] * ybuf[slot, k]
    o_ref[...] = _layernorm_rows(alpha * x_ref[...] + ffn, g_ref[...], b_ref[...])


def _moe_out(yr, dest_k, x2d, gate, ln_g, ln_b, alpha, tm=256):
    t, d = x2d.shape
    row = lambda i: (i, 0)
    nt = t // tm
    first = [pl.BlockSpec((tm,), functools.partial(lambda k, i: (k * nt,), k), memory_space=pltpu.SMEM) for k in range(MOE_TOPK)]
    nxt = [pl.BlockSpec((tm,), functools.partial(lambda k, i: (k * nt + jnp.minimum(i + 1, nt - 1),), k), memory_space=pltpu.SMEM)
           for k in range(MOE_TOPK)]
    return pl.pallas_call(
        functools.partial(_moe_out_kernel, alpha=alpha),
        grid=(nt,),
        in_specs=first + nxt + [pl.BlockSpec(memory_space=pl.ANY), pl.BlockSpec((tm, d), row),
                                pl.BlockSpec((tm, gate.shape[1]), row), _const_spec((1, d)), _const_spec((1, d))],
        out_specs=pl.BlockSpec((tm, d), row),
        out_shape=jax.ShapeDtypeStruct((t, d), F32),
        scratch_shapes=[pltpu.VMEM((2, MOE_TOPK, tm, d), yr.dtype), pltpu.SemaphoreType.DMA((2,))],
        compiler_params=_cparams(("arbitrary",), VMEM_LIMIT),
        name="moe_out",
    )(*([dest_k] * (2 * MOE_TOPK)), yr, x2d, gate, ln_g.reshape(1, d).astype(F32), ln_b.reshape(1, d).astype(F32))


def _moe(x2, lg_t, b_group, b_expert, w_gate_up, w_down, layer, ln_g, ln_b, alpha):
    t, d = x2.shape
    n_exp = w_gate_up.shape[1]
    nr = lg_t.shape[0]
    bias = jnp.zeros((nr, 1), F32).at[:MOE_GROUPS + n_exp, 0].set(jnp.concatenate([b_group, b_expert]).astype(F32))
    eid_t, gate_t = _router(lg_t, bias)
    eid = eid_t[:MOE_TOPK].T
    gate = gate_t[:MOE_TOPK].T
    dest, row_tok, blk_exp, n_used = _moe_plan(eid, n_exp, MOE_ROW_BLOCK)
    yr = _experts(x2, row_tok, blk_exp, n_used, w_gate_up, w_down, layer, MOE_ROW_BLOCK)
    return _moe_out(yr, dest.reshape(t, MOE_TOPK).T.reshape(-1), x2, gate, ln_g, ln_b, alpha)


def kernel(x, mem, positions, w_in, s5_lambda_re, s5_lambda_im, s5_log_step, s5_b_re, s5_b_im, s5_c_re, s5_c_im, s5_d, s5_w_glu, s5_b_glu, s5_out_norm, mla_q_norm, mla_w_uq, mla_kv_norm, mla_w_ukv, mla_out_norm, gdn_conv, gdn_a_log, gdn_dt_bias, gdn_out_norm, w_out, ln1_g, ln1_b, xa_w_q, xa_w_k, xa_w_v, xa_w_o, ln2_g, ln2_b, moe_w_group, moe_b_group, moe_w_expert, moe_b_expert, moe_w_gate_up, moe_w_down, ln3_g, ln3_b):
    batch, seq, d = x.shape
    t = batch * seq
    depth = w_in.shape[0]
    alpha = (2 * depth) ** 0.25
    mlen = mem.shape[1]
    s5_w = s5_w_glu.shape[1]
    rank_q = mla_w_uq.shape[1]
    rank_kv = mla_w_ukv.shape[1]
    g_heads = gdn_a_log.shape[1]
    g_qk = g_heads * GDN_DK
    g_v = gdn_conv.shape[2] - 2 * g_qk
    o_kr = s5_w + rank_q + rank_kv
    o_gq = o_kr + MLA_ROPE
    o_gz = o_gq + 2 * g_qk + g_v
    o_ga = o_gz + g_v
    widths = (s5_w, rank_q, rank_kv, LANES, 2 * g_qk + g_v, g_v)
    assert G_LANE == MLA_ROPE and B_LANE == G_LANE + g_heads and o_ga + 2 * g_heads == w_in.shape[2]

    cos_t, sin_t = _rope_tables(positions)
    mem2 = mem.reshape(batch * mlen, d)
    xt = x.reshape(t, d)
    for l in range(depth):
        w = w_in[l]
        w_packed = jnp.concatenate(
            [w[:, :o_gq], w[:, o_ga:], jnp.zeros((d, LANES - MLA_ROPE - 2 * g_heads), w.dtype), w[:, o_gq:o_ga]], axis=1).astype(BF16)
        u, cq, ckv, misc, qkv, gz = _in_proj(xt, w_packed, widths)

        tables = _s5_tables(s5_lambda_re[l], s5_lambda_im[l], s5_log_step[l], s5_b_re[l], s5_b_im[l], s5_c_re[l],
                            s5_c_im[l], s5_d[l], S5_CHUNK, seq // S5_CHUNK)
        y_s5 = _s5_glu(_s5_scan(u, tables, batch, seq, S5_CHUNK), s5_w_glu[l], s5_b_glu[l], s5_out_norm[l])

        qt, kk, vt = _mla_proj(cq, ckv, misc, cos_t, sin_t, mla_q_norm[l], mla_kv_norm[l], mla_w_uq[l], mla_w_ukv[l], batch, seq)
        o_mla = _mla_attention(qt, kk, vt).reshape(t, -1)

        gq, gk, gv, gkt, ggb, ggbt, ggrow = _gdn_pre(qkv, misc, gdn_conv[l], gdn_a_log[l], gdn_dt_bias[l], batch, seq)
        local = _gdn_local(gq, gk, gv, gkt, ggb, ggbt, ggrow)
        y_gdn = _gdn_scan(*local, gz, gdn_out_norm[l]).reshape(t, -1)

        x1 = _mix_out(y_s5, o_mla, y_gdn, xt, w_out[l], mla_out_norm[l], ln1_g[l], ln1_b[l], alpha)

        kv_mem = _matmul(mem2, jnp.concatenate([xa_w_k[l], xa_w_v[l]], axis=1))
        xa_w = xa_w_k.shape[2]
        kt_mem = kv_mem[:, :xa_w].reshape(batch, mlen, xa_w).transpose(0, 2, 1).astype(BF16)
        v_mem = kv_mem[:, xa_w:].reshape(batch, mlen, xa_w).astype(BF16)
        n_route = MOE_GROUPS + moe_w_expert.shape[2]
        w_router_t = jnp.pad(jnp.concatenate([moe_w_group[l], moe_w_expert[l]], axis=1).T.astype(F32),
                             ((0, (-n_route) % 8), (0, 0)))
        x2, lg_t = _xattn(x1, kt_mem, v_mem, xa_w_q[l], xa_w_o[l], ln2_g[l], ln2_b[l], w_router_t, seq, alpha)

        xt = _moe(x2, lg_t, moe_b_group[l], moe_b_expert[l], moe_w_gate_up, moe_w_down, l, ln3_g[l], ln3_b[l], alpha)
    return xt.reshape(batch, seq, d)
```

```python
import functools
import math

import jax
import jax.numpy as jnp
from jax import lax
from jax.experimental import pallas as pl
from jax.experimental.pallas import tpu as pltpu

F32 = jnp.float32
BF16 = jnp.bfloat16
HIGHEST = lax.Precision.HIGHEST

S5_CH = 16
S5_STATE = 64
S5_LAMBDA_RE_MAX = -1e-4
S5_CHUNK = 16
MLA_NOPE = 128
MLA_ROPE = 64
MLA_V = 128
ROPE_THETA = 10000.0
GDN_DK = 128
GDN_DV = 128
GDN_CONV = 4
GDN_CHUNK = 64
XA_DH = 128
MOE_GROUPS = 4
MOE_PER_GROUP = 8
MOE_TOPK = 2
MOE_ROW_BLOCK = 256

LANES = 128
VMEM_LIMIT = 56 * 1024 * 1024


def _cparams(sem, vmem=None, flags=None):
    return pltpu.CompilerParams(dimension_semantics=sem, vmem_limit_bytes=vmem, flags=flags)


def _const_spec(shape):
    nd = len(shape)
    return pl.BlockSpec(shape, lambda *_: (0,) * nd)


def _rms_rows(x, gain, eps=1e-6):
    return x * lax.rsqrt(jnp.mean(x * x, axis=-1, keepdims=True) + eps) * gain


def _layernorm_rows(x, g, b, eps=1e-5):
    mu = jnp.mean(x, axis=-1, keepdims=True)
    xc = x - mu
    var = jnp.mean(xc * xc, axis=-1, keepdims=True)
    return xc * lax.rsqrt(var + eps) * g + b


def _dot(a, b):
    return jnp.dot(a, b, preferred_element_type=F32)


def _in_proj_kernel(x_ref, w_ref, u_ref, cq_ref, ckv_ref, misc_ref, qkv_ref, gz_ref, *, splits):
    xb = x_ref[...].astype(BF16)
    outs = (u_ref, cq_ref, ckv_ref, misc_ref, qkv_ref, gz_ref)
    for o_ref, (lo, hi) in zip(outs, splits):
        o_ref[...] = _dot(xb, w_ref[:, lo:hi]).astype(o_ref.dtype)


def _in_proj(x2d, w_packed, widths, dtypes, tm=256):
    t, d = x2d.shape
    splits, lo = [], 0
    for w in widths:
        splits.append((lo, lo + w))
        lo += w
    out_shape = tuple(jax.ShapeDtypeStruct((t, w), dt) for w, dt in zip(widths, dtypes))
    out_specs = tuple(pl.BlockSpec((tm, w), lambda i: (i, 0)) for w in widths)
    return pl.pallas_call(
        functools.partial(_in_proj_kernel, splits=tuple(splits)),
        grid=(t // tm,),
        in_specs=[pl.BlockSpec((tm, d), lambda i: (i, 0)), _const_spec(w_packed.shape)],
        out_specs=out_specs,
        out_shape=out_shape,
        compiler_params=_cparams(("parallel",), VMEM_LIMIT),
        name="in_proj",
    )(x2d, w_packed)


def _s5_tables(lam_re, lam_im, log_step, b_re, b_im, c_re, c_im, d_skip, chunk, n_chunks):
    g, p = lam_re.shape
    h = b_re.shape[-1]
    gs = LANES // h
    sets = g // gs
    eye = jnp.eye(gs, dtype=F32)
    lr = jnp.minimum(lam_re.astype(F32), S5_LAMBDA_RE_MAX)
    li = lam_im.astype(F32)
    dt = jnp.exp(log_step.astype(F32))[:, None]
    mag = jnp.exp(lr * dt)
    th = li * dt
    ab_re, ab_im = mag * jnp.cos(th), mag * jnp.sin(th)
    den = lr * lr + li * li
    nr, ni = ab_re - 1.0, ab_im
    fr = (nr * lr + ni * li) / den
    fi = (ni * lr - nr * li) / den
    br, bi = b_re.astype(F32), b_im.astype(F32)
    bb_re = fr[..., None] * br - fi[..., None] * bi
    bb_im = fr[..., None] * bi + fi[..., None] * br
    cr, ci = c_re.astype(F32), c_im.astype(F32)
    n = jnp.arange(chunk + 1, dtype=F32)[:, None, None]
    pmag = jnp.exp(n * (lr * dt)[None])
    pr, pi = pmag * jnp.cos(n * th[None]), pmag * jnp.sin(n * th[None])

    cb_re = cr.transpose(0, 2, 1)[:, :, :, None] * bb_re[:, :, None, :] - ci.transpose(0, 2, 1)[:, :, :, None] * bb_im[:, :, None, :]
    cb_im = cr.transpose(0, 2, 1)[:, :, :, None] * bb_im[:, :, None, :] + ci.transpose(0, 2, 1)[:, :, :, None] * bb_re[:, :, None, :]
    kk = (jnp.einsum('tgp,gphk->tghk', pr[:chunk], cb_re, precision=HIGHEST)
          - jnp.einsum('tgp,gphk->tghk', pi[:chunk], cb_im, precision=HIGHEST))
    kk = kk.at[0].add(jnp.eye(h, dtype=F32)[None] * d_skip.astype(F32)[:, :, None])
    blocks = jnp.einsum('kg,tsgoh->stkhgo', eye, kk.reshape(chunk, sets, gs, h, h)).reshape(sets, chunk, LANES, LANES)
    bpad = jnp.concatenate([jnp.zeros_like(blocks[:, :1]), blocks], axis=1)
    half = chunk // 2
    lag = lambda a, b: bpad[:, (b - a + 1)::2][:, :half]
    c_sup = jnp.concatenate([jnp.concatenate([lag(0, 0), lag(0, 1)], axis=-1),
                             jnp.concatenate([lag(1, 0), lag(1, 1)], axis=-1)], axis=-2)

    pr_rev, pi_rev = pr[chunk - 1::-1][:chunk], pi[chunk - 1::-1][:chunk]
    e_re = pr_rev[:, :, :, None] * bb_re[None] - pi_rev[:, :, :, None] * bb_im[None]
    e_im = pr_rev[:, :, :, None] * bb_im[None] + pi_rev[:, :, :, None] * bb_re[None]
    e_all = jnp.stack([e_re, e_im], axis=0).reshape(2, chunk, sets, gs, p, h)
    e_mat = jnp.einsum('kg,cjsgph->sjkhcgp', eye, e_all).reshape(sets, chunk * LANES, 2 * gs * p)

    pr1, pi1 = pr[1:], pi[1:]
    f_re = cr[None] * pr1[:, :, None, :] - ci[None] * pi1[:, :, None, :]
    f_im = cr[None] * pi1[:, :, None, :] + ci[None] * pr1[:, :, None, :]
    f_all = jnp.stack([f_re, -f_im], axis=0).reshape(2, chunk, sets, gs, h, p)
    f_mat = jnp.einsum('kg,cisgop->sckpigo', eye, f_all).reshape(sets, 2 * gs * p, chunk * LANES)

    steps = max(1, int(math.ceil(math.log2(n_chunks))))
    ar, ai = pr[chunk].reshape(sets, gs * p), pi[chunk].reshape(sets, gs * p)
    a1, a2 = [], []
    for _ in range(steps):
        a1.append(jnp.concatenate([ar, ar], axis=-1))
        a2.append(jnp.concatenate([-ai, ai], axis=-1))
        ar, ai = ar * ar - ai * ai, 2.0 * ar * ai
    pad = (-steps) % 8
    a1 = jnp.pad(jnp.stack(a1, axis=1), ((0, 0), (0, pad), (0, 0)))
    a2 = jnp.pad(jnp.stack(a2, axis=1), ((0, 0), (0, pad), (0, 0)))
    return c_sup.astype(BF16), e_mat.astype(BF16), f_mat.astype(BF16), a1, a2


def _s5_kernel(u_ref, c_ref, e_ref, f_ref, a1_ref, a2_ref, y_ref, *, n_chunks, steps):
    u = u_ref[0]
    s = _dot(u, e_ref[0])
    rows, width = s.shape
    c_idx = lax.broadcasted_iota(jnp.int32, (rows, width), 0) % n_chunks
    a1 = a1_ref[0]
    a2 = a2_ref[0]
    for k in range(steps):
        sh = 1 << k
        prev = jnp.where(c_idx >= sh, pltpu.roll(s, sh, axis=0), 0.0)
        s = s + a1[k:k + 1, :] * prev + a2[k:k + 1, :] * pltpu.roll(prev, width // 2, axis=1)
    s_in = jnp.where(c_idx >= 1, pltpu.roll(s, 1, axis=0), 0.0)
    y_state = _dot(s_in.astype(BF16), f_ref[0])
    sw = c_ref.shape[2]
    for i in range(u.shape[1] // sw):
        acc = y_state[:, i * sw:(i + 1) * sw]
        for sg in range(i + 1):
            acc = acc + _dot(u[:, (i - sg) * sw:(i - sg + 1) * sw], c_ref[0, sg])
        y_ref[0, :, i * sw:(i + 1) * sw] = acc


def _s5_scan(u, tables, batch, seq, chunk, batches_per_block=2):
    c_sup, e_mat, f_mat, a1, a2 = tables
    sets = c_sup.shape[0]
    n_chunks = seq // chunk
    steps = max(1, int(math.ceil(math.log2(n_chunks))))
    rows = batch * n_chunks
    rb = batches_per_block * n_chunks
    width = chunk * LANES
    us = u.reshape(rows, chunk, sets, LANES).transpose(2, 0, 1, 3).reshape(sets, rows, width).astype(BF16)
    const = lambda arr: pl.BlockSpec((1,) + arr.shape[1:], lambda i, j: (i,) + (0,) * (arr.ndim - 1))
    y = pl.pallas_call(
        functools.partial(_s5_kernel, n_chunks=n_chunks, steps=steps),
        grid=(sets, rows // rb),
        in_specs=[pl.BlockSpec((1, rb, width), lambda i, j: (i, j, 0)),
                  const(c_sup), const(e_mat), const(f_mat), const(a1), const(a2)],
        out_specs=pl.BlockSpec((1, rb, width), lambda i, j: (i, j, 0)),
        out_shape=jax.ShapeDtypeStruct((sets, rows, width), F32),
        compiler_params=_cparams(("parallel", "parallel"), VMEM_LIMIT),
        name="s5_scan",
    )(us, c_sup, e_mat, f_mat, a1, a2)
    return y.reshape(sets, rows, chunk, LANES).transpose(1, 2, 0, 3).reshape(batch * seq, sets * LANES)


def _s5_glu_kernel(y_ref, w_ref, b_ref, g_ref, o_ref):
    y = jax.nn.gelu(y_ref[...])
    z = _dot(y.astype(BF16), w_ref[...]) + b_ref[...]
    y = y * jax.nn.sigmoid(z)
    o_ref[...] = _rms_rows(y, g_ref[...]).astype(o_ref.dtype)


def _s5_glu(y, w_glu, b_glu, g_out, tm=1024):
    t, w = y.shape
    return pl.pallas_call(
        _s5_glu_kernel,
        grid=(t // tm,),
        in_specs=[pl.BlockSpec((tm, w), lambda i: (i, 0)), _const_spec((w, w)), _const_spec((1, w)), _const_spec((1, w))],
        out_specs=pl.BlockSpec((tm, w), lambda i: (i, 0)),
        out_shape=jax.ShapeDtypeStruct((t, w), BF16),
        compiler_params=_cparams(("parallel",)),
        name="s5_glu",
    )(y, w_glu.astype(BF16), b_glu.reshape(1, w).astype(F32), g_out.reshape(1, w).astype(F32))


def _mla_proj_kernel(cq_ref, ckv_ref, misc_ref, cos_ref, sin_ref, qn_ref, kvn_ref, wuq_ref, wukv_ref,
                     qt_ref, k_ref, vt_ref, *, heads, scale):
    cq = _rms_rows(cq_ref[...].astype(F32), qn_ref[...]).astype(BF16)
    q = _dot(cq, wuq_ref[...]) * scale
    ckv = _rms_rows(ckv_ref[...].astype(F32), kvn_ref[...]).astype(BF16)
    kv = _dot(ckv, wukv_ref[...])
    cos = cos_ref[...]
    sin = sin_ref[...]
    lane = lax.broadcasted_iota(jnp.int32, cos.shape, 1)
    first_half = (lane % MLA_ROPE) < (MLA_ROPE // 2)

    def rope(x):
        partner = jnp.where(first_half, pltpu.roll(x, LANES - MLA_ROPE // 2, axis=1), pltpu.roll(x, MLA_ROPE // 2, axis=1))
        return x * cos + partner * sin

    kpe = rope(misc_ref[...])
    kpe_lo = jnp.where(lane < MLA_ROPE, kpe, 0.0)
    kpe_hi = pltpu.roll(kpe_lo, MLA_ROPE, axis=1)
    nope_w = heads * MLA_NOPE
    for pair in range(heads // 2):
        q_pe = rope(q[:, nope_w + LANES * pair:nope_w + LANES * (pair + 1)])
        for h in (2 * pair, 2 * pair + 1):
            qh = jnp.concatenate([q[:, MLA_NOPE * h:MLA_NOPE * (h + 1)], q_pe], axis=1)
            qt_ref[0, h] = qh.T.astype(BF16)
            kvw = MLA_NOPE + MLA_V
            kh = jnp.concatenate([kv[:, kvw * h:kvw * h + MLA_NOPE], kpe_lo if h % 2 == 0 else kpe_hi], axis=1)
            k_ref[0, h] = kh.astype(BF16)
            vt_ref[0, h] = kv[:, kvw * h + MLA_NOPE:kvw * (h + 1)].T.astype(BF16)


def _mla_proj(cq, ckv, misc, cos_t, sin_t, q_norm, kv_norm, w_uq, w_ukv, batch, seq, ts=256):
    t, rank = cq.shape
    heads = w_ukv.shape[1] // (MLA_NOPE + MLA_V)
    dq = MLA_NOPE + MLA_ROPE
    w3 = w_uq.reshape(rank, heads, dq)
    w_uq_p = jnp.concatenate([w3[:, :, :MLA_NOPE].reshape(rank, -1), w3[:, :, MLA_NOPE:].reshape(rank, -1)], axis=1).astype(BF16)
    per = seq // ts
    dk = MLA_NOPE + LANES
    return pl.pallas_call(
        functools.partial(_mla_proj_kernel, heads=heads, scale=dq ** -0.5),
        grid=(t // ts,),
        in_specs=[pl.BlockSpec((ts, rank), lambda i: (i, 0)), pl.BlockSpec((ts, rank), lambda i: (i, 0)),
                  pl.BlockSpec((ts, LANES), lambda i: (i, 0)), pl.BlockSpec((ts, LANES), lambda i: (i, 0)),
                  pl.BlockSpec((ts, LANES), lambda i: (i, 0)),
                  _const_spec((1, rank)), _const_spec((1, rank)), _const_spec(w_uq_p.shape), _const_spec(w_ukv.shape)],
        out_specs=(pl.BlockSpec((1, heads, dk, ts), lambda i: (i // per, 0, 0, i % per)),
                   pl.BlockSpec((1, heads, ts, dk), lambda i: (i // per, 0, i % per, 0)),
                   pl.BlockSpec((1, heads, MLA_V, ts), lambda i: (i // per, 0, 0, i % per))),
        out_shape=(jax.ShapeDtypeStruct((batch, heads, dk, seq), BF16),
                   jax.ShapeDtypeStruct((batch, heads, seq, dk), BF16),
                   jax.ShapeDtypeStruct((batch, heads, MLA_V, seq), BF16)),
        compiler_params=_cparams(("parallel",), VMEM_LIMIT),
        name="mla_proj",
    )(cq, ckv, misc, cos_t, sin_t, q_norm.reshape(1, rank).astype(F32), kv_norm.reshape(1, rank).astype(F32),
      w_uq_p, w_ukv.astype(BF16))


def _flash_kernel(qt_ref, k_ref, vt_ref, o_ref, s_sc, acc_sc, *, tq, sub):
    seq = k_ref.shape[2]
    nq = seq // tq
    nsub = tq // sub

    def scores(qi, j, slot):
        k0 = pl.multiple_of(j * tq, tq)
        s_sc[slot] = _dot(k_ref[0, 0, pl.ds(k0, tq), :], qt_ref[0, 0, :, qi * tq:(qi + 1) * tq])

    def consume(qi, j, slot, m_prev, l_prev, masked, prefetch):
        k0 = pl.multiple_of(j * tq, tq)
        ss = [s_sc[slot, r * sub:(r + 1) * sub, :] for r in range(nsub)]
        prefetch()
        if masked:
            qpos = qi * tq + lax.broadcasted_iota(jnp.int32, ss[0].shape, 1)
            kpos = k0 + lax.broadcasted_iota(jnp.int32, ss[0].shape, 0)
            ss = [jnp.where(kpos + r * sub <= qpos, s, -1e30) for r, s in enumerate(ss)]
        m_new = functools.reduce(jnp.maximum, [jnp.max(s, axis=0, keepdims=True) for s in ss], m_prev)
        alpha = jnp.exp(m_prev - m_new)
        ps = [jnp.exp(s - m_new) for s in ss]
        l_new = alpha * l_prev + functools.reduce(lambda a, b: a + b, [jnp.sum(p, axis=0, keepdims=True) for p in ps])
        pv = [_dot(vt_ref[0, 0, :, pl.ds(pl.multiple_of(k0 + r * sub, sub), sub)], ps[r].astype(BF16)) for r in range(nsub)]
        acc_sc[...] = alpha * acc_sc[...] + functools.reduce(lambda a, b: a + b, pv)
        return m_new, l_new

    scores(0, 0, 0)
    first = 0
    for qi in range(nq):
        acc_sc[...] = jnp.zeros(acc_sc.shape, F32)
        m = jnp.full((1, tq), -1e30, F32)
        l = jnp.zeros((1, tq), F32)

        def visible(j, carry, qi=qi, first=first):
            slot = lax.rem(j + first, 2)
            return consume(qi, j, slot, carry[0], carry[1], False, lambda: scores(qi, j + 1, 1 - slot))

        m, l = lax.fori_loop(0, qi, visible, (m, l))
        dslot = (qi + first) % 2
        if qi + 1 < nq:
            m, l = consume(qi, qi, dslot, m, l, True, lambda: scores(qi + 1, 0, 1 - dslot))
        else:
            m, l = consume(qi, qi, dslot, m, l, True, lambda: None)
        o_ref[0, qi * tq:(qi + 1) * tq, :] = (acc_sc[...] / l).T.astype(o_ref.dtype)
        first = 1 - dslot


def _mla_attention(qt, k, vt, tq=512, sub=256):
    batch, heads, dk, seq = qt.shape
    dv = vt.shape[2]
    return pl.pallas_call(
        functools.partial(_flash_kernel, tq=tq, sub=min(sub, tq)),
        grid=(batch, heads),
        in_specs=[pl.BlockSpec((1, 1, dk, seq), lambda b, h: (b, h, 0, 0)),
                  pl.BlockSpec((1, 1, seq, dk), lambda b, h: (b, h, 0, 0)),
                  pl.BlockSpec((1, 1, dv, seq), lambda b, h: (b, h, 0, 0))],
        out_specs=pl.BlockSpec((1, seq, dv), lambda b, h: (b, 0, h)),
        out_shape=jax.ShapeDtypeStruct((batch, seq, heads * dv), F32),
        scratch_shapes=[pltpu.VMEM((2, tq, tq), F32), pltpu.VMEM((dv, tq), F32)],
        compiler_params=_cparams(("parallel", "parallel"), VMEM_LIMIT),
        name="mla_flash",
    )(qt, k, vt)


def _rope_tables(positions):
    half = MLA_ROPE // 2
    inv_freq = 1.0 / (ROPE_THETA ** (jnp.arange(half, dtype=F32) * (2.0 / MLA_ROPE)))
    ang = positions.astype(F32).reshape(-1)[:, None] * inv_freq
    cos, sin = jnp.cos(ang), jnp.sin(ang)
    reps = LANES // MLA_ROPE
    return jnp.tile(jnp.concatenate([cos, cos], axis=1), (1, reps)), jnp.tile(jnp.concatenate([-sin, sin], axis=1), (1, reps))


G_LANE = 64
B_LANE = 68


def _gdn_pre_kernel(x_ref, prev_ref, misc_ref, cw_ref, alog_ref, dtb_ref,
                    q_ref, k_ref, v_ref, kt_ref, gb_ref, gbt_ref, grow_ref, *, heads, chunk):
    x = x_ref[0]
    ts = x.shape[0]
    prev = jnp.where(pl.program_id(1) > 0, prev_ref[0], 0.0)
    cw = cw_ref[...]
    row8 = lax.broadcasted_iota(jnp.int32, prev.shape, 0)
    acc = x * cw[GDN_CONV - 1:GDN_CONV, :]
    for d in range(1, GDN_CONV):
        xr = pltpu.roll(x, d, axis=0)
        head = jnp.where(row8 < d, pltpu.roll(prev, d, axis=0), xr[0:8])
        xs = jnp.concatenate([head, xr[8:]], axis=0)
        acc = acc + xs * cw[GDN_CONV - 1 - d:GDN_CONV - d, :]
    y = acc * jax.nn.sigmoid(acc)
    nqk = heads * GDN_DK

    def l2n(z):
        return z * lax.rsqrt(jnp.sum(z * z, axis=-1, keepdims=True) + 1e-6)

    for h in range(heads):
        q_ref[0, :, GDN_DK * h:GDN_DK * (h + 1)] = l2n(y[:, GDN_DK * h:GDN_DK * (h + 1)]).astype(q_ref.dtype)
    kn = jnp.concatenate([l2n(y[:, nqk + GDN_DK * h:nqk + GDN_DK * (h + 1)]) for h in range(heads)], axis=1)
    k_ref[0] = kn.astype(k_ref.dtype)
    v_ref[0] = y[:, 2 * nqk:].astype(v_ref.dtype)
    knt = kn.T
    for n in range(ts // chunk):
        kt_ref[0, n] = knt[:, chunk * n:chunk * (n + 1)].astype(kt_ref.dtype)
    m = misc_ref[0]
    lane = lax.broadcasted_iota(jnp.int32, m.shape, 1)
    g = -jnp.exp(alog_ref[...]) * jax.nn.softplus(m + dtb_ref[...])
    beta = jax.nn.sigmoid(m)
    gb = jnp.where((lane >= G_LANE) & (lane < G_LANE + heads), g,
                   jnp.where((lane >= B_LANE) & (lane < B_LANE + heads), beta, 0.0))
    gb_ref[0] = gb
    gbt = gb.T[G_LANE:G_LANE + 8, :]
    for n in range(ts // chunk):
        gbt_ref[0, n] = gbt[:, chunk * n:chunk * (n + 1)]
        g_rows = jnp.concatenate([gbt[h:h + 1, chunk * n:chunk * (n + 1)] for h in range(heads)], axis=1)
        grow_ref[0, n] = jnp.broadcast_to(g_rows, (8, heads * chunk))


def _gdn_pre(qkv, misc, w_conv, a_log, dt_bias, batch, seq, ts=256):
    width = qkv.shape[-1]
    heads = a_log.shape[0]
    chunk = GDN_CHUNK
    hd = heads * GDN_DK
    x3 = qkv.reshape(batch, seq, width)
    m3 = misc.reshape(batch, seq, LANES)
    alog_row = jnp.zeros((1, LANES), F32).at[0, G_LANE:G_LANE + heads].set(a_log.astype(F32))
    dtb_row = jnp.zeros((1, LANES), F32).at[0, G_LANE:G_LANE + heads].set(dt_bias.astype(F32))
    nck = ts // chunk
    tok = lambda b, s: (b, s, 0)
    return pl.pallas_call(
        functools.partial(_gdn_pre_kernel, heads=heads, chunk=chunk),
        grid=(batch, seq // ts),
        in_specs=[pl.BlockSpec((1, ts, width), tok),
                  pl.BlockSpec((1, 8, width), lambda b, s: (b, jnp.maximum(s * (ts // 8) - 1, 0), 0)),
                  pl.BlockSpec((1, ts, LANES), tok),
                  _const_spec(w_conv.shape), _const_spec((1, LANES)), _const_spec((1, LANES))],
        out_specs=(pl.BlockSpec((1, ts, hd), tok), pl.BlockSpec((1, ts, hd), tok), pl.BlockSpec((1, ts, width - 2 * hd), tok),
                   pl.BlockSpec((1, nck, hd, chunk), lambda b, s: (b, s, 0, 0)),
                   pl.BlockSpec((1, ts, LANES), tok),
                   pl.BlockSpec((1, nck, 8, chunk), lambda b, s: (b, s, 0, 0)),
                   pl.BlockSpec((1, nck, 8, heads * chunk), lambda b, s: (b, s, 0, 0))),
        out_shape=(jax.ShapeDtypeStruct((batch, seq, hd), BF16), jax.ShapeDtypeStruct((batch, seq, hd), BF16),
                   jax.ShapeDtypeStruct((batch, seq, width - 2 * hd), BF16),
                   jax.ShapeDtypeStruct((batch, seq // chunk, hd, chunk), BF16),
                   jax.ShapeDtypeStruct((batch, seq, LANES), F32),
                   jax.ShapeDtypeStruct((batch, seq // chunk, 8, chunk), F32),
                   jax.ShapeDtypeStruct((batch, seq // chunk, 8, heads * chunk), F32)),
        compiler_params=_cparams(("parallel", "parallel"), VMEM_LIMIT),
        name="gdn_pre",
    )(x3, x3, m3, w_conv.astype(F32), alog_row, dtb_row)


def _hdot(a, b):
    return jnp.dot(a, b, preferred_element_type=F32, precision=HIGHEST)


def _gdn_local_kernel(q_ref, k_ref, v_ref, kt_ref, gb_ref, gbt_ref, grow_ref,
                      u_ref, w_ref, qd_ref, a_ref, kend_ref, egl_ref, *, heads, chunk, n_chunks):
    c = chunk
    hc = heads * c
    hd = heads * GDN_DK
    iota = lambda shape, ax: lax.broadcasted_iota(jnp.int32, shape, ax)
    ri, li = iota((c, hc), 0), iota((c, hc), 1)
    lj, lh = li % c, li // c
    tri_cat = ri >= lj
    strict_cat = ri > lj
    eye_cat = (ri == lj).astype(F32)
    r2, l2 = iota((hc, hc), 0), iota((hc, hc), 1)
    same_blk = (r2 // c) == (l2 // c)
    tri_bd = jnp.logical_and(same_blk, (r2 % c) <= (l2 % c)).astype(F32)
    head_rows = (iota((hc, hd), 0) // c) == (iota((hc, hd), 1) // GDN_DK)
    r1, c1 = iota((c, c), 0), iota((c, c), 1)
    tri_f = (r1 >= c1).astype(F32)
    tri_t = (r1 <= c1).astype(F32)
    nt = (((1,), (1,)), ((), ()))

    def bdiag(x):
        return jnp.where(same_blk, jnp.concatenate([x] * heads, axis=0), 0.0)

    def bdiag_wide(x):
        return jnp.where(head_rows, jnp.concatenate([x] * heads, axis=0), 0.0)

    def per_head_cols(cols, width):
        return jnp.concatenate([jnp.broadcast_to(col, (c, width)) for col in cols], axis=1)

    st = []
    for n in range(n_chunks):
        r0 = n * c
        gbc = gb_ref[0, r0:r0 + c, :]
        gcc = _hdot(tri_f, gbc)
        gc_cols = [gcc[:, G_LANE + h:G_LANE + h + 1] for h in range(heads)]
        gc_c = jnp.broadcast_to(gc_cols[0], (c, hc))
        for h in range(1, heads):
            gc_c = jnp.where(lh == h, jnp.broadcast_to(gc_cols[h], (c, hc)), gc_c)
        gc_r = _hdot(grow_ref[0, n], tri_bd)[0:1, :]
        decay = jnp.where(tri_cat, jnp.exp(jnp.where(tri_cat, gc_c - gc_r, 0.0)), 0.0)
        beta_w = per_head_cols([gbc[:, B_LANE + h:B_LANE + h + 1] for h in range(heads)], GDN_DK)
        eg_w = per_head_cols([jnp.exp(col) for col in gc_cols], GDN_DK)
        q = q_ref[0, r0:r0 + c, :].astype(F32) * (GDN_DK ** -0.5)
        k = k_ref[0, r0:r0 + c, :].astype(F32)
        v = v_ref[0, r0:r0 + c, :].astype(F32)
        kb = k * beta_w
        k_bd = bdiag_wide(k).astype(BF16)
        kk = lax.dot_general(kb.astype(BF16), k_bd, nt, preferred_element_type=F32)
        qk = lax.dot_general(q.astype(BF16), k_bd, nt, preferred_element_type=F32)
        lmat = jnp.where(strict_cat, kk * decay, 0.0)
        qd_ref[0, r0:r0 + c, :] = (q * eg_w).astype(qd_ref.dtype)
        a_ref[0, r0:r0 + c, :] = jnp.where(tri_cat, qk * decay, 0.0).astype(a_ref.dtype)
        gcr = _hdot(gbt_ref[0, n], tri_t)
        g_last = [gcr[h:h + 1, c - 1:c] for h in range(heads)]
        f = jnp.concatenate([jnp.broadcast_to(jnp.exp(g_last[h] - gcr[h:h + 1, :]), (GDN_DK, c)) for h in range(heads)], axis=0)
        kend_ref[0, n] = (kt_ref[0, n].astype(F32) * f).astype(kend_ref.dtype)
        egl_ref[0, n] = jnp.concatenate([jnp.broadcast_to(jnp.exp(g_last[h]), (1, LANES)) for h in range(heads)]
                                        + [jnp.zeros((8 - heads, LANES), F32)], axis=0)
        st.append(dict(p=eye_cat - lmat, sq=lmat, vb=v * beta_w, kbe=kb * eg_w))
    kpow = 2
    while kpow < c:
        for d in st:
            d["sq"] = _dot(d["sq"].astype(BF16), bdiag(d["sq"]).astype(BF16))
        for d in st:
            d["p"] = d["p"] + _dot(d["p"].astype(BF16), bdiag(d["sq"]).astype(BF16))
        kpow *= 2
    for n, d in enumerate(st):
        r0 = n * c
        tmat = d["p"].astype(BF16)
        u_ref[0, r0:r0 + c, :] = _dot(tmat, bdiag_wide(d["vb"]).astype(BF16))
        w_ref[0, r0:r0 + c, :] = _dot(tmat, bdiag_wide(d["kbe"]).astype(BF16)).astype(w_ref.dtype)


def _gdn_local(q, k, v, kt, gb, gbt, grow, cb=8):
    batch, seq, hd = q.shape
    heads = hd // GDN_DK
    chunk = GDN_CHUNK
    n_all = seq // chunk
    ts = cb * chunk
    tok = lambda b, s: (b, s, 0)
    ck = lambda b, s: (b, s, 0, 0)
    return pl.pallas_call(
        functools.partial(_gdn_local_kernel, heads=heads, chunk=chunk, n_chunks=cb),
        grid=(batch, n_all // cb),
        in_specs=[pl.BlockSpec((1, ts, hd), tok), pl.BlockSpec((1, ts, hd), tok), pl.BlockSpec((1, ts, hd), tok),
                  pl.BlockSpec((1, cb, hd, chunk), ck), pl.BlockSpec((1, ts, LANES), tok), pl.BlockSpec((1, cb, 8, chunk), ck),
                  pl.BlockSpec((1, cb, 8, heads * chunk), ck)],
        out_specs=(pl.BlockSpec((1, ts, hd), tok), pl.BlockSpec((1, ts, hd), tok), pl.BlockSpec((1, ts, hd), tok),
                   pl.BlockSpec((1, ts, heads * chunk), tok), pl.BlockSpec((1, cb, hd, chunk), ck),
                   pl.BlockSpec((1, cb, 8, LANES), ck)),
        out_shape=(jax.ShapeDtypeStruct((batch, seq, hd), F32), jax.ShapeDtypeStruct((batch, seq, hd), BF16),
                   jax.ShapeDtypeStruct((batch, seq, hd), BF16), jax.ShapeDtypeStruct((batch, seq, heads * chunk), BF16),
                   jax.ShapeDtypeStruct((batch, n_all, hd, chunk), BF16), jax.ShapeDtypeStruct((batch, n_all, 8, LANES), F32)),
        compiler_params=_cparams(("parallel", "parallel"), VMEM_LIMIT),
        name="gdn_local",
    )(q, k, v, kt, gb, gbt, grow)


def _gdn_scan_kernel(u_ref, w_ref, qd_ref, a_ref, kend_ref, egl_ref, gz_ref, gn_ref, o_ref, st_ref, *, heads, chunk, n_chunks):
    c = chunk
    nb = u_ref.shape[0]

    @pl.when(pl.program_id(1) == 0)
    def _():
        st_ref[...] = jnp.zeros(st_ref.shape, F32)

    gn = gn_ref[...]
    lanes = [(b, h) for b in range(nb) for h in range(heads)]
    state = {bh: st_ref[bh[0], bh[1]] for bh in lanes}
    col = lambda h: slice(GDN_DV * h, GDN_DV * (h + 1))
    for n in range(n_chunks):
        r0 = n * c
        sb = {bh: state[bh].astype(BF16) for bh in lanes}
        v_new = {(b, h): u_ref[b, r0:r0 + c, col(h)] - _dot(w_ref[b, r0:r0 + c, col(h)], sb[(b, h)]) for b, h in lanes}
        vb = {bh: v_new[bh].astype(BF16) for bh in lanes}
        state = {(b, h): state[(b, h)] * egl_ref[b, n][h:h + 1, :] + _dot(kend_ref[b, n, col(h), :], vb[(b, h)]) for b, h in lanes}
        for b, h in lanes:
            o = _dot(qd_ref[b, r0:r0 + c, col(h)], sb[(b, h)]) + _dot(a_ref[b, r0:r0 + c, c * h:c * (h + 1)], vb[(b, h)])
            z = gz_ref[b, r0:r0 + c, col(h)].astype(F32)
            o_ref[b, r0:r0 + c, col(h)] = (_rms_rows(o, gn) * (z * jax.nn.sigmoid(z))).astype(o_ref.dtype)
    for b, h in lanes:
        st_ref[b, h] = state[(b, h)]


def _gdn_scan(u, w, qd, a, kend, egl, gz, g_out, cs=4):
    batch, seq, hd = u.shape
    nb = max(n for n in (4, 2, 1) if batch % n == 0)
    heads = hd // GDN_DV
    chunk = GDN_CHUNK
    n_all = seq // chunk
    ts = cs * chunk
    tok = lambda b, s: (b, s, 0)
    ck = lambda b, s: (b, s, 0, 0)
    return pl.pallas_call(
        functools.partial(_gdn_scan_kernel, heads=heads, chunk=chunk, n_chunks=cs),
        grid=(batch // nb, n_all // cs),
        in_specs=[pl.BlockSpec((nb, ts, hd), tok), pl.BlockSpec((nb, ts, hd), tok), pl.BlockSpec((nb, ts, hd), tok),
                  pl.BlockSpec((nb, ts, heads * chunk), tok), pl.BlockSpec((nb, cs, hd, chunk), ck),
                  pl.BlockSpec((nb, cs, 8, LANES), ck), pl.BlockSpec((nb, ts, hd), tok), _const_spec((1, GDN_DV))],
        out_specs=pl.BlockSpec((nb, ts, hd), tok),
        out_shape=jax.ShapeDtypeStruct((batch, seq, hd), BF16),
        scratch_shapes=[pltpu.VMEM((nb, heads, GDN_DK, GDN_DV), F32)],
        compiler_params=_cparams(("parallel", "arbitrary"), VMEM_LIMIT),
        name="gdn_scan",
    )(u, w, qd, a, kend, egl, gz.reshape(batch, seq, hd), g_out.reshape(1, GDN_DV).astype(F32))


def _mix_out_kernel(s5_ref, mla_ref, gdn_ref, x_ref, w_ref, mg_ref, g_ref, b_ref, o_ref, *, alpha, parts):
    w5 = s5_ref.shape[1]
    wm = mla_ref.shape[1]
    pm = x_ref.shape[0] // parts
    rows = lambda r: slice(r * pm, (r + 1) * pm)
    accs = []
    for r in range(parts):
        mla = _rms_rows(mla_ref[rows(r), :], mg_ref[...]).astype(BF16)
        accs.append(_dot(s5_ref[rows(r), :], w_ref[0:w5, :]) + _dot(mla, w_ref[w5:w5 + wm, :])
                    + _dot(gdn_ref[rows(r), :], w_ref[w5 + wm:, :]))
    for r in range(parts):
        o_ref[rows(r), :] = _layernorm_rows(alpha * x_ref[rows(r), :] + accs[r], g_ref[...], b_ref[...])


def _mix_out(y_s5, o_mla, y_gdn, x2d, w_out, mla_gain, ln_g, ln_b, alpha, tm=512, parts=2):
    t, d = x2d.shape
    row = lambda i: (i, 0)
    return pl.pallas_call(
        functools.partial(_mix_out_kernel, alpha=alpha, parts=parts),
        grid=(t // tm,),
        in_specs=[pl.BlockSpec((tm, y_s5.shape[1]), row), pl.BlockSpec((tm, o_mla.shape[1]), row),
                  pl.BlockSpec((tm, y_gdn.shape[1]), row), pl.BlockSpec((tm, d), row),
                  _const_spec(w_out.shape), _const_spec((1, o_mla.shape[1])), _const_spec((1, d)), _const_spec((1, d))],
        out_specs=pl.BlockSpec((tm, d), row),
        out_shape=jax.ShapeDtypeStruct((t, d), F32),
        compiler_params=_cparams(("parallel",), VMEM_LIMIT),
        name="mix_out",
    )(y_s5, o_mla, y_gdn, x2d, w_out.astype(BF16), mla_gain.reshape(1, -1).astype(F32),
      ln_g.reshape(1, d).astype(F32), ln_b.reshape(1, d).astype(F32))


def _matmul_kernel(x_ref, w_ref, o_ref):
    o_ref[...] = _dot(x_ref[...].astype(BF16), w_ref[...]).astype(o_ref.dtype)


def _matmul(x, w, tm=256, tn=512):
    m, k = x.shape
    n = w.shape[1]
    return pl.pallas_call(
        _matmul_kernel,
        grid=(m // tm, n // tn),
        in_specs=[pl.BlockSpec((tm, k), lambda i, j: (i, 0)), pl.BlockSpec((k, tn), lambda i, j: (0, j))],
        out_specs=pl.BlockSpec((tm, tn), lambda i, j: (i, j)),
        out_shape=jax.ShapeDtypeStruct((m, n), F32),
        compiler_params=_cparams(("parallel", "parallel")),
        name="mem_kv_proj",
    )(x, w.astype(BF16))


def _xattn_kernel(x_ref, wq_ref, kt_ref, v_ref, wo_ref, g_ref, b_ref, wr_ref, o_ref, lg_ref, *, heads, alpha, parts):
    tm = x_ref.shape[0]
    pm = tm // parts
    nr = lg_ref.shape[0]
    nt = (((1,), (1,)), ((), ()))
    cols = lambda h: slice(XA_DH * h, XA_DH * (h + 1))
    xs = [x_ref[r * pm:(r + 1) * pm, :] for r in range(parts)]
    qs = [(_dot(x.astype(BF16), wq_ref[...]) * (XA_DH ** -0.5)).astype(BF16) for x in xs]
    ss = [[_dot(q[:, cols(h)], kt_ref[0, cols(h), :]) for h in range(heads)] for q in qs]
    ps = []
    for part in ss:
        row = []
        for s in part:
            e = jnp.exp(s - jnp.max(s, axis=-1, keepdims=True))
            row.append((e / jnp.sum(e, axis=-1, keepdims=True)).astype(BF16))
        ps.append(row)
    os = [jnp.concatenate([_dot(p[h], v_ref[0, :, cols(h)]) for h in range(heads)], axis=1).astype(BF16) for p in ps]
    ys = [_layernorm_rows(alpha * xs[r] + _dot(os[r], wo_ref[...]), g_ref[...], b_ref[...]) for r in range(parts)]
    w_hi_lo = wr_ref[...]
    for r in range(parts):
        o_ref[r * pm:(r + 1) * pm, :] = ys[r]
        y_hi = ys[r].astype(BF16)
        y_lo = (ys[r] - y_hi.astype(F32)).astype(BF16)
        p1 = lax.dot_general(w_hi_lo, y_hi, nt, preferred_element_type=F32)
        p2 = lax.dot_general(w_hi_lo[:nr], y_lo, nt, preferred_element_type=F32)
        lg_ref[:, r * pm:(r + 1) * pm] = p1[:nr] + p1[nr:] + p2


def _xattn(x2d, kt, v, w_q, w_o, ln_g, ln_b, w_router_t, seq, alpha, tm=512, parts=2):
    t, d = x2d.shape
    width = w_q.shape[1]
    heads = width // XA_DH
    mlen = v.shape[1]
    per = seq // tm
    nr = w_router_t.shape[0]
    w_hi = w_router_t.astype(BF16)
    w_lo = (w_router_t - w_hi.astype(F32)).astype(BF16)
    row = lambda i: (i, 0)
    return pl.pallas_call(
        functools.partial(_xattn_kernel, heads=heads, alpha=alpha, parts=parts),
        grid=(t // tm,),
        in_specs=[pl.BlockSpec((tm, d), row), _const_spec((d, width)),
                  pl.BlockSpec((1, width, mlen), lambda i: (i // per, 0, 0)),
                  pl.BlockSpec((1, mlen, width), lambda i: (i // per, 0, 0)),
                  _const_spec((width, d)), _const_spec((1, d)), _const_spec((1, d)), _const_spec((2 * nr, d))],
        out_specs=(pl.BlockSpec((tm, d), row), pl.BlockSpec((nr, tm), lambda i: (0, i))),
        out_shape=(jax.ShapeDtypeStruct((t, d), F32), jax.ShapeDtypeStruct((nr, t), F32)),
        compiler_params=_cparams(("parallel",), VMEM_LIMIT),
        name="xattn",
    )(x2d, w_q.astype(BF16), kt, v, w_o.astype(BF16), ln_g.reshape(1, d).astype(F32), ln_b.reshape(1, d).astype(F32),
      jnp.concatenate([w_hi, w_lo], axis=0))


def _router_kernel(lg_ref, bias_ref, eid_ref, gate_ref):
    lg = lg_ref[...] + bias_ref[...]
    ng, ne = MOE_GROUPS, MOE_PER_GROUP
    grp = [lg[g:g + 1, :] for g in range(ng)]
    gmax = functools.reduce(jnp.maximum, grp)
    gexp = [jnp.exp(r - gmax) for r in grp]
    gsum = functools.reduce(lambda a, b: a + b, gexp)
    pg = [e / gsum for e in gexp]
    best, gsel = pg[0], jnp.zeros(pg[0].shape, jnp.int32)
    for g in range(1, ng):
        better = pg[g] > best
        gsel = jnp.where(better, g, gsel)
        best = jnp.where(better, pg[g], best)
    le = []
    for e in range(ne):
        r = lg[ng + e:ng + e + 1, :]
        for g in range(1, ng):
            r = jnp.where(gsel == g, lg[ng + g * ne + e:ng + g * ne + e + 1, :], r)
        le.append(r)
    emax = functools.reduce(jnp.maximum, le)
    eexp = [jnp.exp(r - emax) for r in le]
    esum = functools.reduce(lambda a, b: a + b, eexp)
    pe = [e / esum for e in eexp]
    sel, val = [], []
    for k in range(MOE_TOPK):
        bv, bi = None, None
        for e in range(ne):
            cand = pe[e]
            for prev in sel:
                cand = jnp.where(prev == e, -1.0, cand)
            if bv is None:
                bv, bi = cand, jnp.zeros(cand.shape, jnp.int32)
            else:
                better = cand > bv
                bi = jnp.where(better, e, bi)
                bv = jnp.where(better, cand, bv)
        sel.append(bi)
        val.append(bv)
    tot = functools.reduce(lambda a, b: a + b, val)
    zero_i = jnp.zeros((8 - MOE_TOPK,) + sel[0].shape[1:], jnp.int32)
    zero_f = jnp.zeros((8 - MOE_TOPK,) + sel[0].shape[1:], F32)
    eid_ref[...] = jnp.concatenate([gsel * ne + s for s in sel] + [zero_i], axis=0)
    gate_ref[...] = jnp.concatenate([best * v / tot for v in val] + [zero_f], axis=0)


def _router(lg_t, bias_col, tn=2048):
    nr, t = lg_t.shape
    tn = min(tn, t)
    return pl.pallas_call(
        _router_kernel,
        grid=(t // tn,),
        in_specs=[pl.BlockSpec((nr, tn), lambda i: (0, i)), _const_spec((nr, 1))],
        out_specs=(pl.BlockSpec((8, tn), lambda i: (0, i)), pl.BlockSpec((8, tn), lambda i: (0, i))),
        out_shape=(jax.ShapeDtypeStruct((8, t), jnp.int32), jax.ShapeDtypeStruct((8, t), F32)),
        compiler_params=_cparams(("parallel",)),
        name="router",
    )(lg_t, bias_col)


def _start_row_gather(idx_ref, src_hbm, buf, sem):
    for r in range(buf.shape[0]):
        pltpu.make_async_copy(src_hbm.at[pl.ds(idx_ref[r], 1)], buf.at[pl.ds(r, 1)], sem).start(priority=r % 2)


def _wait_row_gather(buf, sem):
    pltpu.make_async_copy(buf, buf, sem).wait()


def _expert_kernel(be_ref, nu_ref, idx0_ref, idxn_ref, x_hbm, wgu_ref, wd_ref, y_ref, xbuf, sem, wgu_sc, wd_sc, *, ff):
    i = pl.program_id(0)
    n_used = nu_ref[0]
    slot = lax.rem(i, 2)
    changed = jnp.logical_or(i == 0, be_ref[i] != be_ref[jnp.maximum(i - 1, 0)])

    @pl.when(changed)
    def _():
        wgu_sc[...] = wgu_ref[0, 0].astype(BF16)
        wd_sc[...] = wd_ref[0, 0].astype(BF16)

    @pl.when(i == 0)
    def _():
        _start_row_gather(idx0_ref, x_hbm, xbuf.at[0], sem.at[0])

    @pl.when(i < n_used)
    def _():
        _wait_row_gather(xbuf.at[slot], sem.at[slot])
        x = xbuf[slot].astype(BF16)
        _start_row_gather(idxn_ref, x_hbm, xbuf.at[1 - slot], sem.at[1 - slot])
        gu = _dot(x, wgu_sc[...])
        gate = gu[:, :ff]
        h = gate * jax.nn.sigmoid(gate) * gu[:, ff:]
        y_ref[...] = _dot(h.astype(BF16), wd_sc[...])

    @pl.when(i == n_used)
    def _():
        _wait_row_gather(xbuf.at[slot], sem.at[slot])

    @pl.when(i >= n_used)
    def _():
        y_ref[...] = jnp.zeros(y_ref.shape, y_ref.dtype)


def _experts(x2d, row_tok, blk_exp, n_used, w_gate_up, w_down, layer, rb):
    rows = row_tok.shape[0]
    d = x2d.shape[1]
    ff = w_down.shape[2]
    nblk = rows // rb
    grid_spec = pltpu.PrefetchScalarGridSpec(
        num_scalar_prefetch=2,
        grid=(nblk,),
        in_specs=[pl.BlockSpec((rb,), lambda i, be, nu: (0,), memory_space=pltpu.SMEM),
                  pl.BlockSpec((rb,), lambda i, be, nu: (jnp.minimum(i + 1, nblk - 1),), memory_space=pltpu.SMEM),
                  pl.BlockSpec(memory_space=pl.ANY),
                  pl.BlockSpec((1, 1, d, 2 * ff), lambda i, be, nu: (layer, be[i], 0, 0)),
                  pl.BlockSpec((1, 1, ff, d), lambda i, be, nu: (layer, be[i], 0, 0))],
        out_specs=pl.BlockSpec((rb, d), lambda i, be, nu: (i, 0)),
        scratch_shapes=[pltpu.VMEM((2, rb, d), x2d.dtype), pltpu.SemaphoreType.DMA((2,)),
                        pltpu.VMEM((d, 2 * ff), BF16), pltpu.VMEM((ff, d), BF16)],
    )
    return pl.pallas_call(
        functools.partial(_expert_kernel, ff=ff),
        grid_spec=grid_spec,
        out_shape=jax.ShapeDtypeStruct((rows, d), F32),
        compiler_params=_cparams(("arbitrary",), VMEM_LIMIT),
        name="experts",
    )(blk_exp, n_used, row_tok, row_tok, x2d, w_gate_up, w_down)


def _moe_plan(eid, n_exp, rb):
    t, topk = eid.shape
    m = t * topk
    flat_e = eid.reshape(m)
    onehot = (flat_e[:, None] == jnp.arange(n_exp, dtype=jnp.int32)[None, :]).astype(jnp.int32)
    csum = jnp.cumsum(onehot, axis=0)
    counts = csum[-1]
    pcounts = (counts + rb - 1) // rb * rb
    pends = jnp.cumsum(pcounts)
    pstarts = pends - pcounts
    dest = jnp.sum(onehot * (pstarts[None, :] + csum - 1), axis=1)
    rows = m + n_exp * rb
    row_tok = (jnp.arange(rows, dtype=jnp.int32) % t).at[dest].set(jnp.arange(m, dtype=jnp.int32) // topk)
    nblk = rows // rb
    blk_start = jnp.arange(nblk, dtype=jnp.int32) * rb
    blk_exp = jnp.minimum(jnp.sum((pends[None, :] <= blk_start[:, None]).astype(jnp.int32), axis=1), n_exp - 1)
    n_used = (pends[-1] // rb).astype(jnp.int32).reshape(1)
    return dest.astype(jnp.int32), row_tok, blk_exp, n_used


def _moe_out_kernel(*refs, alpha):
    idx0 = refs[:MOE_TOPK]
    idxn = refs[MOE_TOPK:2 * MOE_TOPK]
    y_hbm, x_ref, gate_ref, g_ref, b_ref, o_ref, ybuf, sem = refs[2 * MOE_TOPK:]
    i = pl.program_id(0)
    slot = lax.rem(i, 2)

    @pl.when(i == 0)
    def _():
        for k in range(MOE_TOPK):
            _start_row_gather(idx0[k], y_hbm, ybuf.at[0, k], sem.at[0])

    @pl.when(i + 1 < pl.num_programs(0))
    def _():
        for k in range(MOE_TOPK):
            _start_row_gather(idxn[k], y_hbm, ybuf.at[1 - slot, k], sem.at[1 - slot])

    _wait_row_gather(ybuf.at[slot], sem.at[slot])
    gate = gate_ref[...]
    ffn = gate[:, 0:1] * ybuf[slot, 0]
    for k in range(1, MOE_TOPK):
        ffn = ffn + gate[:, k:k + 1] * ybuf[slot, k]
    o_ref[...] = _layernorm_rows(alpha * x_ref[...] + ffn, g_ref[...], b_ref[...])


def _moe_out(yr, dest_k, x2d, gate, ln_g, ln_b, alpha, tm=256):
    t, d = x2d.shape
    row = lambda i: (i, 0)
    nt = t // tm
    first = [pl.BlockSpec((tm,), functools.partial(lambda k, i: (k * nt,), k), memory_space=pltpu.SMEM) for k in range(MOE_TOPK)]
    nxt = [pl.BlockSpec((tm,), functools.partial(lambda k, i: (k * nt + jnp.minimum(i + 1, nt - 1),), k), memory_space=pltpu.SMEM)
           for k in range(MOE_TOPK)]
    return pl.pallas_call(
        functools.partial(_moe_out_kernel, alpha=alpha),
        grid=(nt,),
        in_specs=first + nxt + [pl.BlockSpec(memory_space=pl.ANY), pl.BlockSpec((tm, d), row),
                                pl.BlockSpec((tm, gate.shape[1]), row), _const_spec((1, d)), _const_spec((1, d))],
        out_specs=pl.BlockSpec((tm, d), row),
        out_shape=jax.ShapeDtypeStruct((t, d), F32),
        scratch_shapes=[pltpu.VMEM((2, MOE_TOPK, tm, d), yr.dtype), pltpu.SemaphoreType.DMA((2,))],
        compiler_params=_cparams(("arbitrary",), VMEM_LIMIT),
        name="moe_out",
    )(*([dest_k] * (2 * MOE_TOPK)), yr, x2d, gate, ln_g.reshape(1, d).astype(F32), ln_b.reshape(1, d).astype(F32))


def _moe(x2, lg_t, b_group, b_expert, w_gate_up, w_down, layer, ln_g, ln_b, alpha):
    t, d = x2.shape
    n_exp = w_gate_up.shape[1]
    nr = lg_t.shape[0]
    bias = jnp.zeros((nr, 1), F32).at[:MOE_GROUPS + n_exp, 0].set(jnp.concatenate([b_group, b_expert]).astype(F32))
    eid_t, gate_t = _router(lg_t, bias)
    eid = eid_t[:MOE_TOPK].T
    gate = gate_t[:MOE_TOPK].T
    dest, row_tok, blk_exp, n_used = _moe_plan(eid, n_exp, MOE_ROW_BLOCK)
    yr = _experts(x2, row_tok, blk_exp, n_used, w_gate_up, w_down, layer, MOE_ROW_BLOCK)
    return _moe_out(yr, dest.reshape(t, MOE_TOPK).T.reshape(-1), x2, gate, ln_g, ln_b, alpha)


def kernel(x, mem, positions, w_in, s5_lambda_re, s5_lambda_im, s5_log_step, s5_b_re, s5_b_im, s5_c_re, s5_c_im, s5_d, s5_w_glu, s5_b_glu, s5_out_norm, mla_q_norm, mla_w_uq, mla_kv_norm, mla_w_ukv, mla_out_norm, gdn_conv, gdn_a_log, gdn_dt_bias, gdn_out_norm, w_out, ln1_g, ln1_b, xa_w_q, xa_w_k, xa_w_v, xa_w_o, ln2_g, ln2_b, moe_w_group, moe_b_group, moe_w_expert, moe_b_expert, moe_w_gate_up, moe_w_down, ln3_g, ln3_b):
    batch, seq, d = x.shape
    t = batch * seq
    depth = w_in.shape[0]
    alpha = (2 * depth) ** 0.25
    mlen = mem.shape[1]
    s5_w = s5_w_glu.shape[1]
    rank_q = mla_w_uq.shape[1]
    rank_kv = mla_w_ukv.shape[1]
    g_heads = gdn_a_log.shape[1]
    g_qk = g_heads * GDN_DK
    g_v = gdn_conv.shape[2] - 2 * g_qk
    o_kr = s5_w + rank_q + rank_kv
    o_gq = o_kr + MLA_ROPE
    o_gz = o_gq + 2 * g_qk + g_v
    o_ga = o_gz + g_v
    widths = (s5_w, rank_q, rank_kv, LANES, 2 * g_qk + g_v, g_v)
    assert G_LANE == MLA_ROPE and B_LANE == G_LANE + g_heads and o_ga + 2 * g_heads == w_in.shape[2]

    cos_t, sin_t = _rope_tables(positions)
    mem2 = mem.reshape(batch * mlen, d)
    xt = x.reshape(t, d)
    for l in range(depth):
        w = w_in[l]
        w_packed = jnp.concatenate(
            [w[:, :o_gq], w[:, o_ga:], jnp.zeros((d, LANES - MLA_ROPE - 2 * g_heads), w.dtype), w[:, o_gq:o_ga]], axis=1).astype(BF16)
        u, cq, ckv, misc, qkv, gz = _in_proj(xt, w_packed, widths, (BF16, BF16, BF16, F32, F32, BF16))

        tables = _s5_tables(s5_lambda_re[l], s5_lambda_im[l], s5_log_step[l], s5_b_re[l], s5_b_im[l], s5_c_re[l],
                            s5_c_im[l], s5_d[l], S5_CHUNK, seq // S5_CHUNK)
        y_s5 = _s5_glu(_s5_scan(u, tables, batch, seq, S5_CHUNK), s5_w_glu[l], s5_b_glu[l], s5_out_norm[l])

        qt, kk, vt = _mla_proj(cq, ckv, misc, cos_t, sin_t, mla_q_norm[l], mla_kv_norm[l], mla_w_uq[l], mla_w_ukv[l], batch, seq)
        o_mla = _mla_attention(qt, kk, vt).reshape(t, -1)

        gq, gk, gv, gkt, ggb, ggbt, ggrow = _gdn_pre(qkv, misc, gdn_conv[l], gdn_a_log[l], gdn_dt_bias[l], batch, seq)
        local = _gdn_local(gq, gk, gv, gkt, ggb, ggbt, ggrow)
        y_gdn = _gdn_scan(*local, gz, gdn_out_norm[l]).reshape(t, -1)

        x1 = _mix_out(y_s5, o_mla, y_gdn, xt, w_out[l], mla_out_norm[l], ln1_g[l], ln1_b[l], alpha)

        kv_mem = _matmul(mem2, jnp.concatenate([xa_w_k[l], xa_w_v[l]], axis=1))
        xa_w = xa_w_k.shape[2]
        kt_mem = kv_mem[:, :xa_w].reshape(batch, mlen, xa_w).transpose(0, 2, 1).astype(BF16)
        v_mem = kv_mem[:, xa_w:].reshape(batch, mlen, xa_w).astype(BF16)
        n_route = MOE_GROUPS + moe_w_expert.shape[2]
        w_router_t = jnp.pad(jnp.concatenate([moe_w_group[l], moe_w_expert[l]], axis=1).T.astype(F32),
                             ((0, (-n_route) % 8), (0, 0)))
        x2, lg_t = _xattn(x1, kt_mem, v_mem, xa_w_q[l], xa_w_o[l], ln2_g[l], ln2_b[l], w_router_t, seq, alpha)

        xt = _moe(x2, lg_t, moe_b_group[l], moe_b_expert[l], moe_w_gate_up, moe_w_down, l, ln3_g[l], ln3_b[l], alpha)
    return xt.reshape(batch, seq, d)
```

```python
import functools
import math

import jax
import jax.numpy as jnp
from jax import lax
from jax.experimental import pallas as pl
from jax.experimental.pallas import tpu as pltpu

F32 = jnp.float32
BF16 = jnp.bfloat16
HIGHEST = lax.Precision.HIGHEST

S5_CH = 16
S5_STATE = 64
S5_LAMBDA_RE_MAX = -1e-4
S5_CHUNK = 16
MLA_NOPE = 128
MLA_ROPE = 64
MLA_V = 128
ROPE_THETA = 10000.0
GDN_DK = 128
GDN_DV = 128
GDN_CONV = 4
GDN_CHUNK = 64
XA_DH = 128
MOE_GROUPS = 4
MOE_PER_GROUP = 8
MOE_TOPK = 2
MOE_ROW_BLOCK = 256

LANES = 128
VMEM_LIMIT = 56 * 1024 * 1024


def _cparams(sem, vmem=None, flags=None):
    return pltpu.CompilerParams(dimension_semantics=sem, vmem_limit_bytes=vmem, flags=flags)


def _const_spec(shape):
    nd = len(shape)
    return pl.BlockSpec(shape, lambda *_: (0,) * nd)


def _rms_rows(x, gain, eps=1e-6):
    return x * lax.rsqrt(jnp.mean(x * x, axis=-1, keepdims=True) + eps) * gain


def _layernorm_rows(x, g, b, eps=1e-5):
    mu = jnp.mean(x, axis=-1, keepdims=True)
    xc = x - mu
    var = jnp.mean(xc * xc, axis=-1, keepdims=True)
    return xc * lax.rsqrt(var + eps) * g + b


def _dot(a, b):
    return jnp.dot(a, b, preferred_element_type=F32)


def _in_proj_kernel(x_ref, w_ref, u_ref, cq_ref, ckv_ref, misc_ref, qkv_ref, gz_ref, u_sc, *, splits, chunk):
    xb = x_ref[...].astype(BF16)
    outs = (cq_ref, ckv_ref, misc_ref, qkv_ref, gz_ref)
    for o_ref, (lo, hi) in zip(outs, splits[1:]):
        o_ref[...] = _dot(xb, w_ref[:, lo:hi]).astype(o_ref.dtype)
    lo, hi = splits[0]
    u = _dot(xb, w_ref[:, lo:hi])
    sets, tm = u_sc.shape[0], u_sc.shape[1]
    for s in range(sets):
        u_sc[s] = u[:, LANES * s:LANES * (s + 1)]
    for s in range(sets):
        for j in range(chunk):
            u_ref[s, :, LANES * j:LANES * (j + 1)] = u_sc[s, pl.ds(j, tm // chunk, stride=chunk), :].astype(u_ref.dtype)


def _in_proj(x2d, w_packed, widths, dtypes, chunk, tm=256):
    t, d = x2d.shape
    splits, lo = [], 0
    for w in widths:
        splits.append((lo, lo + w))
        lo += w
    sets = widths[0] // LANES
    out_shape = (jax.ShapeDtypeStruct((sets, t // chunk, chunk * LANES), dtypes[0]),) + tuple(
        jax.ShapeDtypeStruct((t, w), dt) for w, dt in zip(widths[1:], dtypes[1:]))
    out_specs = (pl.BlockSpec((sets, tm // chunk, chunk * LANES), lambda i: (0, i, 0)),) + tuple(
        pl.BlockSpec((tm, w), lambda i: (i, 0)) for w in widths[1:])
    return pl.pallas_call(
        functools.partial(_in_proj_kernel, splits=tuple(splits), chunk=chunk),
        grid=(t // tm,),
        in_specs=[pl.BlockSpec((tm, d), lambda i: (i, 0)), _const_spec(w_packed.shape)],
        out_specs=out_specs,
        out_shape=out_shape,
        scratch_shapes=[pltpu.VMEM((sets, tm, LANES), F32)],
        compiler_params=_cparams(("parallel",), VMEM_LIMIT),
        name="in_proj",
    )(x2d, w_packed)


def _s5_tables(lam_re, lam_im, log_step, b_re, b_im, c_re, c_im, d_skip, chunk, n_chunks):
    g, p = lam_re.shape
    h = b_re.shape[-1]
    gs = LANES // h
    sets = g // gs
    lr = jnp.minimum(lam_re.astype(F32), S5_LAMBDA_RE_MAX)
    li = lam_im.astype(F32)
    dt = jnp.exp(log_step.astype(F32))[:, None]
    mag = jnp.exp(lr * dt)
    th = li * dt
    ab_re, ab_im = mag * jnp.cos(th), mag * jnp.sin(th)
    den = lr * lr + li * li
    nr, ni = ab_re - 1.0, ab_im
    fr = (nr * lr + ni * li) / den
    fi = (ni * lr - nr * li) / den
    br, bi = b_re.astype(F32), b_im.astype(F32)
    bb_re = fr[..., None] * br - fi[..., None] * bi
    bb_im = fr[..., None] * bi + fi[..., None] * br
    cr, ci = c_re.astype(F32), c_im.astype(F32)
    n = jnp.arange(chunk + 1, dtype=F32)[:, None, None]
    pmag = jnp.exp(n * (lr * dt)[None])
    pr, pi = pmag * jnp.cos(n * th[None]), pmag * jnp.sin(n * th[None])

    crt, cit = cr.transpose(0, 2, 1), ci.transpose(0, 2, 1)
    cb_re = crt[:, :, :, None] * bb_re[:, :, None, :] - cit[:, :, :, None] * bb_im[:, :, None, :]
    cb_im = crt[:, :, :, None] * bb_im[:, :, None, :] + cit[:, :, :, None] * bb_re[:, :, None, :]
    kk = jnp.sum(pr[:chunk, :, :, None, None] * cb_re[None] - pi[:chunk, :, :, None, None] * cb_im[None], axis=2)
    kk = kk.at[0].add(jnp.eye(h, dtype=F32)[None] * d_skip.astype(F32)[:, :, None])
    k_c = kk.reshape(chunk, sets, gs, h, h).transpose(1, 0, 4, 2, 3).reshape(sets, chunk, h, LANES)

    n_rev = jnp.arange(chunk - 1, -1, -1, dtype=F32)[:, None, None]
    rmag = jnp.exp(n_rev * (lr * dt)[None])
    pr_rev, pi_rev = rmag * jnp.cos(n_rev * th[None]), rmag * jnp.sin(n_rev * th[None])
    e_re = pr_rev[:, :, :, None] * bb_re[None] - pi_rev[:, :, :, None] * bb_im[None]
    e_im = pr_rev[:, :, :, None] * bb_im[None] + pi_rev[:, :, :, None] * bb_re[None]
    e_all = jnp.stack([e_re, e_im], axis=0).reshape(2, chunk, sets, gs, p, h)
    e_c = e_all.transpose(2, 1, 5, 0, 3, 4).reshape(sets, chunk * h, 2 * gs * p)

    pr1, pi1 = pr[1:], pi[1:]
    f_re = cr[None] * pr1[:, :, None, :] - ci[None] * pi1[:, :, None, :]
    f_im = cr[None] * pi1[:, :, None, :] + ci[None] * pr1[:, :, None, :]
    f_all = jnp.stack([f_re, -f_im], axis=0).reshape(2, chunk, sets, gs, h, p)
    f_c = f_all.transpose(2, 1, 4, 0, 3, 5).reshape(sets, chunk * h, 2 * gs * p)

    steps = max(1, int(math.ceil(math.log2(n_chunks))))
    ar, ai = pr[chunk].reshape(sets, gs * p), pi[chunk].reshape(sets, gs * p)
    a1, a2 = [], []
    for _ in range(steps):
        a1.append(jnp.concatenate([ar, ar], axis=-1))
        a2.append(jnp.concatenate([-ai, ai], axis=-1))
        ar, ai = ar * ar - ai * ai, 2.0 * ar * ai
    pad = (-steps) % 8
    a1 = jnp.pad(jnp.stack(a1, axis=1), ((0, 0), (0, pad), (0, 0)))
    a2 = jnp.pad(jnp.stack(a2, axis=1), ((0, 0), (0, pad), (0, 0)))
    return k_c, e_c, f_c, a1, a2


def _s5_kernel(u_ref, kc_ref, ec_ref, fc_ref, a1_ref, a2_ref, y_ref, csup_sc, e_sc, ft_sc, *, n_chunks, steps):
    chunk, h = kc_ref.shape[1], kc_ref.shape[2]
    gs = LANES // h

    @pl.when(pl.program_id(1) == 0)
    def _():
        rg = lax.broadcasted_iota(jnp.int32, (LANES, LANES), 0) // h
        own128 = rg == lax.broadcasted_iota(jnp.int32, (LANES, LANES), 1) // h
        width = e_sc.shape[1]
        rgw = lax.broadcasted_iota(jnp.int32, (LANES, width), 0) // h
        cgw = (lax.broadcasted_iota(jnp.int32, (LANES, width), 1) % (width // 2)) // (width // 2 // gs)
        own_w = rgw == cgw
        blocks = [jnp.where(own128, jnp.concatenate([kc_ref[0, t]] * gs, axis=0), 0.0).astype(BF16) for t in range(chunk)]
        zero = jnp.zeros((LANES, LANES), BF16)
        for sg in range(chunk // 2):
            below = blocks[2 * sg - 1] if sg > 0 else zero
            csup_sc[sg] = jnp.concatenate([jnp.concatenate([blocks[2 * sg], blocks[2 * sg + 1]], axis=1),
                                           jnp.concatenate([below, blocks[2 * sg]], axis=1)], axis=0)
        for j in range(chunk):
            rows = slice(LANES * j, LANES * (j + 1))
            e_sc[rows, :] = jnp.where(own_w, jnp.concatenate([ec_ref[0, h * j:h * (j + 1), :]] * gs, axis=0), 0.0).astype(BF16)
            ft_sc[rows, :] = jnp.where(own_w, jnp.concatenate([fc_ref[0, h * j:h * (j + 1), :]] * gs, axis=0), 0.0).astype(BF16)

    u = u_ref[0]
    s = _dot(u, e_sc[...])
    rows, width = s.shape
    c_idx = lax.broadcasted_iota(jnp.int32, (rows, width), 0) % n_chunks
    a1 = a1_ref[0]
    a2 = a2_ref[0]
    for k in range(steps):
        sh = 1 << k
        prev = jnp.where(c_idx >= sh, pltpu.roll(s, sh, axis=0), 0.0)
        s = s + a1[k:k + 1, :] * prev + a2[k:k + 1, :] * pltpu.roll(prev, width // 2, axis=1)
    s_in = jnp.where(c_idx >= 1, pltpu.roll(s, 1, axis=0), 0.0)
    y_state = lax.dot_general(s_in.astype(BF16), ft_sc[...], (((1,), (1,)), ((), ())), preferred_element_type=F32)
    sw = 2 * LANES
    for i in range(u.shape[1] // sw):
        acc = y_state[:, i * sw:(i + 1) * sw]
        for sg in range(i + 1):
            acc = acc + _dot(u[:, (i - sg) * sw:(i - sg + 1) * sw], csup_sc[sg])
        y_ref[0, :, i * sw:(i + 1) * sw] = acc


def _s5_scan(us, tables, batch, seq, chunk, batches_per_block=2):
    k_c, e_c, f_c, a1, a2 = tables
    sets = k_c.shape[0]
    n_chunks = seq // chunk
    steps = max(1, int(math.ceil(math.log2(n_chunks))))
    rows = batch * n_chunks
    rb = batches_per_block * n_chunks
    width = chunk * LANES
    sw = e_c.shape[2]
    const = lambda arr: pl.BlockSpec((1,) + arr.shape[1:], lambda i, j: (i,) + (0,) * (arr.ndim - 1))
    return pl.pallas_call(
        functools.partial(_s5_kernel, n_chunks=n_chunks, steps=steps),
        grid=(sets, rows // rb),
        in_specs=[pl.BlockSpec((1, rb, width), lambda i, j: (i, j, 0)),
                  const(k_c), const(e_c), const(f_c), const(a1), const(a2)],
        out_specs=pl.BlockSpec((1, rb, width), lambda i, j: (i, j, 0)),
        out_shape=jax.ShapeDtypeStruct((sets, rows, width), F32),
        scratch_shapes=[pltpu.VMEM((chunk // 2, 2 * LANES, 2 * LANES), BF16), pltpu.VMEM((width, sw), BF16),
                        pltpu.VMEM((width, sw), BF16)],
        compiler_params=_cparams(("parallel", "arbitrary"), VMEM_LIMIT),
        name="s5_scan",
    )(us, k_c, e_c, f_c, a1, a2)


def _s5_glu_kernel(y_ref, w_ref, b_ref, g_ref, o_ref, y_sc, *, chunk):
    sets, tm = y_sc.shape[0], y_sc.shape[1]
    for s in range(sets):
        for j in range(chunk):
            y_sc[s, pl.ds(j, tm // chunk, stride=chunk), :] = y_ref[s, :, LANES * j:LANES * (j + 1)]
    y = jax.nn.gelu(jnp.concatenate([y_sc[s] for s in range(sets)], axis=1))
    z = _dot(y.astype(BF16), w_ref[...]) + b_ref[...]
    y = y * jax.nn.sigmoid(z)
    o_ref[...] = _rms_rows(y, g_ref[...]).astype(o_ref.dtype)


def _s5_glu(ys, w_glu, b_glu, g_out, chunk, tm=1024):
    sets, rows, width = ys.shape
    t = rows * chunk
    w = sets * LANES
    tm = min(tm, t)
    return pl.pallas_call(
        functools.partial(_s5_glu_kernel, chunk=chunk),
        grid=(t // tm,),
        in_specs=[pl.BlockSpec((sets, tm // chunk, width), lambda i: (0, i, 0)), _const_spec((w, w)), _const_spec((1, w)),
                  _const_spec((1, w))],
        out_specs=pl.BlockSpec((tm, w), lambda i: (i, 0)),
        out_shape=jax.ShapeDtypeStruct((t, w), BF16),
        scratch_shapes=[pltpu.VMEM((sets, tm, LANES), F32)],
        compiler_params=_cparams(("parallel",), VMEM_LIMIT),
        name="s5_glu",
    )(ys, w_glu.astype(BF16), b_glu.reshape(1, w).astype(F32), g_out.reshape(1, w).astype(F32))


def _mla_proj_kernel(cq_ref, ckv_ref, misc_ref, cos_ref, sin_ref, qn_ref, kvn_ref, wuq_ref, wukv_ref,
                     qt_ref, k_ref, vt_ref, *, heads, scale):
    cq = _rms_rows(cq_ref[...].astype(F32), qn_ref[...]).astype(BF16)
    q = _dot(cq, wuq_ref[...]) * scale
    ckv = _rms_rows(ckv_ref[...].astype(F32), kvn_ref[...]).astype(BF16)
    kv = _dot(ckv, wukv_ref[...])
    cos = cos_ref[...]
    sin = sin_ref[...]
    lane = lax.broadcasted_iota(jnp.int32, cos.shape, 1)
    first_half = (lane % MLA_ROPE) < (MLA_ROPE // 2)

    def rope(x):
        partner = jnp.where(first_half, pltpu.roll(x, LANES - MLA_ROPE // 2, axis=1), pltpu.roll(x, MLA_ROPE // 2, axis=1))
        return x * cos + partner * sin

    kpe = rope(misc_ref[...])
    kpe_lo = jnp.where(lane < MLA_ROPE, kpe, 0.0)
    kpe_hi = pltpu.roll(kpe_lo, MLA_ROPE, axis=1)
    nope_w = heads * MLA_NOPE
    for pair in range(heads // 2):
        q_pe = rope(q[:, nope_w + LANES * pair:nope_w + LANES * (pair + 1)])
        for h in (2 * pair, 2 * pair + 1):
            qh = jnp.concatenate([q[:, MLA_NOPE * h:MLA_NOPE * (h + 1)], q_pe], axis=1)
            qt_ref[0, h] = qh.T.astype(BF16)
            kvw = MLA_NOPE + MLA_V
            kh = jnp.concatenate([kv[:, kvw * h:kvw * h + MLA_NOPE], kpe_lo if h % 2 == 0 else kpe_hi], axis=1)
            k_ref[0, h] = kh.astype(BF16)
            vt_ref[0, h] = kv[:, kvw * h + MLA_NOPE:kvw * (h + 1)].T.astype(BF16)


def _mla_proj(cq, ckv, misc, cos_t, sin_t, q_norm, kv_norm, w_uq, w_ukv, batch, seq, ts=256):
    t, rank = cq.shape
    heads = w_ukv.shape[1] // (MLA_NOPE + MLA_V)
    dq = MLA_NOPE + MLA_ROPE
    w3 = w_uq.reshape(rank, heads, dq)
    w_uq_p = jnp.concatenate([w3[:, :, :MLA_NOPE].reshape(rank, -1), w3[:, :, MLA_NOPE:].reshape(rank, -1)], axis=1).astype(BF16)
    per = seq // ts
    dk = MLA_NOPE + LANES
    return pl.pallas_call(
        functools.partial(_mla_proj_kernel, heads=heads, scale=dq ** -0.5),
        grid=(t // ts,),
        in_specs=[pl.BlockSpec((ts, rank), lambda i: (i, 0)), pl.BlockSpec((ts, rank), lambda i: (i, 0)),
                  pl.BlockSpec((ts, LANES), lambda i: (i, 0)), pl.BlockSpec((ts, LANES), lambda i: (i, 0)),
                  pl.BlockSpec((ts, LANES), lambda i: (i, 0)),
                  _const_spec((1, rank)), _const_spec((1, rank)), _const_spec(w_uq_p.shape), _const_spec(w_ukv.shape)],
        out_specs=(pl.BlockSpec((1, heads, dk, ts), lambda i: (i // per, 0, 0, i % per)),
                   pl.BlockSpec((1, heads, ts, dk), lambda i: (i // per, 0, i % per, 0)),
                   pl.BlockSpec((1, heads, MLA_V, ts), lambda i: (i // per, 0, 0, i % per))),
        out_shape=(jax.ShapeDtypeStruct((batch, heads, dk, seq), BF16),
                   jax.ShapeDtypeStruct((batch, heads, seq, dk), BF16),
                   jax.ShapeDtypeStruct((batch, heads, MLA_V, seq), BF16)),
        compiler_params=_cparams(("parallel",), VMEM_LIMIT),
        name="mla_proj",
    )(cq, ckv, misc, cos_t, sin_t, q_norm.reshape(1, rank).astype(F32), kv_norm.reshape(1, rank).astype(F32),
      w_uq_p, w_ukv.astype(BF16))


def _flash_kernel(qt_ref, k_ref, vt_ref, o_ref, s_sc, acc_sc, *, tq, sub):
    seq = k_ref.shape[2]
    nq = seq // tq
    nsub = tq // sub

    def scores(qi, j, slot):
        k0 = pl.multiple_of(j * tq, tq)
        s_sc[slot] = _dot(k_ref[0, 0, pl.ds(k0, tq), :], qt_ref[0, 0, :, qi * tq:(qi + 1) * tq])

    def consume(qi, j, slot, m_prev, l_prev, masked, prefetch):
        k0 = pl.multiple_of(j * tq, tq)
        ss = [s_sc[slot, r * sub:(r + 1) * sub, :] for r in range(nsub)]
        prefetch()
        if masked:
            qpos = qi * tq + lax.broadcasted_iota(jnp.int32, ss[0].shape, 1)
            kpos = k0 + lax.broadcasted_iota(jnp.int32, ss[0].shape, 0)
            ss = [jnp.where(kpos + r * sub <= qpos, s, -1e30) for r, s in enumerate(ss)]
        m_new = functools.reduce(jnp.maximum, [jnp.max(s, axis=0, keepdims=True) for s in ss], m_prev)
        alpha = jnp.exp(m_prev - m_new)
        ps = [jnp.exp(s - m_new) for s in ss]
        l_new = alpha * l_prev + functools.reduce(lambda a, b: a + b, [jnp.sum(p, axis=0, keepdims=True) for p in ps])
        pv = [_dot(vt_ref[0, 0, :, pl.ds(pl.multiple_of(k0 + r * sub, sub), sub)], ps[r].astype(BF16)) for r in range(nsub)]
        acc_sc[...] = alpha * acc_sc[...] + functools.reduce(lambda a, b: a + b, pv)
        return m_new, l_new

    scores(0, 0, 0)
    first = 0
    for qi in range(nq):
        acc_sc[...] = jnp.zeros(acc_sc.shape, F32)
        m = jnp.full((1, tq), -1e30, F32)
        l = jnp.zeros((1, tq), F32)

        def visible(j, carry, qi=qi, first=first):
            slot = lax.rem(j + first, 2)
            return consume(qi, j, slot, carry[0], carry[1], False, lambda: scores(qi, j + 1, 1 - slot))

        m, l = lax.fori_loop(0, qi, visible, (m, l))
        dslot = (qi + first) % 2
        if qi + 1 < nq:
            m, l = consume(qi, qi, dslot, m, l, True, lambda: scores(qi + 1, 0, 1 - dslot))
        else:
            m, l = consume(qi, qi, dslot, m, l, True, lambda: None)
        o_ref[0, qi * tq:(qi + 1) * tq, :] = (acc_sc[...] / l).T.astype(o_ref.dtype)
        first = 1 - dslot


def _mla_attention(qt, k, vt, tq=512, sub=256):
    batch, heads, dk, seq = qt.shape
    dv = vt.shape[2]
    return pl.pallas_call(
        functools.partial(_flash_kernel, tq=tq, sub=min(sub, tq)),
        grid=(batch, heads),
        in_specs=[pl.BlockSpec((1, 1, dk, seq), lambda b, h: (b, h, 0, 0)),
                  pl.BlockSpec((1, 1, seq, dk), lambda b, h: (b, h, 0, 0)),
                  pl.BlockSpec((1, 1, dv, seq), lambda b, h: (b, h, 0, 0))],
        out_specs=pl.BlockSpec((1, seq, dv), lambda b, h: (b, 0, h)),
        out_shape=jax.ShapeDtypeStruct((batch, seq, heads * dv), F32),
        scratch_shapes=[pltpu.VMEM((2, tq, tq), F32), pltpu.VMEM((dv, tq), F32)],
        compiler_params=_cparams(("parallel", "parallel"), VMEM_LIMIT),
        name="mla_flash",
    )(qt, k, vt)


def _rope_tables(positions):
    half = MLA_ROPE // 2
    inv_freq = 1.0 / (ROPE_THETA ** (jnp.arange(half, dtype=F32) * (2.0 / MLA_ROPE)))
    ang = positions.astype(F32).reshape(-1)[:, None] * inv_freq
    cos, sin = jnp.cos(ang), jnp.sin(ang)
    reps = LANES // MLA_ROPE
    return jnp.tile(jnp.concatenate([cos, cos], axis=1), (1, reps)), jnp.tile(jnp.concatenate([-sin, sin], axis=1), (1, reps))


G_LANE = 64
B_LANE = 68


def _gdn_pre_kernel(x_ref, prev_ref, misc_ref, cw_ref, alog_ref, dtb_ref,
                    q_ref, k_ref, v_ref, kt_ref, gb_ref, gbt_ref, grow_ref, *, heads, chunk):
    x = x_ref[0]
    ts = x.shape[0]
    prev = jnp.where(pl.program_id(1) > 0, prev_ref[0], 0.0)
    cw = cw_ref[...]
    row8 = lax.broadcasted_iota(jnp.int32, prev.shape, 0)
    acc = x * cw[GDN_CONV - 1:GDN_CONV, :]
    for d in range(1, GDN_CONV):
        xr = pltpu.roll(x, d, axis=0)
        head = jnp.where(row8 < d, pltpu.roll(prev, d, axis=0), xr[0:8])
        xs = jnp.concatenate([head, xr[8:]], axis=0)
        acc = acc + xs * cw[GDN_CONV - 1 - d:GDN_CONV - d, :]
    y = acc * jax.nn.sigmoid(acc)
    nqk = heads * GDN_DK

    def l2n(z):
        return z * lax.rsqrt(jnp.sum(z * z, axis=-1, keepdims=True) + 1e-6)

    for h in range(heads):
        q_ref[0, :, GDN_DK * h:GDN_DK * (h + 1)] = l2n(y[:, GDN_DK * h:GDN_DK * (h + 1)]).astype(q_ref.dtype)
    kn = jnp.concatenate([l2n(y[:, nqk + GDN_DK * h:nqk + GDN_DK * (h + 1)]) for h in range(heads)], axis=1)
    k_ref[0] = kn.astype(k_ref.dtype)
    v_ref[0] = y[:, 2 * nqk:].astype(v_ref.dtype)
    knt = kn.T
    for n in range(ts // chunk):
        kt_ref[0, n] = knt[:, chunk * n:chunk * (n + 1)].astype(kt_ref.dtype)
    m = misc_ref[0]
    lane = lax.broadcasted_iota(jnp.int32, m.shape, 1)
    g = -jnp.exp(alog_ref[...]) * jax.nn.softplus(m + dtb_ref[...])
    beta = jax.nn.sigmoid(m)
    gb = jnp.where((lane >= G_LANE) & (lane < G_LANE + heads), g,
                   jnp.where((lane >= B_LANE) & (lane < B_LANE + heads), beta, 0.0))
    gb_ref[0] = gb
    gbt = gb.T[G_LANE:G_LANE + 8, :]
    for n in range(ts // chunk):
        gbt_ref[0, n] = gbt[:, chunk * n:chunk * (n + 1)]
        g_rows = jnp.concatenate([gbt[h:h + 1, chunk * n:chunk * (n + 1)] for h in range(heads)], axis=1)
        grow_ref[0, n] = jnp.broadcast_to(g_rows, (8, heads * chunk))


def _gdn_pre(qkv, misc, w_conv, a_log, dt_bias, batch, seq, ts=256):
    width = qkv.shape[-1]
    heads = a_log.shape[0]
    chunk = GDN_CHUNK
    hd = heads * GDN_DK
    x3 = qkv.reshape(batch, seq, width)
    m3 = misc.reshape(batch, seq, LANES)
    alog_row = jnp.zeros((1, LANES), F32).at[0, G_LANE:G_LANE + heads].set(a_log.astype(F32))
    dtb_row = jnp.zeros((1, LANES), F32).at[0, G_LANE:G_LANE + heads].set(dt_bias.astype(F32))
    nck = ts // chunk
    tok = lambda b, s: (b, s, 0)
    return pl.pallas_call(
        functools.partial(_gdn_pre_kernel, heads=heads, chunk=chunk),
        grid=(batch, seq // ts),
        in_specs=[pl.BlockSpec((1, ts, width), tok),
                  pl.BlockSpec((1, 8, width), lambda b, s: (b, jnp.maximum(s * (ts // 8) - 1, 0), 0)),
                  pl.BlockSpec((1, ts, LANES), tok),
                  _const_spec(w_conv.shape), _const_spec((1, LANES)), _const_spec((1, LANES))],
        out_specs=(pl.BlockSpec((1, ts, hd), tok), pl.BlockSpec((1, ts, hd), tok), pl.BlockSpec((1, ts, width - 2 * hd), tok),
                   pl.BlockSpec((1, nck, hd, chunk), lambda b, s: (b, s, 0, 0)),
                   pl.BlockSpec((1, ts, LANES), tok),
                   pl.BlockSpec((1, nck, 8, chunk), lambda b, s: (b, s, 0, 0)),
                   pl.BlockSpec((1, nck, 8, heads * chunk), lambda b, s: (b, s, 0, 0))),
        out_shape=(jax.ShapeDtypeStruct((batch, seq, hd), BF16), jax.ShapeDtypeStruct((batch, seq, hd), BF16),
                   jax.ShapeDtypeStruct((batch, seq, width - 2 * hd), BF16),
                   jax.ShapeDtypeStruct((batch, seq // chunk, hd, chunk), BF16),
                   jax.ShapeDtypeStruct((batch, seq, LANES), F32),
                   jax.ShapeDtypeStruct((batch, seq // chunk, 8, chunk), F32),
                   jax.ShapeDtypeStruct((batch, seq // chunk, 8, heads * chunk), F32)),
        compiler_params=_cparams(("parallel", "parallel"), VMEM_LIMIT),
        name="gdn_pre",
    )(x3, x3, m3, w_conv.astype(F32), alog_row, dtb_row)


def _hdot(a, b):
    return jnp.dot(a, b, preferred_element_type=F32, precision=HIGHEST)


def _gdn_local_kernel(q_ref, k_ref, v_ref, kt_ref, gb_ref, gbt_ref, grow_ref,
                      u_ref, w_ref, qd_ref, a_ref, kend_ref, egl_ref, *, heads, chunk, n_chunks):
    c = chunk
    hc = heads * c
    hd = heads * GDN_DK
    iota = lambda shape, ax: lax.broadcasted_iota(jnp.int32, shape, ax)
    ri, li = iota((c, hc), 0), iota((c, hc), 1)
    lj, lh = li % c, li // c
    tri_cat = ri >= lj
    strict_cat = ri > lj
    eye_cat = (ri == lj).astype(F32)
    r2, l2 = iota((hc, hc), 0), iota((hc, hc), 1)
    same_blk = (r2 // c) == (l2 // c)
    tri_bd = jnp.logical_and(same_blk, (r2 % c) <= (l2 % c)).astype(F32)
    head_rows = (iota((hc, hd), 0) // c) == (iota((hc, hd), 1) // GDN_DK)
    r1, c1 = iota((c, c), 0), iota((c, c), 1)
    tri_f = (r1 >= c1).astype(F32)
    tri_t = (r1 <= c1).astype(F32)
    nt = (((1,), (1,)), ((), ()))

    def bdiag(x):
        return jnp.where(same_blk, jnp.concatenate([x] * heads, axis=0), 0.0)

    def bdiag_wide(x):
        return jnp.where(head_rows, jnp.concatenate([x] * heads, axis=0), 0.0)

    def per_head_cols(cols, width):
        return jnp.concatenate([jnp.broadcast_to(col, (c, width)) for col in cols], axis=1)

    st = []
    for n in range(n_chunks):
        r0 = n * c
        gbc = gb_ref[0, r0:r0 + c, :]
        gcc = _hdot(tri_f, gbc)
        gc_cols = [gcc[:, G_LANE + h:G_LANE + h + 1] for h in range(heads)]
        gc_c = jnp.broadcast_to(gc_cols[0], (c, hc))
        for h in range(1, heads):
            gc_c = jnp.where(lh == h, jnp.broadcast_to(gc_cols[h], (c, hc)), gc_c)
        gc_r = _hdot(grow_ref[0, n], tri_bd)[0:1, :]
        decay = jnp.where(tri_cat, jnp.exp(jnp.where(tri_cat, gc_c - gc_r, 0.0)), 0.0)
        beta_w = per_head_cols([gbc[:, B_LANE + h:B_LANE + h + 1] for h in range(heads)], GDN_DK)
        eg_w = per_head_cols([jnp.exp(col) for col in gc_cols], GDN_DK)
        q = q_ref[0, r0:r0 + c, :].astype(F32) * (GDN_DK ** -0.5)
        k = k_ref[0, r0:r0 + c, :].astype(F32)
        v = v_ref[0, r0:r0 + c, :].astype(F32)
        kb = k * beta_w
        k_bd = bdiag_wide(k).astype(BF16)
        kk = lax.dot_general(kb.astype(BF16), k_bd, nt, preferred_element_type=F32)
        qk = lax.dot_general(q.astype(BF16), k_bd, nt, preferred_element_type=F32)
        lmat = jnp.where(strict_cat, kk * decay, 0.0)
        qd_ref[0, r0:r0 + c, :] = (q * eg_w).astype(qd_ref.dtype)
        a_ref[0, r0:r0 + c, :] = jnp.where(tri_cat, qk * decay, 0.0).astype(a_ref.dtype)
        gcr = _hdot(gbt_ref[0, n], tri_t)
        g_last = [gcr[h:h + 1, c - 1:c] for h in range(heads)]
        f = jnp.concatenate([jnp.broadcast_to(jnp.exp(g_last[h] - gcr[h:h + 1, :]), (GDN_DK, c)) for h in range(heads)], axis=0)
        kend_ref[0, n] = (kt_ref[0, n].astype(F32) * f).astype(kend_ref.dtype)
        egl_ref[0, n] = jnp.concatenate([jnp.broadcast_to(jnp.exp(g_last[h]), (1, LANES)) for h in range(heads)]
                                        + [jnp.zeros((8 - heads, LANES), F32)], axis=0)
        st.append(dict(p=eye_cat - lmat, sq=lmat, vb=v * beta_w, kbe=kb * eg_w))
    kpow = 2
    while kpow < c:
        for d in st:
            d["sq"] = _dot(d["sq"].astype(BF16), bdiag(d["sq"]).astype(BF16))
        for d in st:
            d["p"] = d["p"] + _dot(d["p"].astype(BF16), bdiag(d["sq"]).astype(BF16))
        kpow *= 2
    for n, d in enumerate(st):
        r0 = n * c
        tmat = d["p"].astype(BF16)
        u_ref[0, r0:r0 + c, :] = _dot(tmat, bdiag_wide(d["vb"]).astype(BF16))
        w_ref[0, r0:r0 + c, :] = _dot(tmat, bdiag_wide(d["kbe"]).astype(BF16)).astype(w_ref.dtype)


def _gdn_local(q, k, v, kt, gb, gbt, grow, cb=8):
    batch, seq, hd = q.shape
    heads = hd // GDN_DK
    chunk = GDN_CHUNK
    n_all = seq // chunk
    ts = cb * chunk
    tok = lambda b, s: (b, s, 0)
    ck = lambda b, s: (b, s, 0, 0)
    return pl.pallas_call(
        functools.partial(_gdn_local_kernel, heads=heads, chunk=chunk, n_chunks=cb),
        grid=(batch, n_all // cb),
        in_specs=[pl.BlockSpec((1, ts, hd), tok), pl.BlockSpec((1, ts, hd), tok), pl.BlockSpec((1, ts, hd), tok),
                  pl.BlockSpec((1, cb, hd, chunk), ck), pl.BlockSpec((1, ts, LANES), tok), pl.BlockSpec((1, cb, 8, chunk), ck),
                  pl.BlockSpec((1, cb, 8, heads * chunk), ck)],
        out_specs=(pl.BlockSpec((1, ts, hd), tok), pl.BlockSpec((1, ts, hd), tok), pl.BlockSpec((1, ts, hd), tok),
                   pl.BlockSpec((1, ts, heads * chunk), tok), pl.BlockSpec((1, cb, hd, chunk), ck),
                   pl.BlockSpec((1, cb, 8, LANES), ck)),
        out_shape=(jax.ShapeDtypeStruct((batch, seq, hd), F32), jax.ShapeDtypeStruct((batch, seq, hd), BF16),
                   jax.ShapeDtypeStruct((batch, seq, hd), BF16), jax.ShapeDtypeStruct((batch, seq, heads * chunk), BF16),
                   jax.ShapeDtypeStruct((batch, n_all, hd, chunk), BF16), jax.ShapeDtypeStruct((batch, n_all, 8, LANES), F32)),
        compiler_params=_cparams(("parallel", "parallel"), VMEM_LIMIT),
        name="gdn_local",
    )(q, k, v, kt, gb, gbt, grow)


def _gdn_scan_kernel(u_ref, w_ref, qd_ref, a_ref, kend_ref, egl_ref, gz_ref, gn_ref, o_ref, st_ref, *, heads, chunk, n_chunks):
    c = chunk
    nb = u_ref.shape[0]

    @pl.when(pl.program_id(1) == 0)
    def _():
        st_ref[...] = jnp.zeros(st_ref.shape, F32)

    gn = gn_ref[...]
    lanes = [(b, h) for b in range(nb) for h in range(heads)]
    state = {bh: st_ref[bh[0], bh[1]] for bh in lanes}
    col = lambda h: slice(GDN_DV * h, GDN_DV * (h + 1))
    for n in range(n_chunks):
        r0 = n * c
        sb = {bh: state[bh].astype(BF16) for bh in lanes}
        v_new = {(b, h): u_ref[b, r0:r0 + c, col(h)] - _dot(w_ref[b, r0:r0 + c, col(h)], sb[(b, h)]) for b, h in lanes}
        vb = {bh: v_new[bh].astype(BF16) for bh in lanes}
        state = {(b, h): state[(b, h)] * egl_ref[b, n][h:h + 1, :] + _dot(kend_ref[b, n, col(h), :], vb[(b, h)]) for b, h in lanes}
        for b, h in lanes:
            o = _dot(qd_ref[b, r0:r0 + c, col(h)], sb[(b, h)]) + _dot(a_ref[b, r0:r0 + c, c * h:c * (h + 1)], vb[(b, h)])
            z = gz_ref[b, r0:r0 + c, col(h)].astype(F32)
            o_ref[b, r0:r0 + c, col(h)] = (_rms_rows(o, gn) * (z * jax.nn.sigmoid(z))).astype(o_ref.dtype)
    for b, h in lanes:
        st_ref[b, h] = state[(b, h)]


def _gdn_scan(u, w, qd, a, kend, egl, gz, g_out, cs=4):
    batch, seq, hd = u.shape
    nb = max(n for n in (4, 2, 1) if batch % n == 0)
    heads = hd // GDN_DV
    chunk = GDN_CHUNK
    n_all = seq // chunk
    ts = cs * chunk
    tok = lambda b, s: (b, s, 0)
    ck = lambda b, s: (b, s, 0, 0)
    return pl.pallas_call(
        functools.partial(_gdn_scan_kernel, heads=heads, chunk=chunk, n_chunks=cs),
        grid=(batch // nb, n_all // cs),
        in_specs=[pl.BlockSpec((nb, ts, hd), tok), pl.BlockSpec((nb, ts, hd), tok), pl.BlockSpec((nb, ts, hd), tok),
                  pl.BlockSpec((nb, ts, heads * chunk), tok), pl.BlockSpec((nb, cs, hd, chunk), ck),
                  pl.BlockSpec((nb, cs, 8, LANES), ck), pl.BlockSpec((nb, ts, hd), tok), _const_spec((1, GDN_DV))],
        out_specs=pl.BlockSpec((nb, ts, hd), tok),
        out_shape=jax.ShapeDtypeStruct((batch, seq, hd), BF16),
        scratch_shapes=[pltpu.VMEM((nb, heads, GDN_DK, GDN_DV), F32)],
        compiler_params=_cparams(("parallel", "arbitrary"), VMEM_LIMIT),
        name="gdn_scan",
    )(u, w, qd, a, kend, egl, gz.reshape(batch, seq, hd), g_out.reshape(1, GDN_DV).astype(F32))


def _mix_out_kernel(s5_ref, mla_ref, gdn_ref, x_ref, w_ref, mg_ref, g_ref, b_ref, o_ref, *, alpha, parts):
    w5 = s5_ref.shape[1]
    wm = mla_ref.shape[1]
    pm = x_ref.shape[0] // parts
    rows = lambda r: slice(r * pm, (r + 1) * pm)
    accs = []
    for r in range(parts):
        mla = _rms_rows(mla_ref[rows(r), :], mg_ref[...]).astype(BF16)
        accs.append(_dot(s5_ref[rows(r), :], w_ref[0:w5, :]) + _dot(mla, w_ref[w5:w5 + wm, :])
                    + _dot(gdn_ref[rows(r), :], w_ref[w5 + wm:, :]))
    for r in range(parts):
        o_ref[rows(r), :] = _layernorm_rows(alpha * x_ref[rows(r), :] + accs[r], g_ref[...], b_ref[...])


def _mix_out(y_s5, o_mla, y_gdn, x2d, w_out, mla_gain, ln_g, ln_b, alpha, tm=512, parts=2):
    t, d = x2d.shape
    row = lambda i: (i, 0)
    return pl.pallas_call(
        functools.partial(_mix_out_kernel, alpha=alpha, parts=parts),
        grid=(t // tm,),
        in_specs=[pl.BlockSpec((tm, y_s5.shape[1]), row), pl.BlockSpec((tm, o_mla.shape[1]), row),
                  pl.BlockSpec((tm, y_gdn.shape[1]), row), pl.BlockSpec((tm, d), row),
                  _const_spec(w_out.shape), _const_spec((1, o_mla.shape[1])), _const_spec((1, d)), _const_spec((1, d))],
        out_specs=pl.BlockSpec((tm, d), row),
        out_shape=jax.ShapeDtypeStruct((t, d), F32),
        compiler_params=_cparams(("parallel",), VMEM_LIMIT),
        name="mix_out",
    )(y_s5, o_mla, y_gdn, x2d, w_out.astype(BF16), mla_gain.reshape(1, -1).astype(F32),
      ln_g.reshape(1, d).astype(F32), ln_b.reshape(1, d).astype(F32))


def _matmul_kernel(x_ref, w_ref, o_ref):
    o_ref[...] = _dot(x_ref[...].astype(BF16), w_ref[...]).astype(o_ref.dtype)


def _matmul(x, w, tm=256, tn=512):
    m, k = x.shape
    n = w.shape[1]
    return pl.pallas_call(
        _matmul_kernel,
        grid=(m // tm, n // tn),
        in_specs=[pl.BlockSpec((tm, k), lambda i, j: (i, 0)), pl.BlockSpec((k, tn), lambda i, j: (0, j))],
        out_specs=pl.BlockSpec((tm, tn), lambda i, j: (i, j)),
        out_shape=jax.ShapeDtypeStruct((m, n), F32),
        compiler_params=_cparams(("parallel", "parallel")),
        name="mem_kv_proj",
    )(x, w.astype(BF16))


def _xattn_kernel(x_ref, wq_ref, kt_ref, v_ref, wo_ref, g_ref, b_ref, wr_ref, o_ref, lg_ref, *, heads, alpha, parts):
    tm = x_ref.shape[0]
    pm = tm // parts
    nr = lg_ref.shape[0]
    nt = (((1,), (1,)), ((), ()))
    cols = lambda h: slice(XA_DH * h, XA_DH * (h + 1))
    xs = [x_ref[r * pm:(r + 1) * pm, :] for r in range(parts)]
    qs = [(_dot(x.astype(BF16), wq_ref[...]) * (XA_DH ** -0.5)).astype(BF16) for x in xs]
    ss = [[_dot(q[:, cols(h)], kt_ref[0, cols(h), :]) for h in range(heads)] for q in qs]
    ps = []
    for part in ss:
        row = []
        for s in part:
            e = jnp.exp(s - jnp.max(s, axis=-1, keepdims=True))
            row.append((e / jnp.sum(e, axis=-1, keepdims=True)).astype(BF16))
        ps.append(row)
    os = [jnp.concatenate([_dot(p[h], v_ref[0, :, cols(h)]) for h in range(heads)], axis=1).astype(BF16) for p in ps]
    ys = [_layernorm_rows(alpha * xs[r] + _dot(os[r], wo_ref[...]), g_ref[...], b_ref[...]) for r in range(parts)]
    w_hi_lo = wr_ref[...]
    for r in range(parts):
        o_ref[r * pm:(r + 1) * pm, :] = ys[r]
        y_hi = ys[r].astype(BF16)
        y_lo = (ys[r] - y_hi.astype(F32)).astype(BF16)
        p1 = lax.dot_general(w_hi_lo, y_hi, nt, preferred_element_type=F32)
        p2 = lax.dot_general(w_hi_lo[:nr], y_lo, nt, preferred_element_type=F32)
        lg_ref[:, r * pm:(r + 1) * pm] = p1[:nr] + p1[nr:] + p2


def _xattn(x2d, kt, v, w_q, w_o, ln_g, ln_b, w_router_t, seq, alpha, tm=512, parts=2):
    t, d = x2d.shape
    width = w_q.shape[1]
    heads = width // XA_DH
    mlen = v.shape[1]
    per = seq // tm
    nr = w_router_t.shape[0]
    w_hi = w_router_t.astype(BF16)
    w_lo = (w_router_t - w_hi.astype(F32)).astype(BF16)
    row = lambda i: (i, 0)
    return pl.pallas_call(
        functools.partial(_xattn_kernel, heads=heads, alpha=alpha, parts=parts),
        grid=(t // tm,),
        in_specs=[pl.BlockSpec((tm, d), row), _const_spec((d, width)),
                  pl.BlockSpec((1, width, mlen), lambda i: (i // per, 0, 0)),
                  pl.BlockSpec((1, mlen, width), lambda i: (i // per, 0, 0)),
                  _const_spec((width, d)), _const_spec((1, d)), _const_spec((1, d)), _const_spec((2 * nr, d))],
        out_specs=(pl.BlockSpec((tm, d), row), pl.BlockSpec((nr, tm), lambda i: (0, i))),
        out_shape=(jax.ShapeDtypeStruct((t, d), F32), jax.ShapeDtypeStruct((nr, t), F32)),
        compiler_params=_cparams(("parallel",), VMEM_LIMIT),
        name="xattn",
    )(x2d, w_q.astype(BF16), kt, v, w_o.astype(BF16), ln_g.reshape(1, d).astype(F32), ln_b.reshape(1, d).astype(F32),
      jnp.concatenate([w_hi, w_lo], axis=0))


def _router_kernel(lg_ref, bias_ref, eid_ref, gate_ref):
    lg = lg_ref[...] + bias_ref[...]
    ng, ne = MOE_GROUPS, MOE_PER_GROUP
    grp = [lg[g:g + 1, :] for g in range(ng)]
    gmax = functools.reduce(jnp.maximum, grp)
    gexp = [jnp.exp(r - gmax) for r in grp]
    gsum = functools.reduce(lambda a, b: a + b, gexp)
    pg = [e / gsum for e in gexp]
    best, gsel = pg[0], jnp.zeros(pg[0].shape, jnp.int32)
    for g in range(1, ng):
        better = pg[g] > best
        gsel = jnp.where(better, g, gsel)
        best = jnp.where(better, pg[g], best)
    le = []
    for e in range(ne):
        r = lg[ng + e:ng + e + 1, :]
        for g in range(1, ng):
            r = jnp.where(gsel == g, lg[ng + g * ne + e:ng + g * ne + e + 1, :], r)
        le.append(r)
    emax = functools.reduce(jnp.maximum, le)
    eexp = [jnp.exp(r - emax) for r in le]
    esum = functools.reduce(lambda a, b: a + b, eexp)
    pe = [e / esum for e in eexp]
    sel, val = [], []
    for k in range(MOE_TOPK):
        bv, bi = None, None
        for e in range(ne):
            cand = pe[e]
            for prev in sel:
                cand = jnp.where(prev == e, -1.0, cand)
            if bv is None:
                bv, bi = cand, jnp.zeros(cand.shape, jnp.int32)
            else:
                better = cand > bv
                bi = jnp.where(better, e, bi)
                bv = jnp.where(better, cand, bv)
        sel.append(bi)
        val.append(bv)
    tot = functools.reduce(lambda a, b: a + b, val)
    zero_i = jnp.zeros((8 - MOE_TOPK,) + sel[0].shape[1:], jnp.int32)
    zero_f = jnp.zeros((8 - MOE_TOPK,) + sel[0].shape[1:], F32)
    eid_ref[...] = jnp.concatenate([gsel * ne + s for s in sel] + [zero_i], axis=0)
    gate_ref[...] = jnp.concatenate([best * v / tot for v in val] + [zero_f], axis=0)


def _router(lg_t, bias_col, tn=2048):
    nr, t = lg_t.shape
    tn = min(tn, t)
    return pl.pallas_call(
        _router_kernel,
        grid=(t // tn,),
        in_specs=[pl.BlockSpec((nr, tn), lambda i: (0, i)), _const_spec((nr, 1))],
        out_specs=(pl.BlockSpec((8, tn), lambda i: (0, i)), pl.BlockSpec((8, tn), lambda i: (0, i))),
        out_shape=(jax.ShapeDtypeStruct((8, t), jnp.int32), jax.ShapeDtypeStruct((8, t), F32)),
        compiler_params=_cparams(("parallel",)),
        name="router",
    )(lg_t, bias_col)


def _start_row_gather(idx_ref, src_hbm, buf, sem):
    for r in range(buf.shape[0]):
        pltpu.make_async_copy(src_hbm.at[pl.ds(idx_ref[r], 1)], buf.at[pl.ds(r, 1)], sem).start(priority=r % 2)


def _wait_row_gather(buf, sem):
    pltpu.make_async_copy(buf, buf, sem).wait()


def _expert_kernel(be_ref, nu_ref, idx0_ref, idxn_ref, x_hbm, wgu_ref, wd_ref, y_ref, xbuf, sem, wgu_sc, wd_sc, *, ff):
    i = pl.program_id(0)
    n_used = nu_ref[0]
    slot = lax.rem(i, 2)
    changed = jnp.logical_or(i == 0, be_ref[i] != be_ref[jnp.maximum(i - 1, 0)])

    @pl.when(changed)
    def _():
        wgu_sc[...] = wgu_ref[0, 0].astype(BF16)
        wd_sc[...] = wd_ref[0, 0].astype(BF16)

    @pl.when(i == 0)
    def _():
        _start_row_gather(idx0_ref, x_hbm, xbuf.at[0], sem.at[0])

    @pl.when(i < n_used)
    def _():
        _wait_row_gather(xbuf.at[slot], sem.at[slot])
        x = xbuf[slot].astype(BF16)
        _start_row_gather(idxn_ref, x_hbm, xbuf.at[1 - slot], sem.at[1 - slot])
        gu = _dot(x, wgu_sc[...])
        gate = gu[:, :ff]
        h = gate * jax.nn.sigmoid(gate) * gu[:, ff:]
        y_ref[...] = _dot(h.astype(BF16), wd_sc[...])

    @pl.when(i == n_used)
    def _():
        _wait_row_gather(xbuf.at[slot], sem.at[slot])

    @pl.when(i >= n_used)
    def _():
        y_ref[...] = jnp.zeros(y_ref.shape, y_ref.dtype)


def _experts(x2d, row_tok, blk_exp, n_used, w_gate_up, w_down, layer, rb):
    rows = row_tok.shape[0]
    d = x2d.shape[1]
    ff = w_down.shape[2]
    nblk = rows // rb
    grid_spec = pltpu.PrefetchScalarGridSpec(
        num_scalar_prefetch=2,
        grid=(nblk,),
        in_specs=[pl.BlockSpec((rb,), lambda i, be, nu: (0,), memory_space=pltpu.SMEM),
                  pl.BlockSpec((rb,), lambda i, be, nu: (jnp.minimum(i + 1, nblk - 1),), memory_space=pltpu.SMEM),
                  pl.BlockSpec(memory_space=pl.ANY),
                  pl.BlockSpec((1, 1, d, 2 * ff), lambda i, be, nu: (layer, be[i], 0, 0)),
                  pl.BlockSpec((1, 1, ff, d), lambda i, be, nu: (layer, be[i], 0, 0))],
        out_specs=pl.BlockSpec((rb, d), lambda i, be, nu: (i, 0)),
        scratch_shapes=[pltpu.VMEM((2, rb, d), x2d.dtype), pltpu.SemaphoreType.DMA((2,)),
                        pltpu.VMEM((d, 2 * ff), BF16), pltpu.VMEM((ff, d), BF16)],
    )
    return pl.pallas_call(
        functools.partial(_expert_kernel, ff=ff),
        grid_spec=grid_spec,
        out_shape=jax.ShapeDtypeStruct((rows, d), F32),
        compiler_params=_cparams(("arbitrary",), VMEM_LIMIT),
        name="experts",
    )(blk_exp, n_used, row_tok, row_tok, x2d, w_gate_up, w_down)


def _moe_plan(eid, n_exp, rb):
    t, topk = eid.shape
    m = t * topk
    flat_e = eid.reshape(m)
    onehot = (flat_e[:, None] == jnp.arange(n_exp, dtype=jnp.int32)[None, :]).astype(jnp.int32)
    csum = jnp.cumsum(onehot, axis=0)
    counts = csum[-1]
    pcounts = (counts + rb - 1) // rb * rb
    pends = jnp.cumsum(pcounts)
    pstarts = pends - pcounts
    dest = jnp.sum(onehot * (pstarts[None, :] + csum - 1), axis=1)
    rows = m + n_exp * rb
    row_tok = (jnp.arange(rows, dtype=jnp.int32) % t).at[dest].set(jnp.arange(m, dtype=jnp.int32) // topk)
    nblk = rows // rb
    blk_start = jnp.arange(nblk, dtype=jnp.int32) * rb
    blk_exp = jnp.minimum(jnp.sum((pends[None, :] <= blk_start[:, None]).astype(jnp.int32), axis=1), n_exp - 1)
    n_used = (pends[-1] // rb).astype(jnp.int32).reshape(1)
    return dest.astype(jnp.int32), row_tok, blk_exp, n_used


def _moe_out_kernel(*refs, alpha):
    idx0 = refs[:MOE_TOPK]
    idxn = refs[MOE_TOPK:2 * MOE_TOPK]
    y_hbm, x_ref, gate_ref, g_ref, b_ref, o_ref, ybuf, sem = refs[2 * MOE_TOPK:]
    i = pl.program_id(0)
    slot = lax.rem(i, 2)

    @pl.when(i == 0)
    def _():
        for k in range(MOE_TOPK):
            _start_row_gather(idx0[k], y_hbm, ybuf.at[0, k], sem.at[0])

    _wait_row_gather(ybuf.at[slot], sem.at[slot])
    for k in range(MOE_TOPK):
        _start_row_gather(idxn[k], y_hbm, ybuf.at[1 - slot, k], sem.at[1 - slot])
    gate = gate_ref[...]
    ffn = gate[:, 0:1] * ybuf[slot, 0]
    for k in range(1, MOE_TOPK):
        ffn = ffn + gate[:, k:k + 1] * ybuf[slot, k]
    o_ref[...] = _layernorm_rows(alpha * x_ref[...] + ffn, g_ref[...], b_ref[...])

    @pl.when(i + 1 == pl.num_programs(0))
    def _():
        _wait_row_gather(ybuf.at[1 - slot], sem.at[1 - slot])


def _moe_out(yr, dest_k, x2d, gate, ln_g, ln_b, alpha, tm=256):
    t, d = x2d.shape
    row = lambda i: (i, 0)
    nt = t // tm
    first = [pl.BlockSpec((tm,), functools.partial(lambda k, i: (k * nt,), k), memory_space=pltpu.SMEM) for k in range(MOE_TOPK)]
    nxt = [pl.BlockSpec((tm,), functools.partial(lambda k, i: (k * nt + jnp.minimum(i + 1, nt - 1),), k), memory_space=pltpu.SMEM)
           for k in range(MOE_TOPK)]
    return pl.pallas_call(
        functools.partial(_moe_out_kernel, alpha=alpha),
        grid=(nt,),
        in_specs=first + nxt + [pl.BlockSpec(memory_space=pl.ANY), pl.BlockSpec((tm, d), row),
                                pl.BlockSpec((tm, gate.shape[1]), row), _const_spec((1, d)), _const_spec((1, d))],
        out_specs=pl.BlockSpec((tm, d), row),
        out_shape=jax.ShapeDtypeStruct((t, d), F32),
        scratch_shapes=[pltpu.VMEM((2, MOE_TOPK, tm, d), yr.dtype), pltpu.SemaphoreType.DMA((2,))],
        compiler_params=_cparams(("arbitrary",), VMEM_LIMIT),
        name="moe_out",
    )(*([dest_k] * (2 * MOE_TOPK)), yr, x2d, gate, ln_g.reshape(1, d).astype(F32), ln_b.reshape(1, d).astype(F32))


def _moe(x2, lg_t, b_group, b_expert, w_gate_up, w_down, layer, ln_g, ln_b, alpha):
    t, d = x2.shape
    n_exp = w_gate_up.shape[1]
    nr = lg_t.shape[0]
    bias = jnp.zeros((nr, 1), F32).at[:MOE_GROUPS + n_exp, 0].set(jnp.concatenate([b_group, b_expert]).astype(F32))
    eid_t, gate_t = _router(lg_t, bias)
    eid = eid_t[:MOE_TOPK].T
    gate = gate_t[:MOE_TOPK].T
    dest, row_tok, blk_exp, n_used = _moe_plan(eid, n_exp, MOE_ROW_BLOCK)
    yr = _experts(x2, row_tok, blk_exp, n_used, w_gate_up, w_down, layer, MOE_ROW_BLOCK)
    return _moe_out(yr, dest.reshape(t, MOE_TOPK).T.reshape(-1), x2, gate, ln_g, ln_b, alpha)


def kernel(x, mem, positions, w_in, s5_lambda_re, s5_lambda_im, s5_log_step, s5_b_re, s5_b_im, s5_c_re, s5_c_im, s5_d, s5_w_glu, s5_b_glu, s5_out_norm, mla_q_norm, mla_w_uq, mla_kv_norm, mla_w_ukv, mla_out_norm, gdn_conv, gdn_a_log, gdn_dt_bias, gdn_out_norm, w_out, ln1_g, ln1_b, xa_w_q, xa_w_k, xa_w_v, xa_w_o, ln2_g, ln2_b, moe_w_group, moe_b_group, moe_w_expert, moe_b_expert, moe_w_gate_up, moe_w_down, ln3_g, ln3_b):
    batch, seq, d = x.shape
    t = batch * seq
    depth = w_in.shape[0]
    alpha = (2 * depth) ** 0.25
    mlen = mem.shape[1]
    s5_w = s5_w_glu.shape[1]
    rank_q = mla_w_uq.shape[1]
    rank_kv = mla_w_ukv.shape[1]
    g_heads = gdn_a_log.shape[1]
    g_qk = g_heads * GDN_DK
    g_v = gdn_conv.shape[2] - 2 * g_qk
    o_kr = s5_w + rank_q + rank_kv
    o_gq = o_kr + MLA_ROPE
    o_gz = o_gq + 2 * g_qk + g_v
    o_ga = o_gz + g_v
    widths = (s5_w, rank_q, rank_kv, LANES, 2 * g_qk + g_v, g_v)
    assert G_LANE == MLA_ROPE and B_LANE == G_LANE + g_heads and o_ga + 2 * g_heads == w_in.shape[2]

    cos_t, sin_t = _rope_tables(positions)
    mem2 = mem.reshape(batch * mlen, d)
    xt = x.reshape(t, d)
    for l in range(depth):
        w = w_in[l]
        w_packed = jnp.concatenate(
            [w[:, :o_gq], w[:, o_ga:], jnp.zeros((d, LANES - MLA_ROPE - 2 * g_heads), w.dtype), w[:, o_gq:o_ga]], axis=1).astype(BF16)
        u, cq, ckv, misc, qkv, gz = _in_proj(xt, w_packed, widths, (BF16, BF16, BF16, F32, F32, BF16), S5_CHUNK)

        tables = _s5_tables(s5_lambda_re[l], s5_lambda_im[l], s5_log_step[l], s5_b_re[l], s5_b_im[l], s5_c_re[l],
                            s5_c_im[l], s5_d[l], S5_CHUNK, seq // S5_CHUNK)
        y_s5 = _s5_glu(_s5_scan(u, tables, batch, seq, S5_CHUNK), s5_w_glu[l], s5_b_glu[l], s5_out_norm[l], S5_CHUNK)

        qt, kk, vt = _mla_proj(cq, ckv, misc, cos_t, sin_t, mla_q_norm[l], mla_kv_norm[l], mla_w_uq[l], mla_w_ukv[l], batch, seq)
        o_mla = _mla_attention(qt, kk, vt).reshape(t, -1)

        gq, gk, gv, gkt, ggb, ggbt, ggrow = _gdn_pre(qkv, misc, gdn_conv[l], gdn_a_log[l], gdn_dt_bias[l], batch, seq)
        local = _gdn_local(gq, gk, gv, gkt, ggb, ggbt, ggrow)
        y_gdn = _gdn_scan(*local, gz, gdn_out_norm[l]).reshape(t, -1)

        x1 = _mix_out(y_s5, o_mla, y_gdn, xt, w_out[l], mla_out_norm[l], ln1_g[l], ln1_b[l], alpha)

        kv_mem = _matmul(mem2, jnp.concatenate([xa_w_k[l], xa_w_v[l]], axis=1))
        xa_w = xa_w_k.shape[2]
        kt_mem = kv_mem[:, :xa_w].reshape(batch, mlen, xa_w).transpose(0, 2, 1).astype(BF16)
        v_mem = kv_mem[:, xa_w:].reshape(batch, mlen, xa_w).astype(BF16)
        n_route = MOE_GROUPS + moe_w_expert.shape[2]
        w_router_t = jnp.pad(jnp.concatenate([moe_w_group[l], moe_w_expert[l]], axis=1).T.astype(F32),
                             ((0, (-n_route) % 8), (0, 0)))
        x2, lg_t = _xattn(x1, kt_mem, v_mem, xa_w_q[l], xa_w_o[l], ln2_g[l], ln2_b[l], w_router_t, seq, alpha)

        xt = _moe(x2, lg_t, moe_b_group[l], moe_b_expert[l], moe_w_gate_up, moe_w_down, l, ln3_g[l], ln3_b[l], alpha)
    return xt.reshape(batch, seq, d)
```

```python
import functools
import math

import jax
import jax.numpy as jnp
from jax import lax
from jax.experimental import pallas as pl
from jax.experimental.pallas import tpu as pltpu

F32 = jnp.float32
BF16 = jnp.bfloat16
HIGHEST = lax.Precision.HIGHEST

S5_CH = 16
S5_STATE = 64
S5_LAMBDA_RE_MAX = -1e-4
S5_CHUNK = 16
MLA_NOPE = 128
MLA_ROPE = 64
MLA_V = 128
ROPE_THETA = 10000.0
GDN_DK = 128
GDN_DV = 128
GDN_CONV = 4
GDN_CHUNK = 64
XA_DH = 128
MOE_GROUPS = 4
MOE_PER_GROUP = 8
MOE_TOPK = 2
MOE_ROW_BLOCK = 256

LANES = 128
VMEM_LIMIT = 56 * 1024 * 1024


def _cparams(sem, vmem=None, flags=None):
    return pltpu.CompilerParams(dimension_semantics=sem, vmem_limit_bytes=vmem, flags=flags)


def _const_spec(shape):
    nd = len(shape)
    return pl.BlockSpec(shape, lambda *_: (0,) * nd)


def _rms_rows(x, gain, eps=1e-6):
    return x * lax.rsqrt(jnp.mean(x * x, axis=-1, keepdims=True) + eps) * gain


def _layernorm_rows(x, g, b, eps=1e-5):
    mu = jnp.mean(x, axis=-1, keepdims=True)
    xc = x - mu
    var = jnp.mean(xc * xc, axis=-1, keepdims=True)
    return xc * lax.rsqrt(var + eps) * g + b


def _dot(a, b):
    return jnp.dot(a, b, preferred_element_type=F32)


def _in_proj_kernel(x_ref, w_ref, u_ref, cq_ref, ckv_ref, misc_ref, qkv_ref, gz_ref, u_sc, *, splits, chunk):
    xb = x_ref[...].astype(BF16)
    outs = (cq_ref, ckv_ref, misc_ref, qkv_ref, gz_ref)
    for o_ref, (lo, hi) in zip(outs, splits[1:]):
        o_ref[...] = _dot(xb, w_ref[:, lo:hi]).astype(o_ref.dtype)
    lo, hi = splits[0]
    u = _dot(xb, w_ref[:, lo:hi])
    sets, tm = u_sc.shape[0], u_sc.shape[1]
    for s in range(sets):
        u_sc[s] = u[:, LANES * s:LANES * (s + 1)]
    for s in range(sets):
        for j in range(chunk):
            u_ref[s, :, LANES * j:LANES * (j + 1)] = u_sc[s, pl.ds(j, tm // chunk, stride=chunk), :].astype(u_ref.dtype)


def _in_proj(x2d, w_packed, widths, dtypes, chunk, tm=256):
    t, d = x2d.shape
    splits, lo = [], 0
    for w in widths:
        splits.append((lo, lo + w))
        lo += w
    sets = widths[0] // LANES
    out_shape = (jax.ShapeDtypeStruct((sets, t // chunk, chunk * LANES), dtypes[0]),) + tuple(
        jax.ShapeDtypeStruct((t, w), dt) for w, dt in zip(widths[1:], dtypes[1:]))
    out_specs = (pl.BlockSpec((sets, tm // chunk, chunk * LANES), lambda i: (0, i, 0)),) + tuple(
        pl.BlockSpec((tm, w), lambda i: (i, 0)) for w in widths[1:])
    return pl.pallas_call(
        functools.partial(_in_proj_kernel, splits=tuple(splits), chunk=chunk),
        grid=(t // tm,),
        in_specs=[pl.BlockSpec((tm, d), lambda i: (i, 0)), _const_spec(w_packed.shape)],
        out_specs=out_specs,
        out_shape=out_shape,
        scratch_shapes=[pltpu.VMEM((sets, tm, LANES), F32)],
        compiler_params=_cparams(("parallel",), VMEM_LIMIT),
        name="in_proj",
    )(x2d, w_packed)


def _s5_tables(lam_re, lam_im, log_step, b_re, b_im, c_re, c_im, d_skip, chunk, n_chunks):
    g, p = lam_re.shape
    h = b_re.shape[-1]
    gs = LANES // h
    sets = g // gs
    lr = jnp.minimum(lam_re.astype(F32), S5_LAMBDA_RE_MAX)
    li = lam_im.astype(F32)
    dt = jnp.exp(log_step.astype(F32))[:, None]
    mag = jnp.exp(lr * dt)
    th = li * dt
    ab_re, ab_im = mag * jnp.cos(th), mag * jnp.sin(th)
    den = lr * lr + li * li
    nr, ni = ab_re - 1.0, ab_im
    fr = (nr * lr + ni * li) / den
    fi = (ni * lr - nr * li) / den
    br, bi = b_re.astype(F32), b_im.astype(F32)
    bb_re = fr[..., None] * br - fi[..., None] * bi
    bb_im = fr[..., None] * bi + fi[..., None] * br
    cr, ci = c_re.astype(F32), c_im.astype(F32)
    n = jnp.arange(chunk + 1, dtype=F32)[:, None, None]
    pmag = jnp.exp(n * (lr * dt)[None])
    pr, pi = pmag * jnp.cos(n * th[None]), pmag * jnp.sin(n * th[None])

    crt, cit = cr.transpose(0, 2, 1), ci.transpose(0, 2, 1)
    cb_re = crt[:, :, :, None] * bb_re[:, :, None, :] - cit[:, :, :, None] * bb_im[:, :, None, :]
    cb_im = crt[:, :, :, None] * bb_im[:, :, None, :] + cit[:, :, :, None] * bb_re[:, :, None, :]
    kk = jnp.sum(pr[:chunk, :, :, None, None] * cb_re[None] - pi[:chunk, :, :, None, None] * cb_im[None], axis=2)
    kk = kk.at[0].add(jnp.eye(h, dtype=F32)[None] * d_skip.astype(F32)[:, :, None])
    k_c = kk.reshape(chunk, sets, gs, h, h).transpose(1, 0, 4, 2, 3).reshape(sets, chunk, h, LANES)

    n_rev = jnp.arange(chunk - 1, -1, -1, dtype=F32)[:, None, None]
    rmag = jnp.exp(n_rev * (lr * dt)[None])
    pr_rev, pi_rev = rmag * jnp.cos(n_rev * th[None]), rmag * jnp.sin(n_rev * th[None])
    e_re = pr_rev[:, :, :, None] * bb_re[None] - pi_rev[:, :, :, None] * bb_im[None]
    e_im = pr_rev[:, :, :, None] * bb_im[None] + pi_rev[:, :, :, None] * bb_re[None]
    e_all = jnp.stack([e_re, e_im], axis=0).reshape(2, chunk, sets, gs, p, h)
    e_c = e_all.transpose(2, 1, 5, 0, 3, 4).reshape(sets, chunk * h, 2 * gs * p)

    pr1, pi1 = pr[1:], pi[1:]
    f_re = cr[None] * pr1[:, :, None, :] - ci[None] * pi1[:, :, None, :]
    f_im = cr[None] * pi1[:, :, None, :] + ci[None] * pr1[:, :, None, :]
    f_all = jnp.stack([f_re, -f_im], axis=0).reshape(2, chunk, sets, gs, h, p)
    f_c = f_all.transpose(2, 1, 4, 0, 3, 5).reshape(sets, chunk * h, 2 * gs * p)

    steps = max(1, int(math.ceil(math.log2(n_chunks))))
    ar, ai = pr[chunk].reshape(sets, gs * p), pi[chunk].reshape(sets, gs * p)
    a1, a2 = [], []
    for _ in range(steps):
        a1.append(jnp.concatenate([ar, ar], axis=-1))
        a2.append(jnp.concatenate([-ai, ai], axis=-1))
        ar, ai = ar * ar - ai * ai, 2.0 * ar * ai
    pad = (-steps) % 8
    a1 = jnp.pad(jnp.stack(a1, axis=1), ((0, 0), (0, pad), (0, 0)))
    a2 = jnp.pad(jnp.stack(a2, axis=1), ((0, 0), (0, pad), (0, 0)))
    return k_c, e_c, f_c, a1, a2


def _s5_kernel(u_ref, kc_ref, ec_ref, fc_ref, a1_ref, a2_ref, y_ref, csup_sc, e_sc, ft_sc, *, n_chunks, steps):
    chunk, h = kc_ref.shape[1], kc_ref.shape[2]
    gs = LANES // h

    @pl.when(pl.program_id(1) == 0)
    def _():
        rg = lax.broadcasted_iota(jnp.int32, (LANES, LANES), 0) // h
        own128 = rg == lax.broadcasted_iota(jnp.int32, (LANES, LANES), 1) // h
        width = e_sc.shape[1]
        rgw = lax.broadcasted_iota(jnp.int32, (LANES, width), 0) // h
        cgw = (lax.broadcasted_iota(jnp.int32, (LANES, width), 1) % (width // 2)) // (width // 2 // gs)
        own_w = rgw == cgw
        blocks = [jnp.where(own128, jnp.concatenate([kc_ref[0, t]] * gs, axis=0), 0.0).astype(BF16) for t in range(chunk)]
        zero = jnp.zeros((LANES, LANES), BF16)
        for sg in range(chunk // 2):
            below = blocks[2 * sg - 1] if sg > 0 else zero
            csup_sc[sg] = jnp.concatenate([jnp.concatenate([blocks[2 * sg], blocks[2 * sg + 1]], axis=1),
                                           jnp.concatenate([below, blocks[2 * sg]], axis=1)], axis=0)
        for j in range(chunk):
            rows = slice(LANES * j, LANES * (j + 1))
            e_sc[rows, :] = jnp.where(own_w, jnp.concatenate([ec_ref[0, h * j:h * (j + 1), :]] * gs, axis=0), 0.0).astype(BF16)
            ft_sc[rows, :] = jnp.where(own_w, jnp.concatenate([fc_ref[0, h * j:h * (j + 1), :]] * gs, axis=0), 0.0).astype(BF16)

    u = u_ref[0]
    s = _dot(u, e_sc[...])
    rows, width = s.shape
    c_idx = lax.broadcasted_iota(jnp.int32, (rows, width), 0) % n_chunks
    a1 = a1_ref[0]
    a2 = a2_ref[0]
    for k in range(steps):
        sh = 1 << k
        prev = jnp.where(c_idx >= sh, pltpu.roll(s, sh, axis=0), 0.0)
        s = s + a1[k:k + 1, :] * prev + a2[k:k + 1, :] * pltpu.roll(prev, width // 2, axis=1)
    s_in = jnp.where(c_idx >= 1, pltpu.roll(s, 1, axis=0), 0.0)
    y_state = lax.dot_general(s_in.astype(BF16), ft_sc[...], (((1,), (1,)), ((), ())), preferred_element_type=F32)
    sw = 2 * LANES
    for i in range(u.shape[1] // sw):
        acc = y_state[:, i * sw:(i + 1) * sw]
        for sg in range(i + 1):
            acc = acc + _dot(u[:, (i - sg) * sw:(i - sg + 1) * sw], csup_sc[sg])
        y_ref[0, :, i * sw:(i + 1) * sw] = acc


def _s5_scan(us, tables, batch, seq, chunk, batches_per_block=2):
    k_c, e_c, f_c, a1, a2 = tables
    sets = k_c.shape[0]
    n_chunks = seq // chunk
    steps = max(1, int(math.ceil(math.log2(n_chunks))))
    rows = batch * n_chunks
    rb = batches_per_block * n_chunks
    width = chunk * LANES
    sw = e_c.shape[2]
    const = lambda arr: pl.BlockSpec((1,) + arr.shape[1:], lambda i, j: (i,) + (0,) * (arr.ndim - 1))
    return pl.pallas_call(
        functools.partial(_s5_kernel, n_chunks=n_chunks, steps=steps),
        grid=(sets, rows // rb),
        in_specs=[pl.BlockSpec((1, rb, width), lambda i, j: (i, j, 0)),
                  const(k_c), const(e_c), const(f_c), const(a1), const(a2)],
        out_specs=pl.BlockSpec((1, rb, width), lambda i, j: (i, j, 0)),
        out_shape=jax.ShapeDtypeStruct((sets, rows, width), F32),
        scratch_shapes=[pltpu.VMEM((chunk // 2, 2 * LANES, 2 * LANES), BF16), pltpu.VMEM((width, sw), BF16),
                        pltpu.VMEM((width, sw), BF16)],
        compiler_params=_cparams(("parallel", "arbitrary"), VMEM_LIMIT),
        name="s5_scan",
    )(us, k_c, e_c, f_c, a1, a2)


def _s5_glu_kernel(y_ref, w_ref, b_ref, g_ref, o_ref, y_sc, *, chunk):
    sets, tm = y_sc.shape[0], y_sc.shape[1]
    for s in range(sets):
        for j in range(chunk):
            y_sc[s, pl.ds(j, tm // chunk, stride=chunk), :] = y_ref[s, :, LANES * j:LANES * (j + 1)]
    y = jax.nn.gelu(jnp.concatenate([y_sc[s] for s in range(sets)], axis=1))
    z = _dot(y.astype(BF16), w_ref[...]) + b_ref[...]
    y = y * jax.nn.sigmoid(z)
    o_ref[...] = _rms_rows(y, g_ref[...]).astype(o_ref.dtype)


def _s5_glu(ys, w_glu, b_glu, g_out, chunk, tm=1024):
    sets, rows, width = ys.shape
    t = rows * chunk
    w = sets * LANES
    tm = min(tm, t)
    return pl.pallas_call(
        functools.partial(_s5_glu_kernel, chunk=chunk),
        grid=(t // tm,),
        in_specs=[pl.BlockSpec((sets, tm // chunk, width), lambda i: (0, i, 0)), _const_spec((w, w)), _const_spec((1, w)),
                  _const_spec((1, w))],
        out_specs=pl.BlockSpec((tm, w), lambda i: (i, 0)),
        out_shape=jax.ShapeDtypeStruct((t, w), BF16),
        scratch_shapes=[pltpu.VMEM((sets, tm, LANES), F32)],
        compiler_params=_cparams(("parallel",), VMEM_LIMIT),
        name="s5_glu",
    )(ys, w_glu.astype(BF16), b_glu.reshape(1, w).astype(F32), g_out.reshape(1, w).astype(F32))


def _mla_proj_kernel(cq_ref, ckv_ref, misc_ref, cos_ref, sin_ref, qn_ref, kvn_ref, wuq_ref, wukv_ref,
                     qt_ref, k_ref, vt_ref, *, heads, scale):
    cq = _rms_rows(cq_ref[...].astype(F32), qn_ref[...]).astype(BF16)
    q = _dot(cq, wuq_ref[...]) * scale
    ckv = _rms_rows(ckv_ref[...].astype(F32), kvn_ref[...]).astype(BF16)
    kv = _dot(ckv, wukv_ref[...])
    cos = cos_ref[...]
    sin = sin_ref[...]
    lane = lax.broadcasted_iota(jnp.int32, cos.shape, 1)
    first_half = (lane % MLA_ROPE) < (MLA_ROPE // 2)

    def rope(x):
        partner = jnp.where(first_half, pltpu.roll(x, LANES - MLA_ROPE // 2, axis=1), pltpu.roll(x, MLA_ROPE // 2, axis=1))
        return x * cos + partner * sin

    kpe = rope(misc_ref[...])
    kpe_lo = jnp.where(lane < MLA_ROPE, kpe, 0.0)
    kpe_hi = pltpu.roll(kpe_lo, MLA_ROPE, axis=1)
    nope_w = heads * MLA_NOPE
    for pair in range(heads // 2):
        q_pe = rope(q[:, nope_w + LANES * pair:nope_w + LANES * (pair + 1)])
        for h in (2 * pair, 2 * pair + 1):
            qh = jnp.concatenate([q[:, MLA_NOPE * h:MLA_NOPE * (h + 1)], q_pe], axis=1)
            qt_ref[0, h] = qh.T.astype(BF16)
            kvw = MLA_NOPE + MLA_V
            kh = jnp.concatenate([kv[:, kvw * h:kvw * h + MLA_NOPE], kpe_lo if h % 2 == 0 else kpe_hi], axis=1)
            k_ref[0, h] = kh.astype(BF16)
            vt_ref[0, h] = kv[:, kvw * h + MLA_NOPE:kvw * (h + 1)].T.astype(BF16)


def _mla_proj(cq, ckv, misc, cos_t, sin_t, q_norm, kv_norm, w_uq, w_ukv, batch, seq, ts=256):
    t, rank = cq.shape
    heads = w_ukv.shape[1] // (MLA_NOPE + MLA_V)
    dq = MLA_NOPE + MLA_ROPE
    w3 = w_uq.reshape(rank, heads, dq)
    w_uq_p = jnp.concatenate([w3[:, :, :MLA_NOPE].reshape(rank, -1), w3[:, :, MLA_NOPE:].reshape(rank, -1)], axis=1).astype(BF16)
    per = seq // ts
    dk = MLA_NOPE + LANES
    return pl.pallas_call(
        functools.partial(_mla_proj_kernel, heads=heads, scale=dq ** -0.5),
        grid=(t // ts,),
        in_specs=[pl.BlockSpec((ts, rank), lambda i: (i, 0)), pl.BlockSpec((ts, rank), lambda i: (i, 0)),
                  pl.BlockSpec((ts, LANES), lambda i: (i, 0)), pl.BlockSpec((ts, LANES), lambda i: (i, 0)),
                  pl.BlockSpec((ts, LANES), lambda i: (i, 0)),
                  _const_spec((1, rank)), _const_spec((1, rank)), _const_spec(w_uq_p.shape), _const_spec(w_ukv.shape)],
        out_specs=(pl.BlockSpec((1, heads, dk, ts), lambda i: (i // per, 0, 0, i % per)),
                   pl.BlockSpec((1, heads, ts, dk), lambda i: (i // per, 0, i % per, 0)),
                   pl.BlockSpec((1, heads, MLA_V, ts), lambda i: (i // per, 0, 0, i % per))),
        out_shape=(jax.ShapeDtypeStruct((batch, heads, dk, seq), BF16),
                   jax.ShapeDtypeStruct((batch, heads, seq, dk), BF16),
                   jax.ShapeDtypeStruct((batch, heads, MLA_V, seq), BF16)),
        compiler_params=_cparams(("parallel",), VMEM_LIMIT),
        name="mla_proj",
    )(cq, ckv, misc, cos_t, sin_t, q_norm.reshape(1, rank).astype(F32), kv_norm.reshape(1, rank).astype(F32),
      w_uq_p, w_ukv.astype(BF16))


def _flash_kernel(qt_ref, k_ref, vt_ref, o_ref, s_sc, acc_sc, *, tq, sub):
    seq = k_ref.shape[2]
    nq = seq // tq
    nsub = tq // sub

    def scores(qi, j, slot):
        k0 = pl.multiple_of(j * tq, tq)
        s_sc[slot] = _dot(k_ref[0, 0, pl.ds(k0, tq), :], qt_ref[0, 0, :, qi * tq:(qi + 1) * tq])

    def consume(qi, j, slot, m_prev, l_prev, masked, prefetch):
        k0 = pl.multiple_of(j * tq, tq)
        ss = [s_sc[slot, r * sub:(r + 1) * sub, :] for r in range(nsub)]
        prefetch()
        if masked:
            qpos = qi * tq + lax.broadcasted_iota(jnp.int32, ss[0].shape, 1)
            kpos = k0 + lax.broadcasted_iota(jnp.int32, ss[0].shape, 0)
            ss = [jnp.where(kpos + r * sub <= qpos, s, -1e30) for r, s in enumerate(ss)]
        m_new = functools.reduce(jnp.maximum, [jnp.max(s, axis=0, keepdims=True) for s in ss], m_prev)
        alpha = jnp.exp(m_prev - m_new)
        ps = [jnp.exp(s - m_new) for s in ss]
        l_new = alpha * l_prev + functools.reduce(lambda a, b: a + b, [jnp.sum(p, axis=0, keepdims=True) for p in ps])
        pv = [_dot(vt_ref[0, 0, :, pl.ds(pl.multiple_of(k0 + r * sub, sub), sub)], ps[r].astype(BF16)) for r in range(nsub)]
        acc_sc[...] = alpha * acc_sc[...] + functools.reduce(lambda a, b: a + b, pv)
        return m_new, l_new

    scores(0, 0, 0)
    first = 0
    for qi in range(nq):
        acc_sc[...] = jnp.zeros(acc_sc.shape, F32)
        m = jnp.full((1, tq), -1e30, F32)
        l = jnp.zeros((1, tq), F32)

        def visible(j, carry, qi=qi, first=first):
            slot = lax.rem(j + first, 2)
            return consume(qi, j, slot, carry[0], carry[1], False, lambda: scores(qi, j + 1, 1 - slot))

        m, l = lax.fori_loop(0, qi, visible, (m, l))
        dslot = (qi + first) % 2
        if qi + 1 < nq:
            m, l = consume(qi, qi, dslot, m, l, True, lambda: scores(qi + 1, 0, 1 - dslot))
        else:
            m, l = consume(qi, qi, dslot, m, l, True, lambda: None)
        o_ref[0, qi * tq:(qi + 1) * tq, :] = (acc_sc[...] / l).T.astype(o_ref.dtype)
        first = 1 - dslot


def _mla_attention(qt, k, vt, tq=512, sub=256):
    batch, heads, dk, seq = qt.shape
    dv = vt.shape[2]
    return pl.pallas_call(
        functools.partial(_flash_kernel, tq=tq, sub=min(sub, tq)),
        grid=(batch, heads),
        in_specs=[pl.BlockSpec((1, 1, dk, seq), lambda b, h: (b, h, 0, 0)),
                  pl.BlockSpec((1, 1, seq, dk), lambda b, h: (b, h, 0, 0)),
                  pl.BlockSpec((1, 1, dv, seq), lambda b, h: (b, h, 0, 0))],
        out_specs=pl.BlockSpec((1, seq, dv), lambda b, h: (b, 0, h)),
        out_shape=jax.ShapeDtypeStruct((batch, seq, heads * dv), BF16),
        scratch_shapes=[pltpu.VMEM((2, tq, tq), F32), pltpu.VMEM((dv, tq), F32)],
        compiler_params=_cparams(("parallel", "parallel"), VMEM_LIMIT),
        name="mla_flash",
    )(qt, k, vt)


def _rope_tables(positions):
    half = MLA_ROPE // 2
    inv_freq = 1.0 / (ROPE_THETA ** (jnp.arange(half, dtype=F32) * (2.0 / MLA_ROPE)))
    ang = positions.astype(F32).reshape(-1)[:, None] * inv_freq
    cos, sin = jnp.cos(ang), jnp.sin(ang)
    reps = LANES // MLA_ROPE
    return jnp.tile(jnp.concatenate([cos, cos], axis=1), (1, reps)), jnp.tile(jnp.concatenate([-sin, sin], axis=1), (1, reps))


HALO = 16
G_LANE = 64
B_LANE = 68


def _gdn_pre_kernel(x_ref, prev_ref, misc_ref, cw_ref, alog_ref, dtb_ref,
                    q_ref, k_ref, v_ref, kt_ref, gb_ref, gbt_ref, grow_ref, *, heads, chunk):
    x = x_ref[0].astype(F32)
    ts = x.shape[0]
    halo = prev_ref[0].astype(F32)
    prev = jnp.where(pl.program_id(1) > 0, halo[halo.shape[0] - 8:], 0.0)
    cw = cw_ref[...]
    row8 = lax.broadcasted_iota(jnp.int32, prev.shape, 0)
    acc = x * cw[GDN_CONV - 1:GDN_CONV, :]
    for d in range(1, GDN_CONV):
        xr = pltpu.roll(x, d, axis=0)
        head = jnp.where(row8 < d, pltpu.roll(prev, d, axis=0), xr[0:8])
        xs = jnp.concatenate([head, xr[8:]], axis=0)
        acc = acc + xs * cw[GDN_CONV - 1 - d:GDN_CONV - d, :]
    y = acc * jax.nn.sigmoid(acc)
    nqk = heads * GDN_DK

    def l2n(z):
        return z * lax.rsqrt(jnp.sum(z * z, axis=-1, keepdims=True) + 1e-6)

    for h in range(heads):
        q_ref[0, :, GDN_DK * h:GDN_DK * (h + 1)] = l2n(y[:, GDN_DK * h:GDN_DK * (h + 1)]).astype(q_ref.dtype)
    kn = jnp.concatenate([l2n(y[:, nqk + GDN_DK * h:nqk + GDN_DK * (h + 1)]) for h in range(heads)], axis=1)
    k_ref[0] = kn.astype(k_ref.dtype)
    v_ref[0] = y[:, 2 * nqk:].astype(v_ref.dtype)
    knt = kn.T
    for n in range(ts // chunk):
        kt_ref[0, n] = knt[:, chunk * n:chunk * (n + 1)].astype(kt_ref.dtype)
    m = misc_ref[0]
    lane = lax.broadcasted_iota(jnp.int32, m.shape, 1)
    g = -jnp.exp(alog_ref[...]) * jax.nn.softplus(m + dtb_ref[...])
    beta = jax.nn.sigmoid(m)
    gb = jnp.where((lane >= G_LANE) & (lane < G_LANE + heads), g,
                   jnp.where((lane >= B_LANE) & (lane < B_LANE + heads), beta, 0.0))
    gb_ref[0] = gb
    gbt = gb.T[G_LANE:G_LANE + 8, :]
    for n in range(ts // chunk):
        gbt_ref[0, n] = gbt[:, chunk * n:chunk * (n + 1)]
        g_rows = jnp.concatenate([gbt[h:h + 1, chunk * n:chunk * (n + 1)] for h in range(heads)], axis=1)
        grow_ref[0, n] = jnp.broadcast_to(g_rows, (8, heads * chunk))


def _gdn_pre(qkv, misc, w_conv, a_log, dt_bias, batch, seq, ts=256):
    width = qkv.shape[-1]
    heads = a_log.shape[0]
    chunk = GDN_CHUNK
    hd = heads * GDN_DK
    x3 = qkv.reshape(batch, seq, width)
    m3 = misc.reshape(batch, seq, LANES)
    alog_row = jnp.zeros((1, LANES), F32).at[0, G_LANE:G_LANE + heads].set(a_log.astype(F32))
    dtb_row = jnp.zeros((1, LANES), F32).at[0, G_LANE:G_LANE + heads].set(dt_bias.astype(F32))
    nck = ts // chunk
    tok = lambda b, s: (b, s, 0)
    return pl.pallas_call(
        functools.partial(_gdn_pre_kernel, heads=heads, chunk=chunk),
        grid=(batch, seq // ts),
        in_specs=[pl.BlockSpec((1, ts, width), tok),
                  pl.BlockSpec((1, HALO, width), lambda b, s: (b, jnp.maximum(s * (ts // HALO) - 1, 0), 0)),
                  pl.BlockSpec((1, ts, LANES), tok),
                  _const_spec(w_conv.shape), _const_spec((1, LANES)), _const_spec((1, LANES))],
        out_specs=(pl.BlockSpec((1, ts, hd), tok), pl.BlockSpec((1, ts, hd), tok), pl.BlockSpec((1, ts, width - 2 * hd), tok),
                   pl.BlockSpec((1, nck, hd, chunk), lambda b, s: (b, s, 0, 0)),
                   pl.BlockSpec((1, ts, LANES), tok),
                   pl.BlockSpec((1, nck, 8, chunk), lambda b, s: (b, s, 0, 0)),
                   pl.BlockSpec((1, nck, 8, heads * chunk), lambda b, s: (b, s, 0, 0))),
        out_shape=(jax.ShapeDtypeStruct((batch, seq, hd), BF16), jax.ShapeDtypeStruct((batch, seq, hd), BF16),
                   jax.ShapeDtypeStruct((batch, seq, width - 2 * hd), BF16),
                   jax.ShapeDtypeStruct((batch, seq // chunk, hd, chunk), BF16),
                   jax.ShapeDtypeStruct((batch, seq, LANES), F32),
                   jax.ShapeDtypeStruct((batch, seq // chunk, 8, chunk), F32),
                   jax.ShapeDtypeStruct((batch, seq // chunk, 8, heads * chunk), F32)),
        compiler_params=_cparams(("parallel", "parallel"), VMEM_LIMIT),
        name="gdn_pre",
    )(x3, x3, m3, w_conv.astype(F32), alog_row, dtb_row)


def _hdot(a, b):
    return jnp.dot(a, b, preferred_element_type=F32, precision=HIGHEST)


def _gdn_local_kernel(q_ref, k_ref, v_ref, kt_ref, gb_ref, gbt_ref, grow_ref,
                      u_ref, w_ref, qd_ref, a_ref, kend_ref, egl_ref, *, heads, chunk, n_chunks):
    c = chunk
    hc = heads * c
    hd = heads * GDN_DK
    iota = lambda shape, ax: lax.broadcasted_iota(jnp.int32, shape, ax)
    ri, li = iota((c, hc), 0), iota((c, hc), 1)
    lj, lh = li % c, li // c
    tri_cat = ri >= lj
    strict_cat = ri > lj
    eye_cat = (ri == lj).astype(F32)
    r2, l2 = iota((hc, hc), 0), iota((hc, hc), 1)
    same_blk = (r2 // c) == (l2 // c)
    tri_bd = jnp.logical_and(same_blk, (r2 % c) <= (l2 % c)).astype(F32)
    head_rows = (iota((hc, hd), 0) // c) == (iota((hc, hd), 1) // GDN_DK)
    r1, c1 = iota((c, c), 0), iota((c, c), 1)
    tri_f = (r1 >= c1).astype(F32)
    tri_t = (r1 <= c1).astype(F32)
    nt = (((1,), (1,)), ((), ()))

    def bdiag(x):
        return jnp.where(same_blk, jnp.concatenate([x] * heads, axis=0), 0.0)

    def bdiag_wide(x):
        return jnp.where(head_rows, jnp.concatenate([x] * heads, axis=0), 0.0)

    def per_head_cols(cols, width):
        return jnp.concatenate([jnp.broadcast_to(col, (c, width)) for col in cols], axis=1)

    st = []
    for n in range(n_chunks):
        r0 = n * c
        gbc = gb_ref[0, r0:r0 + c, :]
        gcc = _hdot(tri_f, gbc)
        gc_cols = [gcc[:, G_LANE + h:G_LANE + h + 1] for h in range(heads)]
        gc_c = jnp.broadcast_to(gc_cols[0], (c, hc))
        for h in range(1, heads):
            gc_c = jnp.where(lh == h, jnp.broadcast_to(gc_cols[h], (c, hc)), gc_c)
        gc_r = _hdot(grow_ref[0, n], tri_bd)[0:1, :]
        decay = jnp.where(tri_cat, jnp.exp(jnp.where(tri_cat, gc_c - gc_r, 0.0)), 0.0)
        beta_w = per_head_cols([gbc[:, B_LANE + h:B_LANE + h + 1] for h in range(heads)], GDN_DK)
        eg_w = per_head_cols([jnp.exp(col) for col in gc_cols], GDN_DK)
        q = q_ref[0, r0:r0 + c, :].astype(F32) * (GDN_DK ** -0.5)
        k = k_ref[0, r0:r0 + c, :].astype(F32)
        v = v_ref[0, r0:r0 + c, :].astype(F32)
        kb = k * beta_w
        k_bd = bdiag_wide(k).astype(BF16)
        kk = lax.dot_general(kb.astype(BF16), k_bd, nt, preferred_element_type=F32)
        qk = lax.dot_general(q.astype(BF16), k_bd, nt, preferred_element_type=F32)
        lmat = jnp.where(strict_cat, kk * decay, 0.0)
        qd_ref[0, r0:r0 + c, :] = (q * eg_w).astype(qd_ref.dtype)
        a_ref[0, r0:r0 + c, :] = jnp.where(tri_cat, qk * decay, 0.0).astype(a_ref.dtype)
        gcr = _hdot(gbt_ref[0, n], tri_t)
        g_last = [gcr[h:h + 1, c - 1:c] for h in range(heads)]
        f = jnp.concatenate([jnp.broadcast_to(jnp.exp(g_last[h] - gcr[h:h + 1, :]), (GDN_DK, c)) for h in range(heads)], axis=0)
        kend_ref[0, n] = (kt_ref[0, n].astype(F32) * f).astype(kend_ref.dtype)
        egl_ref[0, n] = jnp.concatenate([jnp.broadcast_to(jnp.exp(g_last[h]), (1, LANES)) for h in range(heads)]
                                        + [jnp.zeros((8 - heads, LANES), F32)], axis=0)
        st.append(dict(p=eye_cat - lmat, sq=lmat, vb=v * beta_w, kbe=kb * eg_w))
    kpow = 2
    while kpow < c:
        for d in st:
            d["sq"] = _dot(d["sq"].astype(BF16), bdiag(d["sq"]).astype(BF16))
        for d in st:
            d["p"] = d["p"] + _dot(d["p"].astype(BF16), bdiag(d["sq"]).astype(BF16))
        kpow *= 2
    for n, d in enumerate(st):
        r0 = n * c
        tmat = d["p"].astype(BF16)
        u_ref[0, r0:r0 + c, :] = _dot(tmat, bdiag_wide(d["vb"]).astype(BF16))
        w_ref[0, r0:r0 + c, :] = _dot(tmat, bdiag_wide(d["kbe"]).astype(BF16)).astype(w_ref.dtype)


def _gdn_local(q, k, v, kt, gb, gbt, grow, cb=8):
    batch, seq, hd = q.shape
    heads = hd // GDN_DK
    chunk = GDN_CHUNK
    n_all = seq // chunk
    ts = cb * chunk
    tok = lambda b, s: (b, s, 0)
    ck = lambda b, s: (b, s, 0, 0)
    return pl.pallas_call(
        functools.partial(_gdn_local_kernel, heads=heads, chunk=chunk, n_chunks=cb),
        grid=(batch, n_all // cb),
        in_specs=[pl.BlockSpec((1, ts, hd), tok), pl.BlockSpec((1, ts, hd), tok), pl.BlockSpec((1, ts, hd), tok),
                  pl.BlockSpec((1, cb, hd, chunk), ck), pl.BlockSpec((1, ts, LANES), tok), pl.BlockSpec((1, cb, 8, chunk), ck),
                  pl.BlockSpec((1, cb, 8, heads * chunk), ck)],
        out_specs=(pl.BlockSpec((1, ts, hd), tok), pl.BlockSpec((1, ts, hd), tok), pl.BlockSpec((1, ts, hd), tok),
                   pl.BlockSpec((1, ts, heads * chunk), tok), pl.BlockSpec((1, cb, hd, chunk), ck),
                   pl.BlockSpec((1, cb, 8, LANES), ck)),
        out_shape=(jax.ShapeDtypeStruct((batch, seq, hd), F32), jax.ShapeDtypeStruct((batch, seq, hd), BF16),
                   jax.ShapeDtypeStruct((batch, seq, hd), BF16), jax.ShapeDtypeStruct((batch, seq, heads * chunk), BF16),
                   jax.ShapeDtypeStruct((batch, n_all, hd, chunk), BF16), jax.ShapeDtypeStruct((batch, n_all, 8, LANES), F32)),
        compiler_params=_cparams(("parallel", "parallel"), VMEM_LIMIT),
        name="gdn_local",
    )(q, k, v, kt, gb, gbt, grow)


def _gdn_scan_kernel(u_ref, w_ref, qd_ref, a_ref, kend_ref, egl_ref, gz_ref, gn_ref, o_ref, st_ref, *, heads, chunk, n_chunks):
    c = chunk
    nb = u_ref.shape[0]

    @pl.when(pl.program_id(1) == 0)
    def _():
        st_ref[...] = jnp.zeros(st_ref.shape, F32)

    gn = gn_ref[...]
    lanes = [(b, h) for b in range(nb) for h in range(heads)]
    state = {bh: st_ref[bh[0], bh[1]] for bh in lanes}
    col = lambda h: slice(GDN_DV * h, GDN_DV * (h + 1))
    for n in range(n_chunks):
        r0 = n * c
        sb = {bh: state[bh].astype(BF16) for bh in lanes}
        v_new = {(b, h): u_ref[b, r0:r0 + c, col(h)] - _dot(w_ref[b, r0:r0 + c, col(h)], sb[(b, h)]) for b, h in lanes}
        vb = {bh: v_new[bh].astype(BF16) for bh in lanes}
        state = {(b, h): state[(b, h)] * egl_ref[b, n][h:h + 1, :] + _dot(kend_ref[b, n, col(h), :], vb[(b, h)]) for b, h in lanes}
        for b, h in lanes:
            o = _dot(qd_ref[b, r0:r0 + c, col(h)], sb[(b, h)]) + _dot(a_ref[b, r0:r0 + c, c * h:c * (h + 1)], vb[(b, h)])
            z = gz_ref[b, r0:r0 + c, col(h)].astype(F32)
            o_ref[b, r0:r0 + c, col(h)] = (_rms_rows(o, gn) * (z * jax.nn.sigmoid(z))).astype(o_ref.dtype)
    for b, h in lanes:
        st_ref[b, h] = state[(b, h)]


def _gdn_scan(u, w, qd, a, kend, egl, gz, g_out, cs=4):
    batch, seq, hd = u.shape
    nb = max(n for n in (4, 2, 1) if batch % n == 0)
    heads = hd // GDN_DV
    chunk = GDN_CHUNK
    n_all = seq // chunk
    ts = cs * chunk
    tok = lambda b, s: (b, s, 0)
    ck = lambda b, s: (b, s, 0, 0)
    return pl.pallas_call(
        functools.partial(_gdn_scan_kernel, heads=heads, chunk=chunk, n_chunks=cs),
        grid=(batch // nb, n_all // cs),
        in_specs=[pl.BlockSpec((nb, ts, hd), tok), pl.BlockSpec((nb, ts, hd), tok), pl.BlockSpec((nb, ts, hd), tok),
                  pl.BlockSpec((nb, ts, heads * chunk), tok), pl.BlockSpec((nb, cs, hd, chunk), ck),
                  pl.BlockSpec((nb, cs, 8, LANES), ck), pl.BlockSpec((nb, ts, hd), tok), _const_spec((1, GDN_DV))],
        out_specs=pl.BlockSpec((nb, ts, hd), tok),
        out_shape=jax.ShapeDtypeStruct((batch, seq, hd), BF16),
        scratch_shapes=[pltpu.VMEM((nb, heads, GDN_DK, GDN_DV), F32)],
        compiler_params=_cparams(("parallel", "arbitrary"), VMEM_LIMIT),
        name="gdn_scan",
    )(u, w, qd, a, kend, egl, gz.reshape(batch, seq, hd), g_out.reshape(1, GDN_DV).astype(F32))


def _mix_out_kernel(s5_ref, mla_ref, gdn_ref, x_ref, w_ref, mg_ref, g_ref, b_ref, o_ref, *, alpha, parts):
    w5 = s5_ref.shape[1]
    wm = mla_ref.shape[1]
    pm = x_ref.shape[0] // parts
    rows = lambda r: slice(r * pm, (r + 1) * pm)
    accs = []
    for r in range(parts):
        mla = _rms_rows(mla_ref[rows(r), :].astype(F32), mg_ref[...]).astype(BF16)
        accs.append(_dot(s5_ref[rows(r), :], w_ref[0:w5, :]) + _dot(mla, w_ref[w5:w5 + wm, :])
                    + _dot(gdn_ref[rows(r), :], w_ref[w5 + wm:, :]))
    for r in range(parts):
        o_ref[rows(r), :] = _layernorm_rows(alpha * x_ref[rows(r), :] + accs[r], g_ref[...], b_ref[...])


def _mix_out(y_s5, o_mla, y_gdn, x2d, w_out, mla_gain, ln_g, ln_b, alpha, tm=512, parts=2):
    t, d = x2d.shape
    row = lambda i: (i, 0)
    return pl.pallas_call(
        functools.partial(_mix_out_kernel, alpha=alpha, parts=parts),
        grid=(t // tm,),
        in_specs=[pl.BlockSpec((tm, y_s5.shape[1]), row), pl.BlockSpec((tm, o_mla.shape[1]), row),
                  pl.BlockSpec((tm, y_gdn.shape[1]), row), pl.BlockSpec((tm, d), row),
                  _const_spec(w_out.shape), _const_spec((1, o_mla.shape[1])), _const_spec((1, d)), _const_spec((1, d))],
        out_specs=pl.BlockSpec((tm, d), row),
        out_shape=jax.ShapeDtypeStruct((t, d), F32),
        compiler_params=_cparams(("parallel",), VMEM_LIMIT),
        name="mix_out",
    )(y_s5, o_mla, y_gdn, x2d, w_out.astype(BF16), mla_gain.reshape(1, -1).astype(F32),
      ln_g.reshape(1, d).astype(F32), ln_b.reshape(1, d).astype(F32))


def _matmul_kernel(x_ref, w_ref, o_ref):
    o_ref[...] = _dot(x_ref[...].astype(BF16), w_ref[...]).astype(o_ref.dtype)


def _matmul(x, w, tm=256, tn=512):
    m, k = x.shape
    n = w.shape[1]
    return pl.pallas_call(
        _matmul_kernel,
        grid=(m // tm, n // tn),
        in_specs=[pl.BlockSpec((tm, k), lambda i, j: (i, 0)), pl.BlockSpec((k, tn), lambda i, j: (0, j))],
        out_specs=pl.BlockSpec((tm, tn), lambda i, j: (i, j)),
        out_shape=jax.ShapeDtypeStruct((m, n), F32),
        compiler_params=_cparams(("parallel", "parallel")),
        name="mem_kv_proj",
    )(x, w.astype(BF16))


def _xattn_kernel(x_ref, wq_ref, kt_ref, v_ref, wo_ref, g_ref, b_ref, wr_ref, o_ref, lg_ref, *, heads, alpha, parts):
    tm = x_ref.shape[0]
    pm = tm // parts
    nr = lg_ref.shape[0]
    nt = (((1,), (1,)), ((), ()))
    cols = lambda h: slice(XA_DH * h, XA_DH * (h + 1))
    xs = [x_ref[r * pm:(r + 1) * pm, :] for r in range(parts)]
    qs = [(_dot(x.astype(BF16), wq_ref[...]) * (XA_DH ** -0.5)).astype(BF16) for x in xs]
    ss = [[_dot(q[:, cols(h)], kt_ref[0, cols(h), :]) for h in range(heads)] for q in qs]
    ps = []
    for part in ss:
        row = []
        for s in part:
            e = jnp.exp(s - jnp.max(s, axis=-1, keepdims=True))
            row.append((e / jnp.sum(e, axis=-1, keepdims=True)).astype(BF16))
        ps.append(row)
    os = [jnp.concatenate([_dot(p[h], v_ref[0, :, cols(h)]) for h in range(heads)], axis=1).astype(BF16) for p in ps]
    ys = [_layernorm_rows(alpha * xs[r] + _dot(os[r], wo_ref[...]), g_ref[...], b_ref[...]) for r in range(parts)]
    w_hi_lo = wr_ref[...]
    for r in range(parts):
        o_ref[r * pm:(r + 1) * pm, :] = ys[r]
        y_hi = ys[r].astype(BF16)
        y_lo = (ys[r] - y_hi.astype(F32)).astype(BF16)
        p1 = lax.dot_general(w_hi_lo, y_hi, nt, preferred_element_type=F32)
        p2 = lax.dot_general(w_hi_lo[:nr], y_lo, nt, preferred_element_type=F32)
        lg_ref[:, r * pm:(r + 1) * pm] = p1[:nr] + p1[nr:] + p2


def _xattn(x2d, kt, v, w_q, w_o, ln_g, ln_b, w_router_t, seq, alpha, tm=512, parts=2):
    t, d = x2d.shape
    width = w_q.shape[1]
    heads = width // XA_DH
    mlen = v.shape[1]
    per = seq // tm
    nr = w_router_t.shape[0]
    w_hi = w_router_t.astype(BF16)
    w_lo = (w_router_t - w_hi.astype(F32)).astype(BF16)
    row = lambda i: (i, 0)
    return pl.pallas_call(
        functools.partial(_xattn_kernel, heads=heads, alpha=alpha, parts=parts),
        grid=(t // tm,),
        in_specs=[pl.BlockSpec((tm, d), row), _const_spec((d, width)),
                  pl.BlockSpec((1, width, mlen), lambda i: (i // per, 0, 0)),
                  pl.BlockSpec((1, mlen, width), lambda i: (i // per, 0, 0)),
                  _const_spec((width, d)), _const_spec((1, d)), _const_spec((1, d)), _const_spec((2 * nr, d))],
        out_specs=(pl.BlockSpec((tm, d), row), pl.BlockSpec((nr, tm), lambda i: (0, i))),
        out_shape=(jax.ShapeDtypeStruct((t, d), F32), jax.ShapeDtypeStruct((nr, t), F32)),
        compiler_params=_cparams(("parallel",), VMEM_LIMIT),
        name="xattn",
    )(x2d, w_q.astype(BF16), kt, v, w_o.astype(BF16), ln_g.reshape(1, d).astype(F32), ln_b.reshape(1, d).astype(F32),
      jnp.concatenate([w_hi, w_lo], axis=0))


def _router_kernel(lg_ref, bias_ref, eid_ref, gate_ref):
    lg = lg_ref[...] + bias_ref[...]
    ng, ne = MOE_GROUPS, MOE_PER_GROUP
    grp = [lg[g:g + 1, :] for g in range(ng)]
    gmax = functools.reduce(jnp.maximum, grp)
    gexp = [jnp.exp(r - gmax) for r in grp]
    gsum = functools.reduce(lambda a, b: a + b, gexp)
    pg = [e / gsum for e in gexp]
    best, gsel = pg[0], jnp.zeros(pg[0].shape, jnp.int32)
    for g in range(1, ng):
        better = pg[g] > best
        gsel = jnp.where(better, g, gsel)
        best = jnp.where(better, pg[g], best)
    le = []
    for e in range(ne):
        r = lg[ng + e:ng + e + 1, :]
        for g in range(1, ng):
            r = jnp.where(gsel == g, lg[ng + g * ne + e:ng + g * ne + e + 1, :], r)
        le.append(r)
    emax = functools.reduce(jnp.maximum, le)
    eexp = [jnp.exp(r - emax) for r in le]
    esum = functools.reduce(lambda a, b: a + b, eexp)
    pe = [e / esum for e in eexp]
    sel, val = [], []
    for k in range(MOE_TOPK):
        bv, bi = None, None
        for e in range(ne):
            cand = pe[e]
            for prev in sel:
                cand = jnp.where(prev == e, -1.0, cand)
            if bv is None:
                bv, bi = cand, jnp.zeros(cand.shape, jnp.int32)
            else:
                better = cand > bv
                bi = jnp.where(better, e, bi)
                bv = jnp.where(better, cand, bv)
        sel.append(bi)
        val.append(bv)
    tot = functools.reduce(lambda a, b: a + b, val)
    zero_i = jnp.zeros((8 - MOE_TOPK,) + sel[0].shape[1:], jnp.int32)
    zero_f = jnp.zeros((8 - MOE_TOPK,) + sel[0].shape[1:], F32)
    eid_ref[...] = jnp.concatenate([gsel * ne + s for s in sel] + [zero_i], axis=0)
    gate_ref[...] = jnp.concatenate([best * v / tot for v in val] + [zero_f], axis=0)


def _router(lg_t, bias_col, tn=2048):
    nr, t = lg_t.shape
    tn = min(tn, t)
    return pl.pallas_call(
        _router_kernel,
        grid=(t // tn,),
        in_specs=[pl.BlockSpec((nr, tn), lambda i: (0, i)), _const_spec((nr, 1))],
        out_specs=(pl.BlockSpec((8, tn), lambda i: (0, i)), pl.BlockSpec((8, tn), lambda i: (0, i))),
        out_shape=(jax.ShapeDtypeStruct((8, t), jnp.int32), jax.ShapeDtypeStruct((8, t), F32)),
        compiler_params=_cparams(("parallel",)),
        name="router",
    )(lg_t, bias_col)


def _start_row_gather(idx_ref, src_hbm, buf, sem):
    for r in range(buf.shape[0]):
        pltpu.make_async_copy(src_hbm.at[pl.ds(idx_ref[r], 1)], buf.at[pl.ds(r, 1)], sem).start(priority=r % 2)


def _wait_row_gather(buf, sem):
    pltpu.make_async_copy(buf, buf, sem).wait()


def _expert_kernel(be_ref, nu_ref, idx0_ref, idxn_ref, x_hbm, wgu_ref, wd_ref, y_ref, xbuf, sem, wgu_sc, wd_sc, *, ff):
    i = pl.program_id(0)
    n_used = nu_ref[0]
    slot = lax.rem(i, 2)
    changed = jnp.logical_or(i == 0, be_ref[i] != be_ref[jnp.maximum(i - 1, 0)])

    @pl.when(changed)
    def _():
        wgu_sc[...] = wgu_ref[0, 0].astype(BF16)
        wd_sc[...] = wd_ref[0, 0].astype(BF16)

    @pl.when(i == 0)
    def _():
        _start_row_gather(idx0_ref, x_hbm, xbuf.at[0], sem.at[0])

    @pl.when(i < n_used)
    def _():
        _wait_row_gather(xbuf.at[slot], sem.at[slot])
        x = xbuf[slot].astype(BF16)
        _start_row_gather(idxn_ref, x_hbm, xbuf.at[1 - slot], sem.at[1 - slot])
        gu = _dot(x, wgu_sc[...])
        gate = gu[:, :ff]
        h = gate * jax.nn.sigmoid(gate) * gu[:, ff:]
        y_ref[...] = _dot(h.astype(BF16), wd_sc[...])

    @pl.when(i == n_used)
    def _():
        _wait_row_gather(xbuf.at[slot], sem.at[slot])

    @pl.when(i >= n_used)
    def _():
        y_ref[...] = jnp.zeros(y_ref.shape, y_ref.dtype)


def _experts(x2d, row_tok, blk_exp, n_used, w_gate_up, w_down, layer, rb):
    rows = row_tok.shape[0]
    d = x2d.shape[1]
    ff = w_down.shape[2]
    nblk = rows // rb
    grid_spec = pltpu.PrefetchScalarGridSpec(
        num_scalar_prefetch=2,
        grid=(nblk,),
        in_specs=[pl.BlockSpec((rb,), lambda i, be, nu: (0,), memory_space=pltpu.SMEM),
                  pl.BlockSpec((rb,), lambda i, be, nu: (jnp.minimum(i + 1, nblk - 1),), memory_space=pltpu.SMEM),
                  pl.BlockSpec(memory_space=pl.ANY),
                  pl.BlockSpec((1, 1, d, 2 * ff), lambda i, be, nu: (layer, be[i], 0, 0)),
                  pl.BlockSpec((1, 1, ff, d), lambda i, be, nu: (layer, be[i], 0, 0))],
        out_specs=pl.BlockSpec((rb, d), lambda i, be, nu: (i, 0)),
        scratch_shapes=[pltpu.VMEM((2, rb, d), x2d.dtype), pltpu.SemaphoreType.DMA((2,)),
                        pltpu.VMEM((d, 2 * ff), BF16), pltpu.VMEM((ff, d), BF16)],
    )
    return pl.pallas_call(
        functools.partial(_expert_kernel, ff=ff),
        grid_spec=grid_spec,
        out_shape=jax.ShapeDtypeStruct((rows, d), F32),
        compiler_params=_cparams(("arbitrary",), VMEM_LIMIT),
        name="experts",
    )(blk_exp, n_used, row_tok, row_tok, x2d, w_gate_up, w_down)


def _moe_plan(eid, n_exp, rb):
    t, topk = eid.shape
    m = t * topk
    flat_e = eid.reshape(m)
    onehot = (flat_e[:, None] == jnp.arange(n_exp, dtype=jnp.int32)[None, :]).astype(jnp.int32)
    csum = jnp.cumsum(onehot, axis=0)
    counts = csum[-1]
    pcounts = (counts + rb - 1) // rb * rb
    pends = jnp.cumsum(pcounts)
    pstarts = pends - pcounts
    dest = jnp.sum(onehot * (pstarts[None, :] + csum - 1), axis=1)
    rows = m + n_exp * rb
    row_tok = (jnp.arange(rows, dtype=jnp.int32) % t).at[dest].set(jnp.arange(m, dtype=jnp.int32) // topk)
    nblk = rows // rb
    blk_start = jnp.arange(nblk, dtype=jnp.int32) * rb
    blk_exp = jnp.minimum(jnp.sum((pends[None, :] <= blk_start[:, None]).astype(jnp.int32), axis=1), n_exp - 1)
    n_used = (pends[-1] // rb).astype(jnp.int32).reshape(1)
    return dest.astype(jnp.int32), row_tok, blk_exp, n_used


def _moe_out_kernel(*refs, alpha):
    idx0 = refs[:MOE_TOPK]
    idxn = refs[MOE_TOPK:2 * MOE_TOPK]
    y_hbm, x_ref, gate_ref, g_ref, b_ref, o_ref, ybuf, sem = refs[2 * MOE_TOPK:]
    i = pl.program_id(0)
    slot = lax.rem(i, 2)

    @pl.when(i == 0)
    def _():
        for k in range(MOE_TOPK):
            _start_row_gather(idx0[k], y_hbm, ybuf.at[0, k], sem.at[0])

    @pl.when(i + 1 < pl.num_programs(0))
    def _():
        for k in range(MOE_TOPK):
            _start_row_gather(idxn[k], y_hbm, ybuf.at[1 - slot, k], sem.at[1 - slot])

    _wait_row_gather(ybuf.at[slot], sem.at[slot])
    gate = gate_ref[...]
    ffn = gate[:, 0:1] * ybuf[slot, 0]
    for k in range(1, MOE_TOPK):
        ffn = ffn + gate[:, k:k + 1] * ybuf[slot, k]
    o_ref[...] = _layernorm_rows(alpha * x_ref[...] + ffn, g_ref[...], b_ref[...])


def _moe_out(yr, dest_k, x2d, gate, ln_g, ln_b, alpha, tm=256):
    t, d = x2d.shape
    row = lambda i: (i, 0)
    nt = t // tm
    first = [pl.BlockSpec((tm,), functools.partial(lambda k, i: (k * nt,), k), memory_space=pltpu.SMEM) for k in range(MOE_TOPK)]
    nxt = [pl.BlockSpec((tm,), functools.partial(lambda k, i: (k * nt + jnp.minimum(i + 1, nt - 1),), k), memory_space=pltpu.SMEM)
           for k in range(MOE_TOPK)]
    return pl.pallas_call(
        functools.partial(_moe_out_kernel, alpha=alpha),
        grid=(nt,),
        in_specs=first + nxt + [pl.BlockSpec(memory_space=pl.ANY), pl.BlockSpec((tm, d), row),
                                pl.BlockSpec((tm, gate.shape[1]), row), _const_spec((1, d)), _const_spec((1, d))],
        out_specs=pl.BlockSpec((tm, d), row),
        out_shape=jax.ShapeDtypeStruct((t, d), F32),
        scratch_shapes=[pltpu.VMEM((2, MOE_TOPK, tm, d), yr.dtype), pltpu.SemaphoreType.DMA((2,))],
        compiler_params=_cparams(("arbitrary",), VMEM_LIMIT),
        name="moe_out",
    )(*([dest_k] * (2 * MOE_TOPK)), yr, x2d, gate, ln_g.reshape(1, d).astype(F32), ln_b.reshape(1, d).astype(F32))


def _moe(x2, lg_t, b_group, b_expert, w_gate_up, w_down, layer, ln_g, ln_b, alpha):
    t, d = x2.shape
    n_exp = w_gate_up.shape[1]
    nr = lg_t.shape[0]
    bias = jnp.zeros((nr, 1), F32).at[:MOE_GROUPS + n_exp, 0].set(jnp.concatenate([b_group, b_expert]).astype(F32))
    eid_t, gate_t = _router(lg_t, bias)
    eid = eid_t[:MOE_TOPK].T
    gate = gate_t[:MOE_TOPK].T
    dest, row_tok, blk_exp, n_used = _moe_plan(eid, n_exp, MOE_ROW_BLOCK)
    yr = _experts(x2, row_tok, blk_exp, n_used, w_gate_up, w_down, layer, MOE_ROW_BLOCK)
    return _moe_out(yr, dest.reshape(t, MOE_TOPK).T.reshape(-1), x2, gate, ln_g, ln_b, alpha)


def kernel(x, mem, positions, w_in, s5_lambda_re, s5_lambda_im, s5_log_step, s5_b_re, s5_b_im, s5_c_re, s5_c_im, s5_d, s5_w_glu, s5_b_glu, s5_out_norm, mla_q_norm, mla_w_uq, mla_kv_norm, mla_w_ukv, mla_out_norm, gdn_conv, gdn_a_log, gdn_dt_bias, gdn_out_norm, w_out, ln1_g, ln1_b, xa_w_q, xa_w_k, xa_w_v, xa_w_o, ln2_g, ln2_b, moe_w_group, moe_b_group, moe_w_expert, moe_b_expert, moe_w_gate_up, moe_w_down, ln3_g, ln3_b):
    batch, seq, d = x.shape
    t = batch * seq
    depth = w_in.shape[0]
    alpha = (2 * depth) ** 0.25
    mlen = mem.shape[1]
    s5_w = s5_w_glu.shape[1]
    rank_q = mla_w_uq.shape[1]
    rank_kv = mla_w_ukv.shape[1]
    g_heads = gdn_a_log.shape[1]
    g_qk = g_heads * GDN_DK
    g_v = gdn_conv.shape[2] - 2 * g_qk
    o_kr = s5_w + rank_q + rank_kv
    o_gq = o_kr + MLA_ROPE
    o_gz = o_gq + 2 * g_qk + g_v
    o_ga = o_gz + g_v
    widths = (s5_w, rank_q, rank_kv, LANES, 2 * g_qk + g_v, g_v)
    assert G_LANE == MLA_ROPE and B_LANE == G_LANE + g_heads and o_ga + 2 * g_heads == w_in.shape[2]

    cos_t, sin_t = _rope_tables(positions)
    mem2 = mem.reshape(batch * mlen, d)
    xt = x.reshape(t, d)
    for l in range(depth):
        w = w_in[l]
        w_packed = jnp.concatenate(
            [w[:, :o_gq], w[:, o_ga:], jnp.zeros((d, LANES - MLA_ROPE - 2 * g_heads), w.dtype), w[:, o_gq:o_ga]], axis=1).astype(BF16)
        u, cq, ckv, misc, qkv, gz = _in_proj(xt, w_packed, widths, (BF16, BF16, BF16, F32, BF16, BF16), S5_CHUNK)

        tables = _s5_tables(s5_lambda_re[l], s5_lambda_im[l], s5_log_step[l], s5_b_re[l], s5_b_im[l], s5_c_re[l],
                            s5_c_im[l], s5_d[l], S5_CHUNK, seq // S5_CHUNK)
        y_s5 = _s5_glu(_s5_scan(u, tables, batch, seq, S5_CHUNK), s5_w_glu[l], s5_b_glu[l], s5_out_norm[l], S5_CHUNK)

        qt, kk, vt = _mla_proj(cq, ckv, misc, cos_t, sin_t, mla_q_norm[l], mla_kv_norm[l], mla_w_uq[l], mla_w_ukv[l], batch, seq)
        o_mla = _mla_attention(qt, kk, vt).reshape(t, -1)

        gq, gk, gv, gkt, ggb, ggbt, ggrow = _gdn_pre(qkv, misc, gdn_conv[l], gdn_a_log[l], gdn_dt_bias[l], batch, seq)
        local = _gdn_local(gq, gk, gv, gkt, ggb, ggbt, ggrow)
        y_gdn = _gdn_scan(*local, gz, gdn_out_norm[l]).reshape(t, -1)

        x1 = _mix_out(y_s5, o_mla, y_gdn, xt, w_out[l], mla_out_norm[l], ln1_g[l], ln1_b[l], alpha)

        kv_mem = _matmul(mem2, jnp.concatenate([xa_w_k[l], xa_w_v[l]], axis=1))
        xa_w = xa_w_k.shape[2]
        kt_mem = kv_mem[:, :xa_w].reshape(batch, mlen, xa_w).transpose(0, 2, 1).astype(BF16)
        v_mem = kv_mem[:, xa_w:].reshape(batch, mlen, xa_w).astype(BF16)
        n_route = MOE_GROUPS + moe_w_expert.shape[2]
        w_router_t = jnp.pad(jnp.concatenate([moe_w_group[l], moe_w_expert[l]], axis=1).T.astype(F32),
                             ((0, (-n_route) % 8), (0, 0)))
        x2, lg_t = _xattn(x1, kt_mem, v_mem, xa_w_q[l], xa_w_o[l], ln2_g[l], ln2_b[l], w_router_t, seq, alpha)

        xt = _moe(x2, lg_t, moe_b_group[l], moe_b_expert[l], moe_w_gate_up, moe_w_down, l, ln3_g[l], ln3_b[l], alpha)
    return xt.reshape(batch, seq, d)
```

```python
import functools
import math

import jax
import jax.numpy as jnp
from jax import lax
from jax.experimental import pallas as pl
from jax.experimental.pallas import tpu as pltpu

F32 = jnp.float32
BF16 = jnp.bfloat16
HIGHEST = lax.Precision.HIGHEST

S5_CH = 16
S5_STATE = 64
S5_LAMBDA_RE_MAX = -1e-4
S5_CHUNK = 16
MLA_NOPE = 128
MLA_ROPE = 64
MLA_V = 128
ROPE_THETA = 10000.0
GDN_DK = 128
GDN_DV = 128
GDN_CONV = 4
GDN_CHUNK = 64
XA_DH = 128
MOE_GROUPS = 4
MOE_PER_GROUP = 8
MOE_TOPK = 2
MOE_ROW_BLOCK = 256

LANES = 128
VMEM_LIMIT = 56 * 1024 * 1024


def _cparams(sem, vmem=None, flags=None):
    return pltpu.CompilerParams(dimension_semantics=sem, vmem_limit_bytes=vmem, flags=flags)


def _const_spec(shape):
    nd = len(shape)
    return pl.BlockSpec(shape, lambda *_: (0,) * nd)


def _rms_rows(x, gain, eps=1e-6):
    return x * lax.rsqrt(jnp.mean(x * x, axis=-1, keepdims=True) + eps) * gain


def _layernorm_rows(x, g, b, eps=1e-5):
    mu = jnp.mean(x, axis=-1, keepdims=True)
    xc = x - mu
    var = jnp.mean(xc * xc, axis=-1, keepdims=True)
    return xc * lax.rsqrt(var + eps) * g + b


def _dot(a, b):
    return jnp.dot(a, b, preferred_element_type=F32)


def _in_proj_kernel(x_ref, w_ref, u_ref, cq_ref, ckv_ref, misc_ref, qkv_ref, gz_ref, u_sc, *, splits, chunk):
    xb = x_ref[...].astype(BF16)
    outs = (cq_ref, ckv_ref, misc_ref, qkv_ref, gz_ref)
    for o_ref, (lo, hi) in zip(outs, splits[1:]):
        o_ref[...] = _dot(xb, w_ref[:, lo:hi]).astype(o_ref.dtype)
    lo, hi = splits[0]
    u = _dot(xb, w_ref[:, lo:hi])
    sets, tm = u_sc.shape[0], u_sc.shape[1]
    for s in range(sets):
        u_sc[s] = u[:, LANES * s:LANES * (s + 1)]
    for s in range(sets):
        for j in range(chunk):
            u_ref[s, :, LANES * j:LANES * (j + 1)] = u_sc[s, pl.ds(j, tm // chunk, stride=chunk), :].astype(u_ref.dtype)


def _s5_tables(lam_re, lam_im, log_step, b_re, b_im, c_re, c_im, d_skip, chunk, n_chunks):
    g, p = lam_re.shape
    h = b_re.shape[-1]
    gs = LANES // h
    sets = g // gs
    lr = jnp.minimum(lam_re.astype(F32), S5_LAMBDA_RE_MAX)
    li = lam_im.astype(F32)
    dt = jnp.exp(log_step.astype(F32))[:, None]
    mag = jnp.exp(lr * dt)
    th = li * dt
    ab_re, ab_im = mag * jnp.cos(th), mag * jnp.sin(th)
    den = lr * lr + li * li
    nr, ni = ab_re - 1.0, ab_im
    fr = (nr * lr + ni * li) / den
    fi = (ni * lr - nr * li) / den
    br, bi = b_re.astype(F32), b_im.astype(F32)
    bb_re = fr[..., None] * br - fi[..., None] * bi
    bb_im = fr[..., None] * bi + fi[..., None] * br
    cr, ci = c_re.astype(F32), c_im.astype(F32)
    n = jnp.arange(chunk + 1, dtype=F32)[:, None, None]
    pmag = jnp.exp(n * (lr * dt)[None])
    pr, pi = pmag * jnp.cos(n * th[None]), pmag * jnp.sin(n * th[None])

    crt, cit = cr.transpose(0, 2, 1), ci.transpose(0, 2, 1)
    cb_re = crt[:, :, :, None] * bb_re[:, :, None, :] - cit[:, :, :, None] * bb_im[:, :, None, :]
    cb_im = crt[:, :, :, None] * bb_im[:, :, None, :] + cit[:, :, :, None] * bb_re[:, :, None, :]
    kk = jnp.sum(pr[:chunk, :, :, None, None] * cb_re[None] - pi[:chunk, :, :, None, None] * cb_im[None], axis=2)
    kk = kk.at[0].add(jnp.eye(h, dtype=F32)[None] * d_skip.astype(F32)[:, :, None])
    k_c = kk.reshape(chunk, sets, gs, h, h).transpose(1, 0, 4, 2, 3).reshape(sets, chunk, h, LANES)

    n_rev = jnp.arange(chunk - 1, -1, -1, dtype=F32)[:, None, None]
    rmag = jnp.exp(n_rev * (lr * dt)[None])
    pr_rev, pi_rev = rmag * jnp.cos(n_rev * th[None]), rmag * jnp.sin(n_rev * th[None])
    e_re = pr_rev[:, :, :, None] * bb_re[None] - pi_rev[:, :, :, None] * bb_im[None]
    e_im = pr_rev[:, :, :, None] * bb_im[None] + pi_rev[:, :, :, None] * bb_re[None]
    e_all = jnp.stack([e_re, e_im], axis=0).reshape(2, chunk, sets, gs, p, h)
    e_c = e_all.transpose(2, 1, 5, 0, 3, 4).reshape(sets, chunk * h, 2 * gs * p)

    pr1, pi1 = pr[1:], pi[1:]
    f_re = cr[None] * pr1[:, :, None, :] - ci[None] * pi1[:, :, None, :]
    f_im = cr[None] * pi1[:, :, None, :] + ci[None] * pr1[:, :, None, :]
    f_all = jnp.stack([f_re, -f_im], axis=0).reshape(2, chunk, sets, gs, h, p)
    f_c = f_all.transpose(2, 1, 4, 0, 3, 5).reshape(sets, chunk * h, 2 * gs * p)

    steps = max(1, int(math.ceil(math.log2(n_chunks))))
    ar, ai = pr[chunk].reshape(sets, gs * p), pi[chunk].reshape(sets, gs * p)
    a1, a2 = [], []
    for _ in range(steps):
        a1.append(jnp.concatenate([ar, ar], axis=-1))
        a2.append(jnp.concatenate([-ai, ai], axis=-1))
        ar, ai = ar * ar - ai * ai, 2.0 * ar * ai
    pad = (-steps) % 8
    a1 = jnp.pad(jnp.stack(a1, axis=1), ((0, 0), (0, pad), (0, 0)))
    a2 = jnp.pad(jnp.stack(a2, axis=1), ((0, 0), (0, pad), (0, 0)))
    return k_c, e_c, f_c, a1, a2


def _s5_kernel(u_ref, kc_ref, ec_ref, fc_ref, a1_ref, a2_ref, y_ref, csup_sc, e_sc, ft_sc, *, n_chunks, steps):
    chunk, h = kc_ref.shape[1], kc_ref.shape[2]
    gs = LANES // h

    @pl.when(pl.program_id(1) == 0)
    def _():
        rg = lax.broadcasted_iota(jnp.int32, (LANES, LANES), 0) // h
        own128 = rg == lax.broadcasted_iota(jnp.int32, (LANES, LANES), 1) // h
        width = e_sc.shape[1]
        rgw = lax.broadcasted_iota(jnp.int32, (LANES, width), 0) // h
        cgw = (lax.broadcasted_iota(jnp.int32, (LANES, width), 1) % (width // 2)) // (width // 2 // gs)
        own_w = rgw == cgw
        blocks = [jnp.where(own128, jnp.concatenate([kc_ref[0, t]] * gs, axis=0), 0.0).astype(BF16) for t in range(chunk)]
        zero = jnp.zeros((LANES, LANES), BF16)
        for sg in range(chunk // 2):
            below = blocks[2 * sg - 1] if sg > 0 else zero
            csup_sc[sg] = jnp.concatenate([jnp.concatenate([blocks[2 * sg], blocks[2 * sg + 1]], axis=1),
                                           jnp.concatenate([below, blocks[2 * sg]], axis=1)], axis=0)
        for j in range(chunk):
            rows = slice(LANES * j, LANES * (j + 1))
            e_sc[rows, :] = jnp.where(own_w, jnp.concatenate([ec_ref[0, h * j:h * (j + 1), :]] * gs, axis=0), 0.0).astype(BF16)
            ft_sc[rows, :] = jnp.where(own_w, jnp.concatenate([fc_ref[0, h * j:h * (j + 1), :]] * gs, axis=0), 0.0).astype(BF16)

    u = u_ref[0]
    s = _dot(u, e_sc[...])
    rows, width = s.shape
    c_idx = lax.broadcasted_iota(jnp.int32, (rows, width), 0) % n_chunks
    a1 = a1_ref[0]
    a2 = a2_ref[0]
    for k in range(steps):
        sh = 1 << k
        prev = jnp.where(c_idx >= sh, pltpu.roll(s, sh, axis=0), 0.0)
        s = s + a1[k:k + 1, :] * prev + a2[k:k + 1, :] * pltpu.roll(prev, width // 2, axis=1)
    s_in = jnp.where(c_idx >= 1, pltpu.roll(s, 1, axis=0), 0.0)
    y_state = lax.dot_general(s_in.astype(BF16), ft_sc[...], (((1,), (1,)), ((), ())), preferred_element_type=F32)
    sw = 2 * LANES
    for i in range(u.shape[1] // sw):
        acc = y_state[:, i * sw:(i + 1) * sw]
        for sg in range(i + 1):
            acc = acc + _dot(u[:, (i - sg) * sw:(i - sg + 1) * sw], csup_sc[sg])
        y_ref[0, :, i * sw:(i + 1) * sw] = acc


def _s5_scan(us, tables, batch, seq, chunk, batches_per_block=2):
    k_c, e_c, f_c, a1, a2 = tables
    sets = k_c.shape[0]
    n_chunks = seq // chunk
    steps = max(1, int(math.ceil(math.log2(n_chunks))))
    rows = batch * n_chunks
    rb = batches_per_block * n_chunks
    width = chunk * LANES
    sw = e_c.shape[2]
    const = lambda arr: pl.BlockSpec((1,) + arr.shape[1:], lambda i, j: (i,) + (0,) * (arr.ndim - 1))
    return pl.pallas_call(
        functools.partial(_s5_kernel, n_chunks=n_chunks, steps=steps),
        grid=(sets, rows // rb),
        in_specs=[pl.BlockSpec((1, rb, width), lambda i, j: (i, j, 0)),
                  const(k_c), const(e_c), const(f_c), const(a1), const(a2)],
        out_specs=pl.BlockSpec((1, rb, width), lambda i, j: (i, j, 0)),
        out_shape=jax.ShapeDtypeStruct((sets, rows, width), F32),
        scratch_shapes=[pltpu.VMEM((chunk // 2, 2 * LANES, 2 * LANES), BF16), pltpu.VMEM((width, sw), BF16),
                        pltpu.VMEM((width, sw), BF16)],
        compiler_params=_cparams(("parallel", "arbitrary"), VMEM_LIMIT),
        name="s5_scan",
    )(us, k_c, e_c, f_c, a1, a2)


def _s5_glu_kernel(y_ref, w_ref, b_ref, g_ref, o_ref, y_sc, *, chunk):
    sets, tm = y_sc.shape[0], y_sc.shape[1]
    for s in range(sets):
        for j in range(chunk):
            y_sc[s, pl.ds(j, tm // chunk, stride=chunk), :] = y_ref[s, :, LANES * j:LANES * (j + 1)]
    y = jax.nn.gelu(jnp.concatenate([y_sc[s] for s in range(sets)], axis=1))
    z = _dot(y.astype(BF16), w_ref[...]) + b_ref[...]
    y = y * jax.nn.sigmoid(z)
    o_ref[...] = _rms_rows(y, g_ref[...]).astype(o_ref.dtype)


def _mla_proj_kernel(cq_ref, ckv_ref, misc_ref, cos_ref, sin_ref, qn_ref, kvn_ref, wuq_ref, wukv_ref,
                     qt_ref, k_ref, vt_ref, *, heads, scale):
    cq = _rms_rows(cq_ref[...].astype(F32), qn_ref[...]).astype(BF16)
    q = _dot(cq, wuq_ref[...]) * scale
    ckv = _rms_rows(ckv_ref[...].astype(F32), kvn_ref[...]).astype(BF16)
    kv = _dot(ckv, wukv_ref[...])
    cos = cos_ref[...]
    sin = sin_ref[...]
    lane = lax.broadcasted_iota(jnp.int32, cos.shape, 1)
    first_half = (lane % MLA_ROPE) < (MLA_ROPE // 2)

    def rope(x):
        partner = jnp.where(first_half, pltpu.roll(x, LANES - MLA_ROPE // 2, axis=1), pltpu.roll(x, MLA_ROPE // 2, axis=1))
        return x * cos + partner * sin

    kpe = rope(misc_ref[...])
    kpe_lo = jnp.where(lane < MLA_ROPE, kpe, 0.0)
    kpe_hi = pltpu.roll(kpe_lo, MLA_ROPE, axis=1)
    nope_w = heads * MLA_NOPE
    for pair in range(heads // 2):
        q_pe = rope(q[:, nope_w + LANES * pair:nope_w + LANES * (pair + 1)])
        for h in (2 * pair, 2 * pair + 1):
            qh = jnp.concatenate([q[:, MLA_NOPE * h:MLA_NOPE * (h + 1)], q_pe], axis=1)
            qt_ref[0, h] = qh.T.astype(BF16)
            kvw = MLA_NOPE + MLA_V
            kh = jnp.concatenate([kv[:, kvw * h:kvw * h + MLA_NOPE], kpe_lo if h % 2 == 0 else kpe_hi], axis=1)
            k_ref[0, h] = kh.astype(BF16)
            vt_ref[0, h] = kv[:, kvw * h + MLA_NOPE:kvw * (h + 1)].T.astype(BF16)


def _in_mla_kernel(x_ref, w_ref, cos_ref, sin_ref, qn_ref, kvn_ref, wuq_ref, wukv_ref,
                   u_ref, misc_ref, qkv_ref, gz_ref, qt_ref, k_ref, vt_ref, u_sc, cq_sc, ckv_sc, *, splits, chunk, heads, scale):
    _in_proj_kernel(x_ref, w_ref, u_ref, cq_sc, ckv_sc, misc_ref, qkv_ref, gz_ref, u_sc, splits=splits, chunk=chunk)
    _mla_proj_kernel(cq_sc, ckv_sc, misc_ref, cos_ref, sin_ref, qn_ref, kvn_ref, wuq_ref, wukv_ref,
                     qt_ref, k_ref, vt_ref, heads=heads, scale=scale)


def _in_mla(x2d, w_packed, widths, dtypes, chunk, cos_t, sin_t, q_norm, kv_norm, w_uq, w_ukv, batch, seq, tm=256):
    t, d = x2d.shape
    splits, lo = [], 0
    for w in widths:
        splits.append((lo, lo + w))
        lo += w
    sets = widths[0] // LANES
    rank = widths[1]
    heads = w_ukv.shape[1] // (MLA_NOPE + MLA_V)
    dq = MLA_NOPE + MLA_ROPE
    w3 = w_uq.reshape(rank, heads, dq)
    w_uq_p = jnp.concatenate([w3[:, :, :MLA_NOPE].reshape(rank, -1), w3[:, :, MLA_NOPE:].reshape(rank, -1)], axis=1).astype(BF16)
    per = seq // tm
    dk = MLA_NOPE + LANES
    row = lambda i: (i, 0)
    keep = (3, 4, 5)
    out_shape = ((jax.ShapeDtypeStruct((sets, t // chunk, chunk * LANES), dtypes[0]),)
                 + tuple(jax.ShapeDtypeStruct((t, widths[i]), dtypes[i]) for i in keep)
                 + (jax.ShapeDtypeStruct((batch, heads, dk, seq), BF16), jax.ShapeDtypeStruct((batch, heads, seq, dk), BF16),
                    jax.ShapeDtypeStruct((batch, heads, MLA_V, seq), BF16)))
    out_specs = ((pl.BlockSpec((sets, tm // chunk, chunk * LANES), lambda i: (0, i, 0)),)
                 + tuple(pl.BlockSpec((tm, widths[i]), row) for i in keep)
                 + (pl.BlockSpec((1, heads, dk, tm), lambda i: (i // per, 0, 0, i % per)),
                    pl.BlockSpec((1, heads, tm, dk), lambda i: (i // per, 0, i % per, 0)),
                    pl.BlockSpec((1, heads, MLA_V, tm), lambda i: (i // per, 0, 0, i % per))))
    return pl.pallas_call(
        functools.partial(_in_mla_kernel, splits=tuple(splits), chunk=chunk, heads=heads, scale=dq ** -0.5),
        grid=(t // tm,),
        in_specs=[pl.BlockSpec((tm, d), row), _const_spec(w_packed.shape), pl.BlockSpec((tm, LANES), row), pl.BlockSpec((tm, LANES), row),
                  _const_spec((1, rank)), _const_spec((1, rank)), _const_spec(w_uq_p.shape), _const_spec(w_ukv.shape)],
        out_specs=out_specs,
        out_shape=out_shape,
        scratch_shapes=[pltpu.VMEM((sets, tm, LANES), F32), pltpu.VMEM((tm, rank), dtypes[1]), pltpu.VMEM((tm, widths[2]), dtypes[2])],
        compiler_params=_cparams(("parallel",), VMEM_LIMIT),
        name="in_mla",
    )(x2d, w_packed, cos_t, sin_t, q_norm.reshape(1, rank).astype(F32), kv_norm.reshape(1, rank).astype(F32),
      w_uq_p, w_ukv.astype(BF16))


def _flash_kernel(qt_ref, k_ref, vt_ref, o_ref, s_sc, acc_sc, *, tq, sub):
    seq = k_ref.shape[2]
    nq = seq // tq
    nsub = tq // sub

    def scores(qi, j, slot):
        k0 = pl.multiple_of(j * tq, tq)
        s_sc[slot] = _dot(k_ref[0, 0, pl.ds(k0, tq), :], qt_ref[0, 0, :, qi * tq:(qi + 1) * tq])

    def consume(qi, j, slot, m_prev, l_prev, masked, prefetch):
        k0 = pl.multiple_of(j * tq, tq)
        ss = [s_sc[slot, r * sub:(r + 1) * sub, :] for r in range(nsub)]
        prefetch()
        if masked:
            qpos = qi * tq + lax.broadcasted_iota(jnp.int32, ss[0].shape, 1)
            kpos = k0 + lax.broadcasted_iota(jnp.int32, ss[0].shape, 0)
            ss = [jnp.where(kpos + r * sub <= qpos, s, -1e30) for r, s in enumerate(ss)]
        m_new = functools.reduce(jnp.maximum, [jnp.max(s, axis=0, keepdims=True) for s in ss], m_prev)
        alpha = jnp.exp(m_prev - m_new)
        ps = [jnp.exp(s - m_new) for s in ss]
        l_new = alpha * l_prev + functools.reduce(lambda a, b: a + b, [jnp.sum(p, axis=0, keepdims=True) for p in ps])
        pv = [_dot(vt_ref[0, 0, :, pl.ds(pl.multiple_of(k0 + r * sub, sub), sub)], ps[r].astype(BF16)) for r in range(nsub)]
        acc_sc[...] = alpha * acc_sc[...] + functools.reduce(lambda a, b: a + b, pv)
        return m_new, l_new

    scores(0, 0, 0)
    first = 0
    for qi in range(nq):
        acc_sc[...] = jnp.zeros(acc_sc.shape, F32)
        m = jnp.full((1, tq), -1e30, F32)
        l = jnp.zeros((1, tq), F32)

        def visible(j, carry, qi=qi, first=first):
            slot = lax.rem(j + first, 2)
            return consume(qi, j, slot, carry[0], carry[1], False, lambda: scores(qi, j + 1, 1 - slot))

        m, l = lax.fori_loop(0, qi, visible, (m, l))
        dslot = (qi + first) % 2
        if qi + 1 < nq:
            m, l = consume(qi, qi, dslot, m, l, True, lambda: scores(qi + 1, 0, 1 - dslot))
        else:
            m, l = consume(qi, qi, dslot, m, l, True, lambda: None)
        o_ref[0, qi * tq:(qi + 1) * tq, :] = (acc_sc[...] / l).T.astype(o_ref.dtype)
        first = 1 - dslot


def _mla_attention(qt, k, vt, tq=512, sub=256):
    batch, heads, dk, seq = qt.shape
    dv = vt.shape[2]
    return pl.pallas_call(
        functools.partial(_flash_kernel, tq=tq, sub=min(sub, tq)),
        grid=(batch, heads),
        in_specs=[pl.BlockSpec((1, 1, dk, seq), lambda b, h: (b, h, 0, 0)),
                  pl.BlockSpec((1, 1, seq, dk), lambda b, h: (b, h, 0, 0)),
                  pl.BlockSpec((1, 1, dv, seq), lambda b, h: (b, h, 0, 0))],
        out_specs=pl.BlockSpec((1, seq, dv), lambda b, h: (b, 0, h)),
        out_shape=jax.ShapeDtypeStruct((batch, seq, heads * dv), BF16),
        scratch_shapes=[pltpu.VMEM((2, tq, tq), F32), pltpu.VMEM((dv, tq), F32)],
        compiler_params=_cparams(("parallel", "parallel"), VMEM_LIMIT),
        name="mla_flash",
    )(qt, k, vt)


def _rope_tables(positions):
    half = MLA_ROPE // 2
    inv_freq = 1.0 / (ROPE_THETA ** (jnp.arange(half, dtype=F32) * (2.0 / MLA_ROPE)))
    ang = positions.astype(F32).reshape(-1)[:, None] * inv_freq
    cos, sin = jnp.cos(ang), jnp.sin(ang)
    reps = LANES // MLA_ROPE
    return jnp.tile(jnp.concatenate([cos, cos], axis=1), (1, reps)), jnp.tile(jnp.concatenate([-sin, sin], axis=1), (1, reps))


HALO = 16
G_LANE = 64
B_LANE = 68


def _gdn_pre_kernel(x_ref, prev_ref, misc_ref, cw_ref, alog_ref, dtb_ref,
                    q_ref, k_ref, v_ref, kt_ref, gb_ref, gbt_ref, grow_ref, *, heads, chunk):
    x = x_ref[0].astype(F32)
    ts = x.shape[0]
    halo = prev_ref[0].astype(F32)
    prev = jnp.where(pl.program_id(1) > 0, halo[halo.shape[0] - 8:], 0.0)
    cw = cw_ref[...]
    row8 = lax.broadcasted_iota(jnp.int32, prev.shape, 0)
    acc = x * cw[GDN_CONV - 1:GDN_CONV, :]
    for d in range(1, GDN_CONV):
        xr = pltpu.roll(x, d, axis=0)
        head = jnp.where(row8 < d, pltpu.roll(prev, d, axis=0), xr[0:8])
        xs = jnp.concatenate([head, xr[8:]], axis=0)
        acc = acc + xs * cw[GDN_CONV - 1 - d:GDN_CONV - d, :]
    y = acc * jax.nn.sigmoid(acc)
    nqk = heads * GDN_DK

    def l2n(z):
        return z * lax.rsqrt(jnp.sum(z * z, axis=-1, keepdims=True) + 1e-6)

    for h in range(heads):
        q_ref[0, :, GDN_DK * h:GDN_DK * (h + 1)] = l2n(y[:, GDN_DK * h:GDN_DK * (h + 1)]).astype(q_ref.dtype)
    kn = jnp.concatenate([l2n(y[:, nqk + GDN_DK * h:nqk + GDN_DK * (h + 1)]) for h in range(heads)], axis=1)
    k_ref[0] = kn.astype(k_ref.dtype)
    v_ref[0] = y[:, 2 * nqk:].astype(v_ref.dtype)
    knt = kn.T
    for n in range(ts // chunk):
        kt_ref[0, n] = knt[:, chunk * n:chunk * (n + 1)].astype(kt_ref.dtype)
    m = misc_ref[0]
    lane = lax.broadcasted_iota(jnp.int32, m.shape, 1)
    g = -jnp.exp(alog_ref[...]) * jax.nn.softplus(m + dtb_ref[...])
    beta = jax.nn.sigmoid(m)
    gb = jnp.where((lane >= G_LANE) & (lane < G_LANE + heads), g,
                   jnp.where((lane >= B_LANE) & (lane < B_LANE + heads), beta, 0.0))
    gb_ref[0] = gb
    gbt = gb.T[G_LANE:G_LANE + 8, :]
    for n in range(ts // chunk):
        gbt_ref[0, n] = gbt[:, chunk * n:chunk * (n + 1)]
        g_rows = jnp.concatenate([gbt[h:h + 1, chunk * n:chunk * (n + 1)] for h in range(heads)], axis=1)
        grow_ref[0, n] = jnp.broadcast_to(g_rows, (8, heads * chunk))


def _hdot(a, b):
    return jnp.dot(a, b, preferred_element_type=F32, precision=HIGHEST)


def _gdn_local_kernel(q_ref, k_ref, v_ref, kt_ref, gb_ref, gbt_ref, grow_ref,
                      u_ref, w_ref, qd_ref, a_ref, kend_ref, egl_ref, *, heads, chunk, n_chunks):
    c = chunk
    hc = heads * c
    hd = heads * GDN_DK
    iota = lambda shape, ax: lax.broadcasted_iota(jnp.int32, shape, ax)
    ri, li = iota((c, hc), 0), iota((c, hc), 1)
    lj, lh = li % c, li // c
    tri_cat = ri >= lj
    strict_cat = ri > lj
    eye_cat = (ri == lj).astype(F32)
    r2, l2 = iota((hc, hc), 0), iota((hc, hc), 1)
    same_blk = (r2 // c) == (l2 // c)
    tri_bd = jnp.logical_and(same_blk, (r2 % c) <= (l2 % c)).astype(F32)
    head_rows = (iota((hc, hd), 0) // c) == (iota((hc, hd), 1) // GDN_DK)
    r1, c1 = iota((c, c), 0), iota((c, c), 1)
    tri_f = (r1 >= c1).astype(F32)
    tri_t = (r1 <= c1).astype(F32)
    nt = (((1,), (1,)), ((), ()))

    def bdiag(x):
        return jnp.where(same_blk, jnp.concatenate([x] * heads, axis=0), 0.0)

    def bdiag_wide(x):
        return jnp.where(head_rows, jnp.concatenate([x] * heads, axis=0), 0.0)

    def per_head_cols(cols, width):
        return jnp.concatenate([jnp.broadcast_to(col, (c, width)) for col in cols], axis=1)

    st = []
    for n in range(n_chunks):
        r0 = n * c
        gbc = gb_ref[0, r0:r0 + c, :]
        gcc = _hdot(tri_f, gbc)
        gc_cols = [gcc[:, G_LANE + h:G_LANE + h + 1] for h in range(heads)]
        gc_c = jnp.broadcast_to(gc_cols[0], (c, hc))
        for h in range(1, heads):
            gc_c = jnp.where(lh == h, jnp.broadcast_to(gc_cols[h], (c, hc)), gc_c)
        gc_r = _hdot(grow_ref[0, n], tri_bd)[0:1, :]
        decay = jnp.where(tri_cat, jnp.exp(jnp.where(tri_cat, gc_c - gc_r, 0.0)), 0.0)
        beta_w = per_head_cols([gbc[:, B_LANE + h:B_LANE + h + 1] for h in range(heads)], GDN_DK)
        eg_w = per_head_cols([jnp.exp(col) for col in gc_cols], GDN_DK)
        q = q_ref[0, r0:r0 + c, :].astype(F32) * (GDN_DK ** -0.5)
        k = k_ref[0, r0:r0 + c, :].astype(F32)
        v = v_ref[0, r0:r0 + c, :].astype(F32)
        kb = k * beta_w
        k_bd = bdiag_wide(k).astype(BF16)
        kk = lax.dot_general(kb.astype(BF16), k_bd, nt, preferred_element_type=F32)
        qk = lax.dot_general(q.astype(BF16), k_bd, nt, preferred_element_type=F32)
        lmat = jnp.where(strict_cat, kk * decay, 0.0)
        qd_ref[0, r0:r0 + c, :] = (q * eg_w).astype(qd_ref.dtype)
        a_ref[0, r0:r0 + c, :] = jnp.where(tri_cat, qk * decay, 0.0).astype(a_ref.dtype)
        gcr = _hdot(gbt_ref[0, n], tri_t)
        g_last = [gcr[h:h + 1, c - 1:c] for h in range(heads)]
        f = jnp.concatenate([jnp.broadcast_to(jnp.exp(g_last[h] - gcr[h:h + 1, :]), (GDN_DK, c)) for h in range(heads)], axis=0)
        kend_ref[0, n] = (kt_ref[0, n].astype(F32) * f).astype(kend_ref.dtype)
        egl_ref[0, n] = jnp.concatenate([jnp.broadcast_to(jnp.exp(g_last[h]), (1, LANES)) for h in range(heads)]
                                        + [jnp.zeros((8 - heads, LANES), F32)], axis=0)
        st.append(dict(p=eye_cat - lmat, sq=lmat, vb=v * beta_w, kbe=kb * eg_w))
    kpow = 2
    while kpow < c:
        for d in st:
            d["sq"] = _dot(d["sq"].astype(BF16), bdiag(d["sq"]).astype(BF16))
        for d in st:
            d["p"] = d["p"] + _dot(d["p"].astype(BF16), bdiag(d["sq"]).astype(BF16))
        kpow *= 2
    for n, d in enumerate(st):
        r0 = n * c
        tmat = d["p"].astype(BF16)
        u_ref[0, r0:r0 + c, :] = _dot(tmat, bdiag_wide(d["vb"]).astype(BF16))
        w_ref[0, r0:r0 + c, :] = _dot(tmat, bdiag_wide(d["kbe"]).astype(BF16)).astype(w_ref.dtype)


def _gdn_front_kernel(x_ref, prev_ref, misc_ref, cw_ref, alog_ref, dtb_ref,
                      u_ref, w_ref, qd_ref, a_ref, kend_ref, egl_ref,
                      q_sc, k_sc, v_sc, kt_sc, gb_sc, gbt_sc, grow_sc, *, heads, chunk, n_chunks):
    _gdn_pre_kernel(x_ref, prev_ref, misc_ref, cw_ref, alog_ref, dtb_ref,
                    q_sc, k_sc, v_sc, kt_sc, gb_sc, gbt_sc, grow_sc, heads=heads, chunk=chunk)
    _gdn_local_kernel(q_sc, k_sc, v_sc, kt_sc, gb_sc, gbt_sc, grow_sc,
                      u_ref, w_ref, qd_ref, a_ref, kend_ref, egl_ref, heads=heads, chunk=chunk, n_chunks=n_chunks)


def _gdn_front(qkv, misc, w_conv, a_log, dt_bias, batch, seq, cb=8):
    width = qkv.shape[-1]
    heads = a_log.shape[0]
    chunk = GDN_CHUNK
    hd = heads * GDN_DK
    n_all = seq // chunk
    ts = cb * chunk
    x3 = qkv.reshape(batch, seq, width)
    m3 = misc.reshape(batch, seq, LANES)
    alog_row = jnp.zeros((1, LANES), F32).at[0, G_LANE:G_LANE + heads].set(a_log.astype(F32))
    dtb_row = jnp.zeros((1, LANES), F32).at[0, G_LANE:G_LANE + heads].set(dt_bias.astype(F32))
    tok = lambda b, s: (b, s, 0)
    ck = lambda b, s: (b, s, 0, 0)
    return pl.pallas_call(
        functools.partial(_gdn_front_kernel, heads=heads, chunk=chunk, n_chunks=cb),
        grid=(batch, n_all // cb),
        in_specs=[pl.BlockSpec((1, ts, width), tok),
                  pl.BlockSpec((1, HALO, width), lambda b, s: (b, jnp.maximum(s * (ts // HALO) - 1, 0), 0)),
                  pl.BlockSpec((1, ts, LANES), tok),
                  _const_spec(w_conv.shape), _const_spec((1, LANES)), _const_spec((1, LANES))],
        out_specs=(pl.BlockSpec((1, ts, hd), tok), pl.BlockSpec((1, ts, hd), tok), pl.BlockSpec((1, ts, hd), tok),
                   pl.BlockSpec((1, ts, heads * chunk), tok), pl.BlockSpec((1, cb, hd, chunk), ck),
                   pl.BlockSpec((1, cb, 8, LANES), ck)),
        out_shape=(jax.ShapeDtypeStruct((batch, seq, hd), F32), jax.ShapeDtypeStruct((batch, seq, hd), BF16),
                   jax.ShapeDtypeStruct((batch, seq, hd), BF16), jax.ShapeDtypeStruct((batch, seq, heads * chunk), BF16),
                   jax.ShapeDtypeStruct((batch, n_all, hd, chunk), BF16), jax.ShapeDtypeStruct((batch, n_all, 8, LANES), F32)),
        scratch_shapes=[pltpu.VMEM((1, ts, hd), BF16), pltpu.VMEM((1, ts, hd), BF16), pltpu.VMEM((1, ts, width - 2 * hd), BF16),
                        pltpu.VMEM((1, cb, hd, chunk), BF16), pltpu.VMEM((1, ts, LANES), F32),
                        pltpu.VMEM((1, cb, 8, chunk), F32), pltpu.VMEM((1, cb, 8, heads * chunk), F32)],
        compiler_params=_cparams(("parallel", "parallel"), VMEM_LIMIT),
        name="gdn_front",
    )(x3, x3, m3, w_conv.astype(F32), alog_row, dtb_row)


def _gdn_scan_kernel(u_ref, w_ref, qd_ref, a_ref, kend_ref, egl_ref, gz_ref, gn_ref, o_ref, st_ref, *, heads, chunk, n_chunks):
    c = chunk
    nb = u_ref.shape[0]

    @pl.when(pl.program_id(1) == 0)
    def _():
        st_ref[...] = jnp.zeros(st_ref.shape, F32)

    gn = gn_ref[...]
    lanes = [(b, h) for b in range(nb) for h in range(heads)]
    state = {bh: st_ref[bh[0], bh[1]] for bh in lanes}
    col = lambda h: slice(GDN_DV * h, GDN_DV * (h + 1))
    for n in range(n_chunks):
        r0 = n * c
        sb = {bh: state[bh].astype(BF16) for bh in lanes}
        v_new = {(b, h): u_ref[b, r0:r0 + c, col(h)] - _dot(w_ref[b, r0:r0 + c, col(h)], sb[(b, h)]) for b, h in lanes}
        vb = {bh: v_new[bh].astype(BF16) for bh in lanes}
        state = {(b, h): state[(b, h)] * egl_ref[b, n][h:h + 1, :] + _dot(kend_ref[b, n, col(h), :], vb[(b, h)]) for b, h in lanes}
        for b, h in lanes:
            o = _dot(qd_ref[b, r0:r0 + c, col(h)], sb[(b, h)]) + _dot(a_ref[b, r0:r0 + c, c * h:c * (h + 1)], vb[(b, h)])
            z = gz_ref[b, r0:r0 + c, col(h)].astype(F32)
            o_ref[b, r0:r0 + c, col(h)] = (_rms_rows(o, gn) * (z * jax.nn.sigmoid(z))).astype(o_ref.dtype)
    for b, h in lanes:
        st_ref[b, h] = state[(b, h)]


def _gdn_scan(u, w, qd, a, kend, egl, gz, g_out, cs=4):
    batch, seq, hd = u.shape
    nb = max(n for n in (4, 2, 1) if batch % n == 0)
    heads = hd // GDN_DV
    chunk = GDN_CHUNK
    n_all = seq // chunk
    ts = cs * chunk
    tok = lambda b, s: (b, s, 0)
    ck = lambda b, s: (b, s, 0, 0)
    return pl.pallas_call(
        functools.partial(_gdn_scan_kernel, heads=heads, chunk=chunk, n_chunks=cs),
        grid=(batch // nb, n_all // cs),
        in_specs=[pl.BlockSpec((nb, ts, hd), tok), pl.BlockSpec((nb, ts, hd), tok), pl.BlockSpec((nb, ts, hd), tok),
                  pl.BlockSpec((nb, ts, heads * chunk), tok), pl.BlockSpec((nb, cs, hd, chunk), ck),
                  pl.BlockSpec((nb, cs, 8, LANES), ck), pl.BlockSpec((nb, ts, hd), tok), _const_spec((1, GDN_DV))],
        out_specs=pl.BlockSpec((nb, ts, hd), tok),
        out_shape=jax.ShapeDtypeStruct((batch, seq, hd), BF16),
        scratch_shapes=[pltpu.VMEM((nb, heads, GDN_DK, GDN_DV), F32)],
        compiler_params=_cparams(("parallel", "arbitrary"), VMEM_LIMIT),
        name="gdn_scan",
    )(u, w, qd, a, kend, egl, gz.reshape(batch, seq, hd), g_out.reshape(1, GDN_DV).astype(F32))


def _mix_out_kernel(s5_ref, mla_ref, gdn_ref, x_ref, w_ref, mg_ref, g_ref, b_ref, o_ref, *, alpha, parts):
    w5 = s5_ref.shape[1]
    wm = mla_ref.shape[1]
    pm = x_ref.shape[0] // parts
    rows = lambda r: slice(r * pm, (r + 1) * pm)
    accs = []
    for r in range(parts):
        mla = _rms_rows(mla_ref[rows(r), :].astype(F32), mg_ref[...]).astype(BF16)
        accs.append(_dot(s5_ref[rows(r), :], w_ref[0:w5, :]) + _dot(mla, w_ref[w5:w5 + wm, :])
                    + _dot(gdn_ref[rows(r), :], w_ref[w5 + wm:, :]))
    for r in range(parts):
        o_ref[rows(r), :] = _layernorm_rows(alpha * x_ref[rows(r), :] + accs[r], g_ref[...], b_ref[...])


def _mix_kernel(ys_ref, wg_ref, bg_ref, gg_ref, mla_ref, gdn_ref, x_ref, w_ref, mg_ref, g_ref, b_ref, o_ref, ysc, s5_sc,
                *, alpha, parts, chunk):
    _s5_glu_kernel(ys_ref, wg_ref, bg_ref, gg_ref, s5_sc, ysc, chunk=chunk)
    _mix_out_kernel(s5_sc, mla_ref, gdn_ref, x_ref, w_ref, mg_ref, g_ref, b_ref, o_ref, alpha=alpha, parts=parts)


def _mix(ys, w_glu, b_glu, g_s5, o_mla, y_gdn, x2d, w_out, mla_gain, ln_g, ln_b, alpha, chunk, tm=512, parts=2):
    t, d = x2d.shape
    sets, _, width = ys.shape
    ws = sets * LANES
    row = lambda i: (i, 0)
    return pl.pallas_call(
        functools.partial(_mix_kernel, alpha=alpha, parts=parts, chunk=chunk),
        grid=(t // tm,),
        in_specs=[pl.BlockSpec((sets, tm // chunk, width), lambda i: (0, i, 0)), _const_spec((ws, ws)), _const_spec((1, ws)),
                  _const_spec((1, ws)),
                  pl.BlockSpec((tm, o_mla.shape[1]), row), pl.BlockSpec((tm, y_gdn.shape[1]), row), pl.BlockSpec((tm, d), row),
                  _const_spec(w_out.shape), _const_spec((1, o_mla.shape[1])), _const_spec((1, d)), _const_spec((1, d))],
        out_specs=pl.BlockSpec((tm, d), row),
        out_shape=jax.ShapeDtypeStruct((t, d), F32),
        scratch_shapes=[pltpu.VMEM((sets, tm, LANES), F32), pltpu.VMEM((tm, ws), BF16)],
        compiler_params=_cparams(("parallel",), VMEM_LIMIT),
        name="mix_out",
    )(ys, w_glu.astype(BF16), b_glu.reshape(1, ws).astype(F32), g_s5.reshape(1, ws).astype(F32), o_mla, y_gdn, x2d,
      w_out.astype(BF16), mla_gain.reshape(1, -1).astype(F32), ln_g.reshape(1, d).astype(F32), ln_b.reshape(1, d).astype(F32))


def _matmul_kernel(x_ref, w_ref, o_ref):
    o_ref[...] = _dot(x_ref[...].astype(BF16), w_ref[...]).astype(o_ref.dtype)


def _matmul(x, w, tm=256, tn=512):
    m, k = x.shape
    n = w.shape[1]
    return pl.pallas_call(
        _matmul_kernel,
        grid=(m // tm, n // tn),
        in_specs=[pl.BlockSpec((tm, k), lambda i, j: (i, 0)), pl.BlockSpec((k, tn), lambda i, j: (0, j))],
        out_specs=pl.BlockSpec((tm, tn), lambda i, j: (i, j)),
        out_shape=jax.ShapeDtypeStruct((m, n), F32),
        compiler_params=_cparams(("parallel", "parallel")),
        name="mem_kv_proj",
    )(x, w.astype(BF16))


def _xattn_kernel(x_ref, wq_ref, kt_ref, v_ref, wo_ref, g_ref, b_ref, wr_ref, o_ref, lg_ref, *, heads, alpha, parts):
    tm = x_ref.shape[0]
    pm = tm // parts
    nr = lg_ref.shape[0]
    nt = (((1,), (1,)), ((), ()))
    cols = lambda h: slice(XA_DH * h, XA_DH * (h + 1))
    xs = [x_ref[r * pm:(r + 1) * pm, :] for r in range(parts)]
    qs = [(_dot(x.astype(BF16), wq_ref[...]) * (XA_DH ** -0.5)).astype(BF16) for x in xs]
    ss = [[_dot(q[:, cols(h)], kt_ref[0, cols(h), :]) for h in range(heads)] for q in qs]
    ps = []
    for part in ss:
        row = []
        for s in part:
            e = jnp.exp(s - jnp.max(s, axis=-1, keepdims=True))
            row.append((e / jnp.sum(e, axis=-1, keepdims=True)).astype(BF16))
        ps.append(row)
    os = [jnp.concatenate([_dot(p[h], v_ref[0, :, cols(h)]) for h in range(heads)], axis=1).astype(BF16) for p in ps]
    ys = [_layernorm_rows(alpha * xs[r] + _dot(os[r], wo_ref[...]), g_ref[...], b_ref[...]) for r in range(parts)]
    w_hi_lo = wr_ref[...]
    for r in range(parts):
        o_ref[r * pm:(r + 1) * pm, :] = ys[r]
        y_hi = ys[r].astype(BF16)
        y_lo = (ys[r] - y_hi.astype(F32)).astype(BF16)
        p1 = lax.dot_general(w_hi_lo, y_hi, nt, preferred_element_type=F32)
        p2 = lax.dot_general(w_hi_lo[:nr], y_lo, nt, preferred_element_type=F32)
        lg_ref[:, r * pm:(r + 1) * pm] = p1[:nr] + p1[nr:] + p2


def _xattn(x2d, kt, v, w_q, w_o, ln_g, ln_b, w_router_t, seq, alpha, tm=512, parts=2):
    t, d = x2d.shape
    width = w_q.shape[1]
    heads = width // XA_DH
    mlen = v.shape[1]
    per = seq // tm
    nr = w_router_t.shape[0]
    w_hi = w_router_t.astype(BF16)
    w_lo = (w_router_t - w_hi.astype(F32)).astype(BF16)
    row = lambda i: (i, 0)
    return pl.pallas_call(
        functools.partial(_xattn_kernel, heads=heads, alpha=alpha, parts=parts),
        grid=(t // tm,),
        in_specs=[pl.BlockSpec((tm, d), row), _const_spec((d, width)),
                  pl.BlockSpec((1, width, mlen), lambda i: (i // per, 0, 0)),
                  pl.BlockSpec((1, mlen, width), lambda i: (i // per, 0, 0)),
                  _const_spec((width, d)), _const_spec((1, d)), _const_spec((1, d)), _const_spec((2 * nr, d))],
        out_specs=(pl.BlockSpec((tm, d), row), pl.BlockSpec((nr, tm), lambda i: (0, i))),
        out_shape=(jax.ShapeDtypeStruct((t, d), F32), jax.ShapeDtypeStruct((nr, t), F32)),
        compiler_params=_cparams(("parallel",), VMEM_LIMIT),
        name="xattn",
    )(x2d, w_q.astype(BF16), kt, v, w_o.astype(BF16), ln_g.reshape(1, d).astype(F32), ln_b.reshape(1, d).astype(F32),
      jnp.concatenate([w_hi, w_lo], axis=0))


def _router_kernel(lg_ref, bias_ref, eid_ref, gate_ref):
    lg = lg_ref[...] + bias_ref[...]
    ng, ne = MOE_GROUPS, MOE_PER_GROUP
    grp = [lg[g:g + 1, :] for g in range(ng)]
    gmax = functools.reduce(jnp.maximum, grp)
    gexp = [jnp.exp(r - gmax) for r in grp]
    gsum = functools.reduce(lambda a, b: a + b, gexp)
    pg = [e / gsum for e in gexp]
    best, gsel = pg[0], jnp.zeros(pg[0].shape, jnp.int32)
    for g in range(1, ng):
        better = pg[g] > best
        gsel = jnp.where(better, g, gsel)
        best = jnp.where(better, pg[g], best)
    le = []
    for e in range(ne):
        r = lg[ng + e:ng + e + 1, :]
        for g in range(1, ng):
            r = jnp.where(gsel == g, lg[ng + g * ne + e:ng + g * ne + e + 1, :], r)
        le.append(r)
    emax = functools.reduce(jnp.maximum, le)
    eexp = [jnp.exp(r - emax) for r in le]
    esum = functools.reduce(lambda a, b: a + b, eexp)
    pe = [e / esum for e in eexp]
    sel, val = [], []
    for k in range(MOE_TOPK):
        bv, bi = None, None
        for e in range(ne):
            cand = pe[e]
            for prev in sel:
                cand = jnp.where(prev == e, -1.0, cand)
            if bv is None:
                bv, bi = cand, jnp.zeros(cand.shape, jnp.int32)
            else:
                better = cand > bv
                bi = jnp.where(better, e, bi)
                bv = jnp.where(better, cand, bv)
        sel.append(bi)
        val.append(bv)
    tot = functools.reduce(lambda a, b: a + b, val)
    zero_i = jnp.zeros((8 - MOE_TOPK,) + sel[0].shape[1:], jnp.int32)
    zero_f = jnp.zeros((8 - MOE_TOPK,) + sel[0].shape[1:], F32)
    eid_ref[...] = jnp.concatenate([gsel * ne + s for s in sel] + [zero_i], axis=0)
    gate_ref[...] = jnp.concatenate([best * v / tot for v in val] + [zero_f], axis=0)


def _router(lg_t, bias_col, tn=2048):
    nr, t = lg_t.shape
    tn = min(tn, t)
    return pl.pallas_call(
        _router_kernel,
        grid=(t // tn,),
        in_specs=[pl.BlockSpec((nr, tn), lambda i: (0, i)), _const_spec((nr, 1))],
        out_specs=(pl.BlockSpec((8, tn), lambda i: (0, i)), pl.BlockSpec((8, tn), lambda i: (0, i))),
        out_shape=(jax.ShapeDtypeStruct((8, t), jnp.int32), jax.ShapeDtypeStruct((8, t), F32)),
        compiler_params=_cparams(("parallel",)),
        name="router",
    )(lg_t, bias_col)


def _start_row_gather(idx_ref, src_hbm, buf, sem):
    for r in range(buf.shape[0]):
        pltpu.make_async_copy(src_hbm.at[pl.ds(idx_ref[r], 1)], buf.at[pl.ds(r, 1)], sem).start(priority=r % 2)


def _wait_row_gather(buf, sem):
    pltpu.make_async_copy(buf, buf, sem).wait()


def _expert_kernel(be_ref, nu_ref, idx0_ref, idxn_ref, x_hbm, wgu_ref, wd_ref, y_ref, xbuf, sem, wgu_sc, wd_sc, *, ff):
    i = pl.program_id(0)
    n_used = nu_ref[0]
    slot = lax.rem(i, 2)
    changed = jnp.logical_or(i == 0, be_ref[i] != be_ref[jnp.maximum(i - 1, 0)])

    @pl.when(changed)
    def _():
        wgu_sc[...] = wgu_ref[0, 0].astype(BF16)
        wd_sc[...] = wd_ref[0, 0].astype(BF16)

    @pl.when(i == 0)
    def _():
        _start_row_gather(idx0_ref, x_hbm, xbuf.at[0], sem.at[0])

    @pl.when(i < n_used)
    def _():
        _wait_row_gather(xbuf.at[slot], sem.at[slot])
        x = xbuf[slot].astype(BF16)
        _start_row_gather(idxn_ref, x_hbm, xbuf.at[1 - slot], sem.at[1 - slot])
        gu = _dot(x, wgu_sc[...])
        gate = gu[:, :ff]
        h = gate * jax.nn.sigmoid(gate) * gu[:, ff:]
        y_ref[...] = _dot(h.astype(BF16), wd_sc[...])

    @pl.when(i == n_used)
    def _():
        _wait_row_gather(xbuf.at[slot], sem.at[slot])

    @pl.when(i >= n_used)
    def _():
        y_ref[...] = jnp.zeros(y_ref.shape, y_ref.dtype)


def _experts(x2d, row_tok, blk_exp, n_used, w_gate_up, w_down, layer, rb):
    rows = row_tok.shape[0]
    d = x2d.shape[1]
    ff = w_down.shape[2]
    nblk = rows // rb
    grid_spec = pltpu.PrefetchScalarGridSpec(
        num_scalar_prefetch=2,
        grid=(nblk,),
        in_specs=[pl.BlockSpec((rb,), lambda i, be, nu: (0,), memory_space=pltpu.SMEM),
                  pl.BlockSpec((rb,), lambda i, be, nu: (jnp.minimum(i + 1, nblk - 1),), memory_space=pltpu.SMEM),
                  pl.BlockSpec(memory_space=pl.ANY),
                  pl.BlockSpec((1, 1, d, 2 * ff), lambda i, be, nu: (layer, be[i], 0, 0)),
                  pl.BlockSpec((1, 1, ff, d), lambda i, be, nu: (layer, be[i], 0, 0))],
        out_specs=pl.BlockSpec((rb, d), lambda i, be, nu: (i, 0)),
        scratch_shapes=[pltpu.VMEM((2, rb, d), x2d.dtype), pltpu.SemaphoreType.DMA((2,)),
                        pltpu.VMEM((d, 2 * ff), BF16), pltpu.VMEM((ff, d), BF16)],
    )
    return pl.pallas_call(
        functools.partial(_expert_kernel, ff=ff),
        grid_spec=grid_spec,
        out_shape=jax.ShapeDtypeStruct((rows, d), F32),
        compiler_params=_cparams(("arbitrary",), VMEM_LIMIT),
        name="experts",
    )(blk_exp, n_used, row_tok, row_tok, x2d, w_gate_up, w_down)


def _moe_plan(eid, n_exp, rb):
    t, topk = eid.shape
    m = t * topk
    flat_e = eid.reshape(m)
    onehot = (flat_e[:, None] == jnp.arange(n_exp, dtype=jnp.int32)[None, :]).astype(jnp.int32)
    csum = jnp.cumsum(onehot, axis=0)
    counts = csum[-1]
    pcounts = (counts + rb - 1) // rb * rb
    pends = jnp.cumsum(pcounts)
    pstarts = pends - pcounts
    dest = jnp.sum(onehot * (pstarts[None, :] + csum - 1), axis=1)
    rows = m + n_exp * rb
    row_tok = (jnp.arange(rows, dtype=jnp.int32) % t).at[dest].set(jnp.arange(m, dtype=jnp.int32) // topk)
    nblk = rows // rb
    blk_start = jnp.arange(nblk, dtype=jnp.int32) * rb
    blk_exp = jnp.minimum(jnp.sum((pends[None, :] <= blk_start[:, None]).astype(jnp.int32), axis=1), n_exp - 1)
    n_used = (pends[-1] // rb).astype(jnp.int32).reshape(1)
    return dest.astype(jnp.int32), row_tok, blk_exp, n_used


def _moe_out_kernel(*refs, alpha):
    idx0 = refs[:MOE_TOPK]
    idxn = refs[MOE_TOPK:2 * MOE_TOPK]
    y_hbm, x_ref, gate_ref, g_ref, b_ref, o_ref, ybuf, sem = refs[2 * MOE_TOPK:]
    i = pl.program_id(0)
    slot = lax.rem(i, 2)

    @pl.when(i == 0)
    def _():
        for k in range(MOE_TOPK):
            _start_row_gather(idx0[k], y_hbm, ybuf.at[0, k], sem.at[0])

    @pl.when(i + 1 < pl.num_programs(0))
    def _():
        for k in range(MOE_TOPK):
            _start_row_gather(idxn[k], y_hbm, ybuf.at[1 - slot, k], sem.at[1 - slot])

    _wait_row_gather(ybuf.at[slot], sem.at[slot])
    gate = gate_ref[...]
    ffn = gate[:, 0:1] * ybuf[slot, 0]
    for k in range(1, MOE_TOPK):
        ffn = ffn + gate[:, k:k + 1] * ybuf[slot, k]
    o_ref[...] = _layernorm_rows(alpha * x_ref[...] + ffn, g_ref[...], b_ref[...])


def _moe_out(yr, dest_k, x2d, gate, ln_g, ln_b, alpha, tm=256):
    t, d = x2d.shape
    row = lambda i: (i, 0)
    nt = t // tm
    first = [pl.BlockSpec((tm,), functools.partial(lambda k, i: (k * nt,), k), memory_space=pltpu.SMEM) for k in range(MOE_TOPK)]
    nxt = [pl.BlockSpec((tm,), functools.partial(lambda k, i: (k * nt + jnp.minimum(i + 1, nt - 1),), k), memory_space=pltpu.SMEM)
           for k in range(MOE_TOPK)]
    return pl.pallas_call(
        functools.partial(_moe_out_kernel, alpha=alpha),
        grid=(nt,),
        in_specs=first + nxt + [pl.BlockSpec(memory_space=pl.ANY), pl.BlockSpec((tm, d), row),
                                pl.BlockSpec((tm, gate.shape[1]), row), _const_spec((1, d)), _const_spec((1, d))],
        out_specs=pl.BlockSpec((tm, d), row),
        out_shape=jax.ShapeDtypeStruct((t, d), F32),
        scratch_shapes=[pltpu.VMEM((2, MOE_TOPK, tm, d), yr.dtype), pltpu.SemaphoreType.DMA((2,))],
        compiler_params=_cparams(("arbitrary",), VMEM_LIMIT),
        name="moe_out",
    )(*([dest_k] * (2 * MOE_TOPK)), yr, x2d, gate, ln_g.reshape(1, d).astype(F32), ln_b.reshape(1, d).astype(F32))


def _moe(x2, lg_t, b_group, b_expert, w_gate_up, w_down, layer, ln_g, ln_b, alpha):
    t, d = x2.shape
    n_exp = w_gate_up.shape[1]
    nr = lg_t.shape[0]
    bias = jnp.zeros((nr, 1), F32).at[:MOE_GROUPS + n_exp, 0].set(jnp.concatenate([b_group, b_expert]).astype(F32))
    eid_t, gate_t = _router(lg_t, bias)
    eid = eid_t[:MOE_TOPK].T
    gate = gate_t[:MOE_TOPK].T
    dest, row_tok, blk_exp, n_used = _moe_plan(eid, n_exp, MOE_ROW_BLOCK)
    yr = _experts(x2, row_tok, blk_exp, n_used, w_gate_up, w_down, layer, MOE_ROW_BLOCK)
    return _moe_out(yr, dest.reshape(t, MOE_TOPK).T.reshape(-1), x2, gate, ln_g, ln_b, alpha)


def kernel(x, mem, positions, w_in, s5_lambda_re, s5_lambda_im, s5_log_step, s5_b_re, s5_b_im, s5_c_re, s5_c_im, s5_d, s5_w_glu, s5_b_glu, s5_out_norm, mla_q_norm, mla_w_uq, mla_kv_norm, mla_w_ukv, mla_out_norm, gdn_conv, gdn_a_log, gdn_dt_bias, gdn_out_norm, w_out, ln1_g, ln1_b, xa_w_q, xa_w_k, xa_w_v, xa_w_o, ln2_g, ln2_b, moe_w_group, moe_b_group, moe_w_expert, moe_b_expert, moe_w_gate_up, moe_w_down, ln3_g, ln3_b):
    batch, seq, d = x.shape
    t = batch * seq
    depth = w_in.shape[0]
    alpha = (2 * depth) ** 0.25
    mlen = mem.shape[1]
    s5_w = s5_w_glu.shape[1]
    rank_q = mla_w_uq.shape[1]
    rank_kv = mla_w_ukv.shape[1]
    g_heads = gdn_a_log.shape[1]
    g_qk = g_heads * GDN_DK
    g_v = gdn_conv.shape[2] - 2 * g_qk
    o_kr = s5_w + rank_q + rank_kv
    o_gq = o_kr + MLA_ROPE
    o_gz = o_gq + 2 * g_qk + g_v
    o_ga = o_gz + g_v
    widths = (s5_w, rank_q, rank_kv, LANES, 2 * g_qk + g_v, g_v)
    assert G_LANE == MLA_ROPE and B_LANE == G_LANE + g_heads and o_ga + 2 * g_heads == w_in.shape[2]

    cos_t, sin_t = _rope_tables(positions)
    mem2 = mem.reshape(batch * mlen, d)
    xt = x.reshape(t, d)
    for l in range(depth):
        w = w_in[l]
        w_packed = jnp.concatenate(
            [w[:, :o_gq], w[:, o_ga:], jnp.zeros((d, LANES - MLA_ROPE - 2 * g_heads), w.dtype), w[:, o_gq:o_ga]], axis=1).astype(BF16)
        u, misc, qkv, gz, qt, kk, vt = _in_mla(xt, w_packed, widths, (BF16, BF16, BF16, F32, BF16, BF16), S5_CHUNK, cos_t, sin_t,
                                               mla_q_norm[l], mla_kv_norm[l], mla_w_uq[l], mla_w_ukv[l], batch, seq)

        tables = _s5_tables(s5_lambda_re[l], s5_lambda_im[l], s5_log_step[l], s5_b_re[l], s5_b_im[l], s5_c_re[l],
                            s5_c_im[l], s5_d[l], S5_CHUNK, seq // S5_CHUNK)
        ys5 = _s5_scan(u, tables, batch, seq, S5_CHUNK)

        o_mla = _mla_attention(qt, kk, vt).reshape(t, -1)

        local = _gdn_front(qkv, misc, gdn_conv[l], gdn_a_log[l], gdn_dt_bias[l], batch, seq)
        y_gdn = _gdn_scan(*local, gz, gdn_out_norm[l]).reshape(t, -1)

        x1 = _mix(ys5, s5_w_glu[l], s5_b_glu[l], s5_out_norm[l], o_mla, y_gdn, xt, w_out[l], mla_out_norm[l], ln1_g[l], ln1_b[l],
                  alpha, S5_CHUNK)

        kv_mem = _matmul(mem2, jnp.concatenate([xa_w_k[l], xa_w_v[l]], axis=1))
        xa_w = xa_w_k.shape[2]
        kt_mem = kv_mem[:, :xa_w].reshape(batch, mlen, xa_w).transpose(0, 2, 1).astype(BF16)
        v_mem = kv_mem[:, xa_w:].reshape(batch, mlen, xa_w).astype(BF16)
        n_route = MOE_GROUPS + moe_w_expert.shape[2]
        w_router_t = jnp.pad(jnp.concatenate([moe_w_group[l], moe_w_expert[l]], axis=1).T.astype(F32),
                             ((0, (-n_route) % 8), (0, 0)))
        x2, lg_t = _xattn(x1, kt_mem, v_mem, xa_w_q[l], xa_w_o[l], ln2_g[l], ln2_b[l], w_router_t, seq, alpha)

        xt = _moe(x2, lg_t, moe_b_group[l], moe_b_expert[l], moe_w_gate_up, moe_w_down, l, ln3_g[l], ln3_b[l], alpha)
    return xt.reshape(batch, seq, d)
```

```python
import functools
import math

import jax
import jax.numpy as jnp
from jax import lax
from jax.experimental import pallas as pl
from jax.experimental.pallas import tpu as pltpu

F32 = jnp.float32
BF16 = jnp.bfloat16
HIGHEST = lax.Precision.HIGHEST

S5_LAMBDA_RE_MAX = -1e-4
S5_CHUNK = 16
MLA_NOPE = 128
MLA_ROPE = 64
MLA_V = 128
ROPE_THETA = 10000.0
GDN_DK = 128
GDN_DV = 128
GDN_CONV = 4
GDN_CHUNK = 64
XA_DH = 128
MOE_GROUPS = 4
MOE_PER_GROUP = 8
MOE_TOPK = 2
MOE_ROW_BLOCK = 256

LANES = 128
VMEM_LIMIT = 56 * 1024 * 1024


def _cparams(sem, vmem=None):
    return pltpu.CompilerParams(dimension_semantics=sem, vmem_limit_bytes=vmem)


def _const_spec(shape):
    nd = len(shape)
    return pl.BlockSpec(shape, lambda *_: (0,) * nd)


def _rms_rows(x, gain, eps=1e-6):
    return x * lax.rsqrt(jnp.mean(x * x, axis=-1, keepdims=True) + eps) * gain


def _layernorm_rows(x, g, b, eps=1e-5):
    mu = jnp.mean(x, axis=-1, keepdims=True)
    xc = x - mu
    var = jnp.mean(xc * xc, axis=-1, keepdims=True)
    return xc * lax.rsqrt(var + eps) * g + b


def _dot(a, b):
    return jnp.dot(a, b, preferred_element_type=F32)


def _in_proj_kernel(x_ref, w_ref, u_ref, cq_ref, ckv_ref, misc_ref, qkv_ref, gz_ref, u_sc, *, splits, chunk):
    xb = x_ref[...].astype(BF16)
    outs = (cq_ref, ckv_ref, misc_ref, qkv_ref, gz_ref)
    for o_ref, (lo, hi) in zip(outs, splits[1:]):
        o_ref[...] = _dot(xb, w_ref[:, lo:hi]).astype(o_ref.dtype)
    lo, hi = splits[0]
    u = _dot(xb, w_ref[:, lo:hi])
    sets, tm = u_sc.shape[0], u_sc.shape[1]
    for s in range(sets):
        u_sc[s] = u[:, LANES * s:LANES * (s + 1)]
    for s in range(sets):
        for j in range(chunk):
            u_ref[s, :, LANES * j:LANES * (j + 1)] = u_sc[s, pl.ds(j, tm // chunk, stride=chunk), :].astype(u_ref.dtype)


def _s5_tables(lam_re, lam_im, log_step, b_re, b_im, c_re, c_im, d_skip, chunk, n_chunks):
    g, p = lam_re.shape
    h = b_re.shape[-1]
    gs = LANES // h
    sets = g // gs
    lr = jnp.minimum(lam_re.astype(F32), S5_LAMBDA_RE_MAX)
    li = lam_im.astype(F32)
    dt = jnp.exp(log_step.astype(F32))[:, None]
    mag = jnp.exp(lr * dt)
    th = li * dt
    ab_re, ab_im = mag * jnp.cos(th), mag * jnp.sin(th)
    den = lr * lr + li * li
    nr, ni = ab_re - 1.0, ab_im
    fr = (nr * lr + ni * li) / den
    fi = (ni * lr - nr * li) / den
    br, bi = b_re.astype(F32), b_im.astype(F32)
    bb_re = fr[..., None] * br - fi[..., None] * bi
    bb_im = fr[..., None] * bi + fi[..., None] * br
    cr, ci = c_re.astype(F32), c_im.astype(F32)
    n = jnp.arange(chunk + 1, dtype=F32)[:, None, None]
    pmag = jnp.exp(n * (lr * dt)[None])
    pr, pi = pmag * jnp.cos(n * th[None]), pmag * jnp.sin(n * th[None])

    crt, cit = cr.transpose(0, 2, 1), ci.transpose(0, 2, 1)
    cb_re = crt[:, :, :, None] * bb_re[:, :, None, :] - cit[:, :, :, None] * bb_im[:, :, None, :]
    cb_im = crt[:, :, :, None] * bb_im[:, :, None, :] + cit[:, :, :, None] * bb_re[:, :, None, :]
    kk = jnp.sum(pr[:chunk, :, :, None, None] * cb_re[None] - pi[:chunk, :, :, None, None] * cb_im[None], axis=2)
    kk = kk.at[0].add(jnp.eye(h, dtype=F32)[None] * d_skip.astype(F32)[:, :, None])
    k_c = kk.reshape(chunk, sets, gs, h, h).transpose(1, 0, 4, 2, 3).reshape(sets, chunk, h, LANES)

    n_rev = jnp.arange(chunk - 1, -1, -1, dtype=F32)[:, None, None]
    rmag = jnp.exp(n_rev * (lr * dt)[None])
    pr_rev, pi_rev = rmag * jnp.cos(n_rev * th[None]), rmag * jnp.sin(n_rev * th[None])
    e_re = pr_rev[:, :, :, None] * bb_re[None] - pi_rev[:, :, :, None] * bb_im[None]
    e_im = pr_rev[:, :, :, None] * bb_im[None] + pi_rev[:, :, :, None] * bb_re[None]
    e_all = jnp.stack([e_re, e_im], axis=0).reshape(2, chunk, sets, gs, p, h)
    e_c = e_all.transpose(2, 1, 5, 0, 3, 4).reshape(sets, chunk * h, 2 * gs * p)

    pr1, pi1 = pr[1:], pi[1:]
    f_re = cr[None] * pr1[:, :, None, :] - ci[None] * pi1[:, :, None, :]
    f_im = cr[None] * pi1[:, :, None, :] + ci[None] * pr1[:, :, None, :]
    f_all = jnp.stack([f_re, -f_im], axis=0).reshape(2, chunk, sets, gs, h, p)
    f_c = f_all.transpose(2, 1, 4, 0, 3, 5).reshape(sets, chunk * h, 2 * gs * p)

    steps = max(1, int(math.ceil(math.log2(n_chunks))))
    ar, ai = pr[chunk].reshape(sets, gs * p), pi[chunk].reshape(sets, gs * p)
    a1, a2 = [], []
    for _ in range(steps):
        a1.append(jnp.concatenate([ar, ar], axis=-1))
        a2.append(jnp.concatenate([-ai, ai], axis=-1))
        ar, ai = ar * ar - ai * ai, 2.0 * ar * ai
    pad = (-steps) % 8
    a1 = jnp.pad(jnp.stack(a1, axis=1), ((0, 0), (0, pad), (0, 0)))
    a2 = jnp.pad(jnp.stack(a2, axis=1), ((0, 0), (0, pad), (0, 0)))
    return k_c, e_c, f_c, a1, a2


def _s5_kernel(u_ref, kc_ref, ec_ref, fc_ref, a1_ref, a2_ref, y_ref, csup_sc, e_sc, ft_sc, *, n_chunks, steps):
    chunk, h = kc_ref.shape[1], kc_ref.shape[2]
    gs = LANES // h

    @pl.when(pl.program_id(1) == 0)
    def _():
        rg = lax.broadcasted_iota(jnp.int32, (LANES, LANES), 0) // h
        own128 = rg == lax.broadcasted_iota(jnp.int32, (LANES, LANES), 1) // h
        width = e_sc.shape[1]
        rgw = lax.broadcasted_iota(jnp.int32, (LANES, width), 0) // h
        cgw = (lax.broadcasted_iota(jnp.int32, (LANES, width), 1) % (width // 2)) // (width // 2 // gs)
        own_w = rgw == cgw
        blocks = [jnp.where(own128, jnp.concatenate([kc_ref[0, t]] * gs, axis=0), 0.0).astype(BF16) for t in range(chunk)]
        zero = jnp.zeros((LANES, LANES), BF16)
        for sg in range(chunk // 2):
            below = blocks[2 * sg - 1] if sg > 0 else zero
            csup_sc[sg] = jnp.concatenate([jnp.concatenate([blocks[2 * sg], blocks[2 * sg + 1]], axis=1),
                                           jnp.concatenate([below, blocks[2 * sg]], axis=1)], axis=0)
        for j in range(chunk):
            rows = slice(LANES * j, LANES * (j + 1))
            e_sc[rows, :] = jnp.where(own_w, jnp.concatenate([ec_ref[0, h * j:h * (j + 1), :]] * gs, axis=0), 0.0).astype(BF16)
            ft_sc[rows, :] = jnp.where(own_w, jnp.concatenate([fc_ref[0, h * j:h * (j + 1), :]] * gs, axis=0), 0.0).astype(BF16)

    u = u_ref[0]
    s = _dot(u, e_sc[...])
    rows, width = s.shape
    c_idx = lax.broadcasted_iota(jnp.int32, (rows, width), 0) % n_chunks
    a1 = a1_ref[0]
    a2 = a2_ref[0]
    for k in range(steps):
        sh = 1 << k
        prev = jnp.where(c_idx >= sh, pltpu.roll(s, sh, axis=0), 0.0)
        s = s + a1[k:k + 1, :] * prev + a2[k:k + 1, :] * pltpu.roll(prev, width // 2, axis=1)
    s_in = jnp.where(c_idx >= 1, pltpu.roll(s, 1, axis=0), 0.0)
    y_state = lax.dot_general(s_in.astype(BF16), ft_sc[...], (((1,), (1,)), ((), ())), preferred_element_type=F32)
    sw = 2 * LANES
    for i in range(u.shape[1] // sw):
        acc = y_state[:, i * sw:(i + 1) * sw]
        for sg in range(i + 1):
            acc = acc + _dot(u[:, (i - sg) * sw:(i - sg + 1) * sw], csup_sc[sg])
        y_ref[0, :, i * sw:(i + 1) * sw] = acc


def _s5_scan(us, tables, batch, seq, chunk, batches_per_block=4):
    k_c, e_c, f_c, a1, a2 = tables
    sets = k_c.shape[0]
    n_chunks = seq // chunk
    steps = max(1, int(math.ceil(math.log2(n_chunks))))
    rows = batch * n_chunks
    rb = max(n for n in range(1, batches_per_block + 1) if batch % n == 0) * n_chunks
    width = chunk * LANES
    sw = e_c.shape[2]
    const = lambda arr: pl.BlockSpec((1,) + arr.shape[1:], lambda i, j: (i,) + (0,) * (arr.ndim - 1))
    return pl.pallas_call(
        functools.partial(_s5_kernel, n_chunks=n_chunks, steps=steps),
        grid=(sets, rows // rb),
        in_specs=[pl.BlockSpec((1, rb, width), lambda i, j: (i, j, 0)),
                  const(k_c), const(e_c), const(f_c), const(a1), const(a2)],
        out_specs=pl.BlockSpec((1, rb, width), lambda i, j: (i, j, 0)),
        out_shape=jax.ShapeDtypeStruct((sets, rows, width), F32),
        scratch_shapes=[pltpu.VMEM((chunk // 2, 2 * LANES, 2 * LANES), BF16), pltpu.VMEM((width, sw), BF16),
                        pltpu.VMEM((width, sw), BF16)],
        compiler_params=_cparams(("parallel", "arbitrary"), VMEM_LIMIT),
        name="s5_scan",
    )(us, k_c, e_c, f_c, a1, a2)


def _s5_glu_kernel(y_ref, w_ref, b_ref, g_ref, o_ref, y_sc, *, chunk):
    sets, tm = y_sc.shape[0], y_sc.shape[1]
    for s in range(sets):
        for j in range(chunk):
            y_sc[s, pl.ds(j, tm // chunk, stride=chunk), :] = y_ref[s, :, LANES * j:LANES * (j + 1)]
    y = jax.nn.gelu(jnp.concatenate([y_sc[s] for s in range(sets)], axis=1))
    z = _dot(y.astype(BF16), w_ref[...]) + b_ref[...]
    y = y * jax.nn.sigmoid(z)
    o_ref[...] = _rms_rows(y, g_ref[...]).astype(o_ref.dtype)


def _mla_proj_kernel(cq_ref, ckv_ref, misc_ref, cos_ref, sin_ref, qn_ref, kvn_ref, wuq_ref, wukv_ref,
                     qt_ref, k_ref, vt_ref, *, heads, scale):
    cq = _rms_rows(cq_ref[...].astype(F32), qn_ref[...]).astype(BF16)
    q = _dot(cq, wuq_ref[...]) * scale
    ckv = _rms_rows(ckv_ref[...].astype(F32), kvn_ref[...]).astype(BF16)
    kv = _dot(ckv, wukv_ref[...])
    cos = cos_ref[...]
    sin = sin_ref[...]
    lane = lax.broadcasted_iota(jnp.int32, cos.shape, 1)
    first_half = (lane % MLA_ROPE) < (MLA_ROPE // 2)

    def rope(x):
        partner = jnp.where(first_half, pltpu.roll(x, LANES - MLA_ROPE // 2, axis=1), pltpu.roll(x, MLA_ROPE // 2, axis=1))
        return x * cos + partner * sin

    kpe = rope(misc_ref[...])
    kpe_lo = jnp.where(lane < MLA_ROPE, kpe, 0.0)
    kpe_hi = pltpu.roll(kpe_lo, MLA_ROPE, axis=1)
    nope_w = heads * MLA_NOPE
    for pair in range(heads // 2):
        q_pe = rope(q[:, nope_w + LANES * pair:nope_w + LANES * (pair + 1)])
        for h in (2 * pair, 2 * pair + 1):
            qh = jnp.concatenate([q[:, MLA_NOPE * h:MLA_NOPE * (h + 1)], q_pe], axis=1)
            qt_ref[0, h] = qh.T.astype(BF16)
            kvw = MLA_NOPE + MLA_V
            kh = jnp.concatenate([kv[:, kvw * h:kvw * h + MLA_NOPE], kpe_lo if h % 2 == 0 else kpe_hi], axis=1)
            k_ref[0, h] = kh.astype(BF16)
            vt_ref[0, h] = kv[:, kvw * h + MLA_NOPE:kvw * (h + 1)].T.astype(BF16)


def _in_mla_kernel(x_ref, w_ref, cos_ref, sin_ref, qn_ref, kvn_ref, wuq_ref, wukv_ref,
                   u_ref, misc_ref, qkv_ref, gz_ref, qt_ref, k_ref, vt_ref, u_sc, cq_sc, ckv_sc, *, splits, chunk, heads, scale):
    _in_proj_kernel(x_ref, w_ref, u_ref, cq_sc, ckv_sc, misc_ref, qkv_ref, gz_ref, u_sc, splits=splits, chunk=chunk)
    _mla_proj_kernel(cq_sc, ckv_sc, misc_ref, cos_ref, sin_ref, qn_ref, kvn_ref, wuq_ref, wukv_ref,
                     qt_ref, k_ref, vt_ref, heads=heads, scale=scale)


def _in_mla(x2d, w_packed, widths, dtypes, chunk, cos_t, sin_t, q_norm, kv_norm, w_uq, w_ukv, batch, seq, tm=256):
    t, d = x2d.shape
    splits, lo = [], 0
    for w in widths:
        splits.append((lo, lo + w))
        lo += w
    sets = widths[0] // LANES
    rank = widths[1]
    heads = w_ukv.shape[1] // (MLA_NOPE + MLA_V)
    dq = MLA_NOPE + MLA_ROPE
    w3 = w_uq.reshape(rank, heads, dq)
    w_uq_p = jnp.concatenate([w3[:, :, :MLA_NOPE].reshape(rank, -1), w3[:, :, MLA_NOPE:].reshape(rank, -1)], axis=1).astype(BF16)
    per = seq // tm
    dk = MLA_NOPE + LANES
    row = lambda i: (i, 0)
    keep = (3, 4, 5)
    out_shape = ((jax.ShapeDtypeStruct((sets, t // chunk, chunk * LANES), dtypes[0]),)
                 + tuple(jax.ShapeDtypeStruct((t, widths[i]), dtypes[i]) for i in keep)
                 + (jax.ShapeDtypeStruct((batch, heads, dk, seq), BF16), jax.ShapeDtypeStruct((batch, heads, seq, dk), BF16),
                    jax.ShapeDtypeStruct((batch, heads, MLA_V, seq), BF16)))
    out_specs = ((pl.BlockSpec((sets, tm // chunk, chunk * LANES), lambda i: (0, i, 0)),)
                 + tuple(pl.BlockSpec((tm, widths[i]), row) for i in keep)
                 + (pl.BlockSpec((1, heads, dk, tm), lambda i: (i // per, 0, 0, i % per)),
                    pl.BlockSpec((1, heads, tm, dk), lambda i: (i // per, 0, i % per, 0)),
                    pl.BlockSpec((1, heads, MLA_V, tm), lambda i: (i // per, 0, 0, i % per))))
    return pl.pallas_call(
        functools.partial(_in_mla_kernel, splits=tuple(splits), chunk=chunk, heads=heads, scale=dq ** -0.5),
        grid=(t // tm,),
        in_specs=[pl.BlockSpec((tm, d), row), _const_spec(w_packed.shape), pl.BlockSpec((tm, LANES), row), pl.BlockSpec((tm, LANES), row),
                  _const_spec((1, rank)), _const_spec((1, rank)), _const_spec(w_uq_p.shape), _const_spec(w_ukv.shape)],
        out_specs=out_specs,
        out_shape=out_shape,
        scratch_shapes=[pltpu.VMEM((sets, tm, LANES), F32), pltpu.VMEM((tm, rank), dtypes[1]), pltpu.VMEM((tm, widths[2]), dtypes[2])],
        compiler_params=_cparams(("parallel",), VMEM_LIMIT),
        name="in_mla",
    )(x2d, w_packed, cos_t, sin_t, q_norm.reshape(1, rank).astype(F32), kv_norm.reshape(1, rank).astype(F32),
      w_uq_p, w_ukv.astype(BF16))


def _flash_kernel(qt_ref, k_ref, vt_ref, o_ref, s_sc, acc_sc, *, tq, sub):
    seq = k_ref.shape[2]
    nq = seq // tq
    nsub = tq // sub

    def scores(qi, j, slot):
        k0 = pl.multiple_of(j * tq, tq)
        s_sc[slot] = _dot(k_ref[0, 0, pl.ds(k0, tq), :], qt_ref[0, 0, :, qi * tq:(qi + 1) * tq])

    def consume(qi, j, slot, m_prev, l_prev, masked, prefetch):
        k0 = pl.multiple_of(j * tq, tq)
        ss = [s_sc[slot, r * sub:(r + 1) * sub, :] for r in range(nsub)]
        prefetch()
        if masked:
            qpos = qi * tq + lax.broadcasted_iota(jnp.int32, ss[0].shape, 1)
            kpos = k0 + lax.broadcasted_iota(jnp.int32, ss[0].shape, 0)
            ss = [jnp.where(kpos + r * sub <= qpos, s, -1e30) for r, s in enumerate(ss)]
        m_new = functools.reduce(jnp.maximum, [jnp.max(s, axis=0, keepdims=True) for s in ss], m_prev)
        alpha = jnp.exp(m_prev - m_new)
        ps = [jnp.exp(s - m_new) for s in ss]
        l_new = alpha * l_prev + functools.reduce(lambda a, b: a + b, [jnp.sum(p, axis=0, keepdims=True) for p in ps])
        pv = [_dot(vt_ref[0, 0, :, pl.ds(pl.multiple_of(k0 + r * sub, sub), sub)], ps[r].astype(BF16)) for r in range(nsub)]
        acc_sc[...] = alpha * acc_sc[...] + functools.reduce(lambda a, b: a + b, pv)
        return m_new, l_new

    scores(0, 0, 0)
    first = 0
    for qi in range(nq):
        acc_sc[...] = jnp.zeros(acc_sc.shape, F32)
        m = jnp.full((1, tq), -1e30, F32)
        l = jnp.zeros((1, tq), F32)

        def visible(j, carry, qi=qi, first=first):
            slot = lax.rem(j + first, 2)
            return consume(qi, j, slot, carry[0], carry[1], False, lambda: scores(qi, j + 1, 1 - slot))

        m, l = lax.fori_loop(0, qi, visible, (m, l))
        dslot = (qi + first) % 2
        if qi + 1 < nq:
            m, l = consume(qi, qi, dslot, m, l, True, lambda: scores(qi + 1, 0, 1 - dslot))
        else:
            m, l = consume(qi, qi, dslot, m, l, True, lambda: None)
        o_ref[0, qi * tq:(qi + 1) * tq, :] = (acc_sc[...] / l).T.astype(o_ref.dtype)
        first = 1 - dslot


def _mla_attention(qt, k, vt, tq=512, sub=256):
    batch, heads, dk, seq = qt.shape
    dv = vt.shape[2]
    return pl.pallas_call(
        functools.partial(_flash_kernel, tq=tq, sub=min(sub, tq)),
        grid=(batch, heads),
        in_specs=[pl.BlockSpec((1, 1, dk, seq), lambda b, h: (b, h, 0, 0)),
                  pl.BlockSpec((1, 1, seq, dk), lambda b, h: (b, h, 0, 0)),
                  pl.BlockSpec((1, 1, dv, seq), lambda b, h: (b, h, 0, 0))],
        out_specs=pl.BlockSpec((1, seq, dv), lambda b, h: (b, 0, h)),
        out_shape=jax.ShapeDtypeStruct((batch, seq, heads * dv), BF16),
        scratch_shapes=[pltpu.VMEM((2, tq, tq), F32), pltpu.VMEM((dv, tq), F32)],
        compiler_params=_cparams(("parallel", "parallel"), VMEM_LIMIT),
        name="mla_flash",
    )(qt, k, vt)


def _rope_tables(positions):
    half = MLA_ROPE // 2
    inv_freq = 1.0 / (ROPE_THETA ** (jnp.arange(half, dtype=F32) * (2.0 / MLA_ROPE)))
    ang = positions.astype(F32).reshape(-1)[:, None] * inv_freq
    cos, sin = jnp.cos(ang), jnp.sin(ang)
    reps = LANES // MLA_ROPE
    return jnp.tile(jnp.concatenate([cos, cos], axis=1), (1, reps)), jnp.tile(jnp.concatenate([-sin, sin], axis=1), (1, reps))


HALO = 16
G_LANE = 64
B_LANE = 68


def _gdn_pre_kernel(x_ref, prev_ref, misc_ref, cw_ref, alog_ref, dtb_ref,
                    q_ref, k_ref, v_ref, kt_ref, gb_ref, gbt_ref, grow_ref, *, heads, chunk):
    x = x_ref[0].astype(F32)
    ts = x.shape[0]
    halo = prev_ref[0].astype(F32)
    prev = jnp.where(pl.program_id(1) > 0, halo[halo.shape[0] - 8:], 0.0)
    cw = cw_ref[...]
    row8 = lax.broadcasted_iota(jnp.int32, prev.shape, 0)
    acc = x * cw[GDN_CONV - 1:GDN_CONV, :]
    for d in range(1, GDN_CONV):
        xr = pltpu.roll(x, d, axis=0)
        head = jnp.where(row8 < d, pltpu.roll(prev, d, axis=0), xr[0:8])
        xs = jnp.concatenate([head, xr[8:]], axis=0)
        acc = acc + xs * cw[GDN_CONV - 1 - d:GDN_CONV - d, :]
    y = acc * jax.nn.sigmoid(acc)
    nqk = heads * GDN_DK

    def l2n(z):
        return z * lax.rsqrt(jnp.sum(z * z, axis=-1, keepdims=True) + 1e-6)

    for h in range(heads):
        q_ref[0, :, GDN_DK * h:GDN_DK * (h + 1)] = l2n(y[:, GDN_DK * h:GDN_DK * (h + 1)]).astype(q_ref.dtype)
    kn = jnp.concatenate([l2n(y[:, nqk + GDN_DK * h:nqk + GDN_DK * (h + 1)]) for h in range(heads)], axis=1)
    k_ref[0] = kn.astype(k_ref.dtype)
    v_ref[0] = y[:, 2 * nqk:].astype(v_ref.dtype)
    knt = kn.T
    for n in range(ts // chunk):
        kt_ref[0, n] = knt[:, chunk * n:chunk * (n + 1)].astype(kt_ref.dtype)
    m = misc_ref[0]
    lane = lax.broadcasted_iota(jnp.int32, m.shape, 1)
    g = -jnp.exp(alog_ref[...]) * jax.nn.softplus(m + dtb_ref[...])
    beta = jax.nn.sigmoid(m)
    gb = jnp.where((lane >= G_LANE) & (lane < G_LANE + heads), g,
                   jnp.where((lane >= B_LANE) & (lane < B_LANE + heads), beta, 0.0))
    gb_ref[0] = gb
    gbt = gb.T[G_LANE:G_LANE + 8, :]
    for n in range(ts // chunk):
        gbt_ref[0, n] = gbt[:, chunk * n:chunk * (n + 1)]
        g_rows = jnp.concatenate([gbt[h:h + 1, chunk * n:chunk * (n + 1)] for h in range(heads)], axis=1)
        grow_ref[0, n] = jnp.broadcast_to(g_rows, (8, heads * chunk))


def _hdot(a, b):
    return jnp.dot(a, b, preferred_element_type=F32, precision=HIGHEST)


def _gdn_local_kernel(q_ref, k_ref, v_ref, kt_ref, gb_ref, gbt_ref, grow_ref,
                      u_ref, w_ref, qd_ref, a_ref, kend_ref, egl_ref, *, heads, chunk, n_chunks):
    c = chunk
    hc = heads * c
    hd = heads * GDN_DK
    iota = lambda shape, ax: lax.broadcasted_iota(jnp.int32, shape, ax)
    ri, li = iota((c, hc), 0), iota((c, hc), 1)
    lj, lh = li % c, li // c
    tri_cat = ri >= lj
    strict_cat = ri > lj
    eye_cat = (ri == lj).astype(F32)
    r2, l2 = iota((hc, hc), 0), iota((hc, hc), 1)
    same_blk = (r2 // c) == (l2 // c)
    tri_bd = jnp.logical_and(same_blk, (r2 % c) <= (l2 % c)).astype(F32)
    head_rows = (iota((hc, hd), 0) // c) == (iota((hc, hd), 1) // GDN_DK)
    r1, c1 = iota((c, c), 0), iota((c, c), 1)
    tri_f = (r1 >= c1).astype(F32)
    tri_t = (r1 <= c1).astype(F32)
    nt = (((1,), (1,)), ((), ()))

    def bdiag(x):
        return jnp.where(same_blk, jnp.concatenate([x] * heads, axis=0), 0.0)

    def bdiag_wide(x):
        return jnp.where(head_rows, jnp.concatenate([x] * heads, axis=0), 0.0)

    def per_head_cols(cols, width):
        return jnp.concatenate([jnp.broadcast_to(col, (c, width)) for col in cols], axis=1)

    st = []
    for n in range(n_chunks):
        r0 = n * c
        gbc = gb_ref[0, r0:r0 + c, :]
        gcc = _hdot(tri_f, gbc)
        gc_cols = [gcc[:, G_LANE + h:G_LANE + h + 1] for h in range(heads)]
        gc_c = jnp.broadcast_to(gc_cols[0], (c, hc))
        for h in range(1, heads):
            gc_c = jnp.where(lh == h, jnp.broadcast_to(gc_cols[h], (c, hc)), gc_c)
        gc_r = _hdot(grow_ref[0, n], tri_bd)[0:1, :]
        decay = jnp.where(tri_cat, jnp.exp(jnp.where(tri_cat, gc_c - gc_r, 0.0)), 0.0)
        beta_w = per_head_cols([gbc[:, B_LANE + h:B_LANE + h + 1] for h in range(heads)], GDN_DK)
        eg_w = per_head_cols([jnp.exp(col) for col in gc_cols], GDN_DK)
        q = q_ref[0, r0:r0 + c, :].astype(F32) * (GDN_DK ** -0.5)
        k = k_ref[0, r0:r0 + c, :].astype(F32)
        v = v_ref[0, r0:r0 + c, :].astype(F32)
        kb = k * beta_w
        k_bd = bdiag_wide(k).astype(BF16)
        kk = lax.dot_general(kb.astype(BF16), k_bd, nt, preferred_element_type=F32)
        qk = lax.dot_general(q.astype(BF16), k_bd, nt, preferred_element_type=F32)
        lmat = jnp.where(strict_cat, kk * decay, 0.0)
        qd_ref[0, r0:r0 + c, :] = (q * eg_w).astype(qd_ref.dtype)
        a_ref[0, r0:r0 + c, :] = jnp.where(tri_cat, qk * decay, 0.0).astype(a_ref.dtype)
        gcr = _hdot(gbt_ref[0, n], tri_t)
        g_last = [gcr[h:h + 1, c - 1:c] for h in range(heads)]
        f = jnp.concatenate([jnp.broadcast_to(jnp.exp(g_last[h] - gcr[h:h + 1, :]), (GDN_DK, c)) for h in range(heads)], axis=0)
        kend_ref[0, n] = (kt_ref[0, n].astype(F32) * f).astype(kend_ref.dtype)
        egl_ref[0, n] = jnp.concatenate([jnp.broadcast_to(jnp.exp(g_last[h]), (1, LANES)) for h in range(heads)]
                                        + [jnp.zeros((8 - heads, LANES), F32)], axis=0)
        st.append(dict(p=eye_cat - lmat, sq=lmat, vb=v * beta_w, kbe=kb * eg_w))
    kpow = 2
    while kpow < c:
        for d in st:
            d["sq"] = _dot(d["sq"].astype(BF16), bdiag(d["sq"]).astype(BF16))
        for d in st:
            d["p"] = d["p"] + _dot(d["p"].astype(BF16), bdiag(d["sq"]).astype(BF16))
        kpow *= 2
    for n, d in enumerate(st):
        r0 = n * c
        tmat = d["p"].astype(BF16)
        u_ref[0, r0:r0 + c, :] = _dot(tmat, bdiag_wide(d["vb"]).astype(BF16))
        w_ref[0, r0:r0 + c, :] = _dot(tmat, bdiag_wide(d["kbe"]).astype(BF16)).astype(w_ref.dtype)


def _gdn_front_kernel(x_ref, prev_ref, misc_ref, cw_ref, alog_ref, dtb_ref,
                      u_ref, w_ref, qd_ref, a_ref, kend_ref, egl_ref,
                      q_sc, k_sc, v_sc, kt_sc, gb_sc, gbt_sc, grow_sc, *, heads, chunk, n_chunks):
    _gdn_pre_kernel(x_ref, prev_ref, misc_ref, cw_ref, alog_ref, dtb_ref,
                    q_sc, k_sc, v_sc, kt_sc, gb_sc, gbt_sc, grow_sc, heads=heads, chunk=chunk)
    _gdn_local_kernel(q_sc, k_sc, v_sc, kt_sc, gb_sc, gbt_sc, grow_sc,
                      u_ref, w_ref, qd_ref, a_ref, kend_ref, egl_ref, heads=heads, chunk=chunk, n_chunks=n_chunks)


def _gdn_front(qkv, misc, w_conv, a_log, dt_bias, batch, seq, cb=8):
    width = qkv.shape[-1]
    heads = a_log.shape[0]
    chunk = GDN_CHUNK
    hd = heads * GDN_DK
    n_all = seq // chunk
    ts = cb * chunk
    x3 = qkv.reshape(batch, seq, width)
    m3 = misc.reshape(batch, seq, LANES)
    alog_row = jnp.zeros((1, LANES), F32).at[0, G_LANE:G_LANE + heads].set(a_log.astype(F32))
    dtb_row = jnp.zeros((1, LANES), F32).at[0, G_LANE:G_LANE + heads].set(dt_bias.astype(F32))
    tok = lambda b, s: (b, s, 0)
    ck = lambda b, s: (b, s, 0, 0)
    return pl.pallas_call(
        functools.partial(_gdn_front_kernel, heads=heads, chunk=chunk, n_chunks=cb),
        grid=(batch, n_all // cb),
        in_specs=[pl.BlockSpec((1, ts, width), tok),
                  pl.BlockSpec((1, HALO, width), lambda b, s: (b, jnp.maximum(s * (ts // HALO) - 1, 0), 0)),
                  pl.BlockSpec((1, ts, LANES), tok),
                  _const_spec(w_conv.shape), _const_spec((1, LANES)), _const_spec((1, LANES))],
        out_specs=(pl.BlockSpec((1, ts, hd), tok), pl.BlockSpec((1, ts, hd), tok), pl.BlockSpec((1, ts, hd), tok),
                   pl.BlockSpec((1, ts, heads * chunk), tok), pl.BlockSpec((1, cb, hd, chunk), ck),
                   pl.BlockSpec((1, cb, 8, LANES), ck)),
        out_shape=(jax.ShapeDtypeStruct((batch, seq, hd), F32), jax.ShapeDtypeStruct((batch, seq, hd), BF16),
                   jax.ShapeDtypeStruct((batch, seq, hd), BF16), jax.ShapeDtypeStruct((batch, seq, heads * chunk), BF16),
                   jax.ShapeDtypeStruct((batch, n_all, hd, chunk), BF16), jax.ShapeDtypeStruct((batch, n_all, 8, LANES), F32)),
        scratch_shapes=[pltpu.VMEM((1, ts, hd), BF16), pltpu.VMEM((1, ts, hd), BF16), pltpu.VMEM((1, ts, width - 2 * hd), BF16),
                        pltpu.VMEM((1, cb, hd, chunk), BF16), pltpu.VMEM((1, ts, LANES), F32),
                        pltpu.VMEM((1, cb, 8, chunk), F32), pltpu.VMEM((1, cb, 8, heads * chunk), F32)],
        compiler_params=_cparams(("parallel", "parallel"), VMEM_LIMIT),
        name="gdn_front",
    )(x3, x3, m3, w_conv.astype(F32), alog_row, dtb_row)


def _gdn_scan_kernel(u_ref, w_ref, qd_ref, a_ref, kend_ref, egl_ref, gz_ref, gn_ref, o_ref, st_ref, *, heads, chunk, n_chunks):
    c = chunk
    nb = u_ref.shape[0]

    @pl.when(pl.program_id(1) == 0)
    def _():
        st_ref[...] = jnp.zeros(st_ref.shape, F32)

    gn = gn_ref[...]
    lanes = [(b, h) for b in range(nb) for h in range(heads)]
    state = {bh: st_ref[bh[0], bh[1]] for bh in lanes}
    col = lambda h: slice(GDN_DV * h, GDN_DV * (h + 1))
    for n in range(n_chunks):
        r0 = n * c
        sb = {bh: state[bh].astype(BF16) for bh in lanes}
        v_new = {(b, h): u_ref[b, r0:r0 + c, col(h)] - _dot(w_ref[b, r0:r0 + c, col(h)], sb[(b, h)]) for b, h in lanes}
        vb = {bh: v_new[bh].astype(BF16) for bh in lanes}
        state = {(b, h): state[(b, h)] * egl_ref[b, n][h:h + 1, :] + _dot(kend_ref[b, n, col(h), :], vb[(b, h)]) for b, h in lanes}
        for b, h in lanes:
            o = _dot(qd_ref[b, r0:r0 + c, col(h)], sb[(b, h)]) + _dot(a_ref[b, r0:r0 + c, c * h:c * (h + 1)], vb[(b, h)])
            z = gz_ref[b, r0:r0 + c, col(h)].astype(F32)
            o_ref[b, r0:r0 + c, col(h)] = (_rms_rows(o, gn) * (z * jax.nn.sigmoid(z))).astype(o_ref.dtype)
    for b, h in lanes:
        st_ref[b, h] = state[(b, h)]


def _gdn_scan(u, w, qd, a, kend, egl, gz, g_out, cs=4):
    batch, seq, hd = u.shape
    nb = max(n for n in (4, 2, 1) if batch % n == 0)
    heads = hd // GDN_DV
    chunk = GDN_CHUNK
    n_all = seq // chunk
    ts = cs * chunk
    tok = lambda b, s: (b, s, 0)
    ck = lambda b, s: (b, s, 0, 0)
    return pl.pallas_call(
        functools.partial(_gdn_scan_kernel, heads=heads, chunk=chunk, n_chunks=cs),
        grid=(batch // nb, n_all // cs),
        in_specs=[pl.BlockSpec((nb, ts, hd), tok), pl.BlockSpec((nb, ts, hd), tok), pl.BlockSpec((nb, ts, hd), tok),
                  pl.BlockSpec((nb, ts, heads * chunk), tok), pl.BlockSpec((nb, cs, hd, chunk), ck),
                  pl.BlockSpec((nb, cs, 8, LANES), ck), pl.BlockSpec((nb, ts, hd), tok), _const_spec((1, GDN_DV))],
        out_specs=pl.BlockSpec((nb, ts, hd), tok),
        out_shape=jax.ShapeDtypeStruct((batch, seq, hd), BF16),
        scratch_shapes=[pltpu.VMEM((nb, heads, GDN_DK, GDN_DV), F32)],
        compiler_params=_cparams(("parallel", "arbitrary"), VMEM_LIMIT),
        name="gdn_scan",
    )(u, w, qd, a, kend, egl, gz.reshape(batch, seq, hd), g_out.reshape(1, GDN_DV).astype(F32))


def _mix_out_kernel(s5_ref, mla_ref, gdn_ref, x_ref, w_ref, mg_ref, g_ref, b_ref, o_ref, *, alpha, parts):
    w5 = s5_ref.shape[1]
    wm = mla_ref.shape[1]
    pm = x_ref.shape[0] // parts
    rows = lambda r: slice(r * pm, (r + 1) * pm)
    accs = []
    for r in range(parts):
        mla = _rms_rows(mla_ref[rows(r), :].astype(F32), mg_ref[...]).astype(BF16)
        accs.append(_dot(s5_ref[rows(r), :], w_ref[0:w5, :]) + _dot(mla, w_ref[w5:w5 + wm, :])
                    + _dot(gdn_ref[rows(r), :], w_ref[w5 + wm:, :]))
    for r in range(parts):
        o_ref[rows(r), :] = _layernorm_rows(alpha * x_ref[rows(r), :] + accs[r], g_ref[...], b_ref[...])


def _mix_kernel(ys_ref, wg_ref, bg_ref, gg_ref, mla_ref, gdn_ref, x_ref, w_ref, mg_ref, g_ref, b_ref, o_ref, ysc, s5_sc,
                *, alpha, parts, chunk):
    _s5_glu_kernel(ys_ref, wg_ref, bg_ref, gg_ref, s5_sc, ysc, chunk=chunk)
    _mix_out_kernel(s5_sc, mla_ref, gdn_ref, x_ref, w_ref, mg_ref, g_ref, b_ref, o_ref, alpha=alpha, parts=parts)


def _mix(ys, w_glu, b_glu, g_s5, o_mla, y_gdn, x2d, w_out, mla_gain, ln_g, ln_b, alpha, chunk, tm=512, parts=2):
    t, d = x2d.shape
    sets, _, width = ys.shape
    ws = sets * LANES
    row = lambda i: (i, 0)
    return pl.pallas_call(
        functools.partial(_mix_kernel, alpha=alpha, parts=parts, chunk=chunk),
        grid=(t // tm,),
        in_specs=[pl.BlockSpec((sets, tm // chunk, width), lambda i: (0, i, 0)), _const_spec((ws, ws)), _const_spec((1, ws)),
                  _const_spec((1, ws)),
                  pl.BlockSpec((tm, o_mla.shape[1]), row), pl.BlockSpec((tm, y_gdn.shape[1]), row), pl.BlockSpec((tm, d), row),
                  _const_spec(w_out.shape), _const_spec((1, o_mla.shape[1])), _const_spec((1, d)), _const_spec((1, d))],
        out_specs=pl.BlockSpec((tm, d), row),
        out_shape=jax.ShapeDtypeStruct((t, d), F32),
        scratch_shapes=[pltpu.VMEM((sets, tm, LANES), F32), pltpu.VMEM((tm, ws), BF16)],
        compiler_params=_cparams(("parallel",), VMEM_LIMIT),
        name="mix_out",
    )(ys, w_glu.astype(BF16), b_glu.reshape(1, ws).astype(F32), g_s5.reshape(1, ws).astype(F32), o_mla, y_gdn, x2d,
      w_out.astype(BF16), mla_gain.reshape(1, -1).astype(F32), ln_g.reshape(1, d).astype(F32), ln_b.reshape(1, d).astype(F32))


def _matmul_kernel(x_ref, w_ref, o_ref):
    o_ref[...] = _dot(x_ref[...].astype(BF16), w_ref[...]).astype(o_ref.dtype)


def _matmul(x, w, tm=256, tn=512):
    m, k = x.shape
    n = w.shape[1]
    return pl.pallas_call(
        _matmul_kernel,
        grid=(m // tm, n // tn),
        in_specs=[pl.BlockSpec((tm, k), lambda i, j: (i, 0)), pl.BlockSpec((k, tn), lambda i, j: (0, j))],
        out_specs=pl.BlockSpec((tm, tn), lambda i, j: (i, j)),
        out_shape=jax.ShapeDtypeStruct((m, n), F32),
        compiler_params=_cparams(("parallel", "parallel")),
        name="mem_kv_proj",
    )(x, w.astype(BF16))


def _xattn_kernel(x_ref, wq_ref, kt_ref, v_ref, wo_ref, g_ref, b_ref, wr_ref, o_ref, lg_ref, *, heads, alpha, parts):
    tm = x_ref.shape[0]
    pm = tm // parts
    nr = lg_ref.shape[0]
    nt = (((1,), (1,)), ((), ()))
    cols = lambda h: slice(XA_DH * h, XA_DH * (h + 1))
    xs = [x_ref[r * pm:(r + 1) * pm, :] for r in range(parts)]
    qs = [(_dot(x.astype(BF16), wq_ref[...]) * (XA_DH ** -0.5)).astype(BF16) for x in xs]
    ss = [[_dot(q[:, cols(h)], kt_ref[0, cols(h), :]) for h in range(heads)] for q in qs]
    ps = []
    for part in ss:
        row = []
        for s in part:
            e = jnp.exp(s - jnp.max(s, axis=-1, keepdims=True))
            row.append((e / jnp.sum(e, axis=-1, keepdims=True)).astype(BF16))
        ps.append(row)
    os = [jnp.concatenate([_dot(p[h], v_ref[0, :, cols(h)]) for h in range(heads)], axis=1).astype(BF16) for p in ps]
    ys = [_layernorm_rows(alpha * xs[r] + _dot(os[r], wo_ref[...]), g_ref[...], b_ref[...]) for r in range(parts)]
    w_hi_lo = wr_ref[...]
    for r in range(parts):
        o_ref[r * pm:(r + 1) * pm, :] = ys[r]
        y_hi = ys[r].astype(BF16)
        y_lo = (ys[r] - y_hi.astype(F32)).astype(BF16)
        p1 = lax.dot_general(w_hi_lo, y_hi, nt, preferred_element_type=F32)
        p2 = lax.dot_general(w_hi_lo[:nr], y_lo, nt, preferred_element_type=F32)
        lg_ref[:, r * pm:(r + 1) * pm] = p1[:nr] + p1[nr:] + p2


def _xattn(x2d, kt, v, w_q, w_o, ln_g, ln_b, w_router_t, seq, alpha, tm=512, parts=2):
    t, d = x2d.shape
    width = w_q.shape[1]
    heads = width // XA_DH
    mlen = v.shape[1]
    per = seq // tm
    nr = w_router_t.shape[0]
    w_hi = w_router_t.astype(BF16)
    w_lo = (w_router_t - w_hi.astype(F32)).astype(BF16)
    row = lambda i: (i, 0)
    return pl.pallas_call(
        functools.partial(_xattn_kernel, heads=heads, alpha=alpha, parts=parts),
        grid=(t // tm,),
        in_specs=[pl.BlockSpec((tm, d), row), _const_spec((d, width)),
                  pl.BlockSpec((1, width, mlen), lambda i: (i // per, 0, 0)),
                  pl.BlockSpec((1, mlen, width), lambda i: (i // per, 0, 0)),
                  _const_spec((width, d)), _const_spec((1, d)), _const_spec((1, d)), _const_spec((2 * nr, d))],
        out_specs=(pl.BlockSpec((tm, d), row), pl.BlockSpec((nr, tm), lambda i: (0, i))),
        out_shape=(jax.ShapeDtypeStruct((t, d), F32), jax.ShapeDtypeStruct((nr, t), F32)),
        compiler_params=_cparams(("parallel",), VMEM_LIMIT),
        name="xattn",
    )(x2d, w_q.astype(BF16), kt, v, w_o.astype(BF16), ln_g.reshape(1, d).astype(F32), ln_b.reshape(1, d).astype(F32),
      jnp.concatenate([w_hi, w_lo], axis=0))


def _router_kernel(lg_ref, bias_ref, eid_ref, gate_ref):
    lg = lg_ref[...] + bias_ref[...]
    ng, ne = MOE_GROUPS, MOE_PER_GROUP
    grp = [lg[g:g + 1, :] for g in range(ng)]
    gmax = functools.reduce(jnp.maximum, grp)
    gexp = [jnp.exp(r - gmax) for r in grp]
    gsum = functools.reduce(lambda a, b: a + b, gexp)
    pg = [e / gsum for e in gexp]
    best, gsel = pg[0], jnp.zeros(pg[0].shape, jnp.int32)
    for g in range(1, ng):
        better = pg[g] > best
        gsel = jnp.where(better, g, gsel)
        best = jnp.where(better, pg[g], best)
    le = []
    for e in range(ne):
        r = lg[ng + e:ng + e + 1, :]
        for g in range(1, ng):
            r = jnp.where(gsel == g, lg[ng + g * ne + e:ng + g * ne + e + 1, :], r)
        le.append(r)
    emax = functools.reduce(jnp.maximum, le)
    eexp = [jnp.exp(r - emax) for r in le]
    esum = functools.reduce(lambda a, b: a + b, eexp)
    pe = [e / esum for e in eexp]
    sel, val = [], []
    for k in range(MOE_TOPK):
        bv, bi = None, None
        for e in range(ne):
            cand = pe[e]
            for prev in sel:
                cand = jnp.where(prev == e, -1.0, cand)
            if bv is None:
                bv, bi = cand, jnp.zeros(cand.shape, jnp.int32)
            else:
                better = cand > bv
                bi = jnp.where(better, e, bi)
                bv = jnp.where(better, cand, bv)
        sel.append(bi)
        val.append(bv)
    tot = functools.reduce(lambda a, b: a + b, val)
    zero_i = jnp.zeros((8 - MOE_TOPK,) + sel[0].shape[1:], jnp.int32)
    zero_f = jnp.zeros((8 - MOE_TOPK,) + sel[0].shape[1:], F32)
    eid_ref[...] = jnp.concatenate([gsel * ne + s for s in sel] + [zero_i], axis=0)
    gate_ref[...] = jnp.concatenate([best * v / tot for v in val] + [zero_f], axis=0)


def _router(lg_t, bias_col, tn=2048):
    nr, t = lg_t.shape
    tn = min(tn, t)
    return pl.pallas_call(
        _router_kernel,
        grid=(t // tn,),
        in_specs=[pl.BlockSpec((nr, tn), lambda i: (0, i)), _const_spec((nr, 1))],
        out_specs=(pl.BlockSpec((8, tn), lambda i: (0, i)), pl.BlockSpec((8, tn), lambda i: (0, i))),
        out_shape=(jax.ShapeDtypeStruct((8, t), jnp.int32), jax.ShapeDtypeStruct((8, t), F32)),
        compiler_params=_cparams(("parallel",)),
        name="router",
    )(lg_t, bias_col)


def _start_row_gather(idx_ref, src_hbm, buf, sem):
    for r in range(buf.shape[0]):
        pltpu.make_async_copy(src_hbm.at[pl.ds(idx_ref[r], 1)], buf.at[pl.ds(r, 1)], sem).start(priority=r % 2)


def _wait_row_gather(buf, sem):
    pltpu.make_async_copy(buf, buf, sem).wait()


def _expert_kernel(be_ref, nu_ref, idx0_ref, idxn_ref, x_hbm, wgu_ref, wd_ref, y_ref, xbuf, sem, wgu_sc, wd_sc, *, ff):
    i = pl.program_id(0)
    n_used = nu_ref[0]
    slot = lax.rem(i, 2)
    changed = jnp.logical_or(i == 0, be_ref[i] != be_ref[jnp.maximum(i - 1, 0)])

    @pl.when(changed)
    def _():
        wgu_sc[...] = wgu_ref[0, 0].astype(BF16)
        wd_sc[...] = wd_ref[0, 0].astype(BF16)

    @pl.when(i == 0)
    def _():
        _start_row_gather(idx0_ref, x_hbm, xbuf.at[0], sem.at[0])

    @pl.when(i < n_used)
    def _():
        _wait_row_gather(xbuf.at[slot], sem.at[slot])
        x = xbuf[slot].astype(BF16)
        _start_row_gather(idxn_ref, x_hbm, xbuf.at[1 - slot], sem.at[1 - slot])
        gu = _dot(x, wgu_sc[...])
        gate = gu[:, :ff]
        h = gate * jax.nn.sigmoid(gate) * gu[:, ff:]
        y_ref[...] = _dot(h.astype(BF16), wd_sc[...])

    @pl.when(i == n_used)
    def _():
        _wait_row_gather(xbuf.at[slot], sem.at[slot])

    @pl.when(i >= n_used)
    def _():
        y_ref[...] = jnp.zeros(y_ref.shape, y_ref.dtype)


def _experts(x2d, row_tok, blk_exp, n_used, w_gate_up, w_down, layer, rb):
    rows = row_tok.shape[0]
    d = x2d.shape[1]
    ff = w_down.shape[2]
    nblk = rows // rb
    grid_spec = pltpu.PrefetchScalarGridSpec(
        num_scalar_prefetch=2,
        grid=(nblk,),
        in_specs=[pl.BlockSpec((rb,), lambda i, be, nu: (0,), memory_space=pltpu.SMEM),
                  pl.BlockSpec((rb,), lambda i, be, nu: (jnp.minimum(i + 1, nblk - 1),), memory_space=pltpu.SMEM),
                  pl.BlockSpec(memory_space=pl.ANY),
                  pl.BlockSpec((1, 1, d, 2 * ff), lambda i, be, nu: (layer, be[i], 0, 0)),
                  pl.BlockSpec((1, 1, ff, d), lambda i, be, nu: (layer, be[i], 0, 0))],
        out_specs=pl.BlockSpec((rb, d), lambda i, be, nu: (i, 0)),
        scratch_shapes=[pltpu.VMEM((2, rb, d), x2d.dtype), pltpu.SemaphoreType.DMA((2,)),
                        pltpu.VMEM((d, 2 * ff), BF16), pltpu.VMEM((ff, d), BF16)],
    )
    return pl.pallas_call(
        functools.partial(_expert_kernel, ff=ff),
        grid_spec=grid_spec,
        out_shape=jax.ShapeDtypeStruct((rows, d), F32),
        compiler_params=_cparams(("arbitrary",), VMEM_LIMIT),
        name="experts",
    )(blk_exp, n_used, row_tok, row_tok, x2d, w_gate_up, w_down)


def _moe_plan(eid, n_exp, rb):
    t, topk = eid.shape
    m = t * topk
    flat_e = eid.reshape(m)
    onehot = (flat_e[:, None] == jnp.arange(n_exp, dtype=jnp.int32)[None, :]).astype(jnp.int32)
    csum = jnp.cumsum(onehot, axis=0)
    counts = csum[-1]
    pcounts = (counts + rb - 1) // rb * rb
    pends = jnp.cumsum(pcounts)
    pstarts = pends - pcounts
    dest = jnp.sum(onehot * (pstarts[None, :] + csum - 1), axis=1)
    rows = m + n_exp * rb
    row_tok = (jnp.arange(rows, dtype=jnp.int32) % t).at[dest].set(
        jnp.arange(m, dtype=jnp.int32) // topk, unique_indices=True, mode='promise_in_bounds')
    nblk = rows // rb
    blk_start = jnp.arange(nblk, dtype=jnp.int32) * rb
    blk_exp = jnp.minimum(jnp.sum((pends[None, :] <= blk_start[:, None]).astype(jnp.int32), axis=1), n_exp - 1)
    n_used = (pends[-1] // rb).astype(jnp.int32).reshape(1)
    return dest.astype(jnp.int32), row_tok, blk_exp, n_used


def _moe_out_kernel(*refs, alpha):
    idx0 = refs[:MOE_TOPK]
    idxn = refs[MOE_TOPK:2 * MOE_TOPK]
    y_hbm, x_ref, gate_ref, g_ref, b_ref, o_ref, ybuf, sem = refs[2 * MOE_TOPK:]
    i = pl.program_id(0)
    slot = lax.rem(i, 2)

    @pl.when(i == 0)
    def _():
        for k in range(MOE_TOPK):
            _start_row_gather(idx0[k], y_hbm, ybuf.at[0, k], sem.at[0])

    @pl.when(i + 1 < pl.num_programs(0))
    def _():
        for k in range(MOE_TOPK):
            _start_row_gather(idxn[k], y_hbm, ybuf.at[1 - slot, k], sem.at[1 - slot])

    _wait_row_gather(ybuf.at[slot], sem.at[slot])
    gate = gate_ref[...]
    ffn = gate[:, 0:1] * ybuf[slot, 0]
    for k in range(1, MOE_TOPK):
        ffn = ffn + gate[:, k:k + 1] * ybuf[slot, k]
    o_ref[...] = _layernorm_rows(alpha * x_ref[...] + ffn, g_ref[...], b_ref[...])


def _moe_out(yr, dest_k, x2d, gate, ln_g, ln_b, alpha, tm=256):
    t, d = x2d.shape
    row = lambda i: (i, 0)
    nt = t // tm
    first = [pl.BlockSpec((tm,), functools.partial(lambda k, i: (k * nt,), k), memory_space=pltpu.SMEM) for k in range(MOE_TOPK)]
    nxt = [pl.BlockSpec((tm,), functools.partial(lambda k, i: (k * nt + jnp.minimum(i + 1, nt - 1),), k), memory_space=pltpu.SMEM)
           for k in range(MOE_TOPK)]
    return pl.pallas_call(
        functools.partial(_moe_out_kernel, alpha=alpha),
        grid=(nt,),
        in_specs=first + nxt + [pl.BlockSpec(memory_space=pl.ANY), pl.BlockSpec((tm, d), row),
                                pl.BlockSpec((tm, gate.shape[1]), row), _const_spec((1, d)), _const_spec((1, d))],
        out_specs=pl.BlockSpec((tm, d), row),
        out_shape=jax.ShapeDtypeStruct((t, d), F32),
        scratch_shapes=[pltpu.VMEM((2, MOE_TOPK, tm, d), yr.dtype), pltpu.SemaphoreType.DMA((2,))],
        compiler_params=_cparams(("arbitrary",), VMEM_LIMIT),
        name="moe_out",
    )(*([dest_k] * (2 * MOE_TOPK)), yr, x2d, gate, ln_g.reshape(1, d).astype(F32), ln_b.reshape(1, d).astype(F32))


def _moe(x2, lg_t, b_group, b_expert, w_gate_up, w_down, layer, ln_g, ln_b, alpha):
    t, d = x2.shape
    n_exp = w_gate_up.shape[1]
    nr = lg_t.shape[0]
    bias = jnp.zeros((nr, 1), F32).at[:MOE_GROUPS + n_exp, 0].set(jnp.concatenate([b_group, b_expert]).astype(F32))
    eid_t, gate_t = _router(lg_t, bias)
    eid = eid_t[:MOE_TOPK].T
    gate = gate_t[:MOE_TOPK].T
    dest, row_tok, blk_exp, n_used = _moe_plan(eid, n_exp, MOE_ROW_BLOCK)
    yr = _experts(x2, row_tok, blk_exp, n_used, w_gate_up, w_down, layer, MOE_ROW_BLOCK)
    return _moe_out(yr, dest.reshape(t, MOE_TOPK).T.reshape(-1), x2, gate, ln_g, ln_b, alpha)


def kernel(x, mem, positions, w_in, s5_lambda_re, s5_lambda_im, s5_log_step, s5_b_re, s5_b_im, s5_c_re, s5_c_im, s5_d, s5_w_glu, s5_b_glu, s5_out_norm, mla_q_norm, mla_w_uq, mla_kv_norm, mla_w_ukv, mla_out_norm, gdn_conv, gdn_a_log, gdn_dt_bias, gdn_out_norm, w_out, ln1_g, ln1_b, xa_w_q, xa_w_k, xa_w_v, xa_w_o, ln2_g, ln2_b, moe_w_group, moe_b_group, moe_w_expert, moe_b_expert, moe_w_gate_up, moe_w_down, ln3_g, ln3_b):
    batch, seq, d = x.shape
    t = batch * seq
    depth = w_in.shape[0]
    alpha = (2 * depth) ** 0.25
    mlen = mem.shape[1]
    s5_w = s5_w_glu.shape[1]
    rank_q = mla_w_uq.shape[1]
    rank_kv = mla_w_ukv.shape[1]
    g_heads = gdn_a_log.shape[1]
    g_qk = g_heads * GDN_DK
    g_v = gdn_conv.shape[2] - 2 * g_qk
    o_kr = s5_w + rank_q + rank_kv
    o_gq = o_kr + MLA_ROPE
    o_gz = o_gq + 2 * g_qk + g_v
    o_ga = o_gz + g_v
    widths = (s5_w, rank_q, rank_kv, LANES, 2 * g_qk + g_v, g_v)
    assert G_LANE == MLA_ROPE and B_LANE == G_LANE + g_heads and o_ga + 2 * g_heads == w_in.shape[2]

    cos_t, sin_t = _rope_tables(positions)
    mem2 = mem.reshape(batch * mlen, d)
    xt = x.reshape(t, d)
    for l in range(depth):
        w = w_in[l]
        w_packed = jnp.concatenate(
            [w[:, :o_gq], w[:, o_ga:], jnp.zeros((d, LANES - MLA_ROPE - 2 * g_heads), w.dtype), w[:, o_gq:o_ga]], axis=1).astype(BF16)
        u, misc, qkv, gz, qt, kk, vt = _in_mla(xt, w_packed, widths, (BF16, BF16, BF16, F32, BF16, BF16), S5_CHUNK, cos_t, sin_t,
                                               mla_q_norm[l], mla_kv_norm[l], mla_w_uq[l], mla_w_ukv[l], batch, seq)

        tables = _s5_tables(s5_lambda_re[l], s5_lambda_im[l], s5_log_step[l], s5_b_re[l], s5_b_im[l], s5_c_re[l],
                            s5_c_im[l], s5_d[l], S5_CHUNK, seq // S5_CHUNK)
        ys5 = _s5_scan(u, tables, batch, seq, S5_CHUNK)

        o_mla = _mla_attention(qt, kk, vt).reshape(t, -1)

        local = _gdn_front(qkv, misc, gdn_conv[l], gdn_a_log[l], gdn_dt_bias[l], batch, seq)
        y_gdn = _gdn_scan(*local, gz, gdn_out_norm[l]).reshape(t, -1)

        x1 = _mix(ys5, s5_w_glu[l], s5_b_glu[l], s5_out_norm[l], o_mla, y_gdn, xt, w_out[l], mla_out_norm[l], ln1_g[l], ln1_b[l],
                  alpha, S5_CHUNK)

        kv_mem = _matmul(mem2, jnp.concatenate([xa_w_k[l], xa_w_v[l]], axis=1))
        xa_w = xa_w_k.shape[2]
        kt_mem = kv_mem[:, :xa_w].reshape(batch, mlen, xa_w).transpose(0, 2, 1).astype(BF16)
        v_mem = kv_mem[:, xa_w:].reshape(batch, mlen, xa_w).astype(BF16)
        n_route = MOE_GROUPS + moe_w_expert.shape[2]
        w_router_t = jnp.pad(jnp.concatenate([moe_w_group[l], moe_w_expert[l]], axis=1).T.astype(F32),
                             ((0, (-n_route) % 8), (0, 0)))
        x2, lg_t = _xattn(x1, kt_mem, v_mem, xa_w_q[l], xa_w_o[l], ln2_g[l], ln2_b[l], w_router_t, seq, alpha)

        xt = _moe(x2, lg_t, moe_b_group[l], moe_b_expert[l], moe_w_gate_up, moe_w_down, l, ln3_g[l], ln3_b[l], alpha)
    return xt.reshape(batch, seq, d)
```
